```python
import math
import jax
import jax.numpy as jnp
from jax import lax
import numpy as np

D_MODEL = 2048
BATCH = 2
SEQ = 4096
DEPTH = 4
DEC_BATCH = 8
DEC_SEQ = 1
PAST_LEN = 16384
PAGE_SIZE = 128

N_MIXERS = 3
N_NSA = (DEPTH + 2) // 3
N_HGRN = (DEPTH + 1) // 3
N_MLSTM = DEPTH // 3

NSA_HEAD_DIM = 128
NSA_HEADS = D_MODEL // NSA_HEAD_DIM
NSA_KV_HEADS = 4
NSA_GROUP = NSA_HEADS // NSA_KV_HEADS
NSA_Q_WIDTH = NSA_HEADS * NSA_HEAD_DIM
NSA_KV_WIDTH = 6 * NSA_KV_HEADS * NSA_HEAD_DIM
NSA_IN = NSA_Q_WIDTH + NSA_KV_WIDTH + 3 * NSA_HEADS
CMP_STRIDE = 16
CMP_RATIO = 2
CMP_BLOCK = CMP_STRIDE * CMP_RATIO
CMP_HIDDEN = NSA_HEAD_DIM
SLC_BLOCK = 64
SLC_TOP_N = 16
FORCE_BONUS = 1000.0
WINDOW = 512
Q_BLOCK = 128
ROPE_THETA = 500000.0
ROPE_DIM = NSA_HEAD_DIM // 4

HG_HEAD_DIM = 128
HG_HEADS = D_MODEL // HG_HEAD_DIM
HG_WIDTH = HG_HEADS * HG_HEAD_DIM

ML_HEADS = 8
ML_DK = D_MODEL // (2 * ML_HEADS)
ML_DV = D_MODEL // ML_HEADS
ML_IN = 2 * ML_HEADS * ML_DK + 2 * ML_HEADS * ML_DV + 2 * ML_HEADS
ML_GATE_CAP = 15.0
CHUNK = 64

D_FF = (11 * D_MODEL) // 4
CONV_W = 3

EPS = 1e-6
NEG = -1e30
F32 = jnp.float32

kernel_name = 'hybrid_nsa_hgrn2_mlstm_convffn_step'


def rmsnorm(x, g):
    xf = x.astype(F32)
    y = xf * lax.rsqrt(jnp.mean(xf * xf, axis=-1, keepdims=True) + EPS)
    return (y * g.astype(F32)).astype(x.dtype)


def head_rmsnorm(x, g, n_heads):
    shp = x.shape
    xf = x.astype(F32).reshape(shp[:-1] + (n_heads, shp[-1] // n_heads))
    xf = xf * lax.rsqrt(jnp.mean(xf * xf, axis=-1, keepdims=True) + EPS)
    return xf.reshape(shp) * g.astype(F32)


def partial_rope(x, pos):
    half = ROPE_DIM // 2
    inv = ROPE_THETA ** (-jnp.arange(half, dtype=F32) / half)
    ang = pos.astype(F32)[:, None] * inv[None, :]
    cos = jnp.cos(ang)[None, :, None, :]
    sin = jnp.sin(ang)[None, :, None, :]
    xr = x[..., :ROPE_DIM].astype(F32)
    x1, x2 = xr[..., :half], xr[..., half:]
    rot = jnp.concatenate([x1 * cos - x2 * sin, x2 * cos + x1 * sin], axis=-1)
    return jnp.concatenate([rot.astype(x.dtype), x[..., ROPE_DIM:]], axis=-1)


def group_heads(q):
    B, T = q.shape[:2]
    return q.reshape(B, T, NSA_KV_HEADS, NSA_GROUP, NSA_HEAD_DIM).transpose(0, 2, 3, 1, 4)


def ungroup_heads(o):
    B, T = o.shape[0], o.shape[3]
    return o.transpose(0, 3, 1, 2, 4).reshape(B, T, NSA_HEADS, NSA_HEAD_DIM)


def masked_attention(q, k, v, mask):
    s = jnp.einsum('...kgtd,...skd->...kgts', q, k, preferred_element_type=F32) * (NSA_HEAD_DIM ** -0.5)
    p = jax.nn.softmax(jnp.where(mask, s, NEG), axis=-1) * mask
    return jnp.einsum('...kgts,...skd->...kgtd', p.astype(v.dtype), v)


def compress_blocks(rows, w1, pe, w2):
    B, L, KV, HD = rows.shape
    n_seg = L // CMP_STRIDE
    n_cmp = n_seg - CMP_RATIO + 1
    seg = rows[:, :n_seg * CMP_STRIDE].reshape(B, n_seg, CMP_STRIDE, KV, HD)
    part = jnp.einsum('bnjkd,rjde->rbnke', seg, w1.reshape(CMP_RATIO, CMP_STRIDE, HD, CMP_HIDDEN))
    hid = jnp.einsum('ld,lde->e', pe, w1)
    for r in range(CMP_RATIO):
        hid = hid + part[r, :, r:r + n_cmp]
    return jax.nn.gelu(hid) @ w2


def selected_attention(qg, idx, pos, k_b, v_b):
    B, KV, G, T, HD = qg.shape
    n = idx.shape[-1]
    tb = math.gcd(T, Q_BLOCK)
    nb = T // tb
    q_blk = jnp.moveaxis(qg.reshape(B, KV, G, nb, tb, HD), 3, 0)
    i_blk = jnp.moveaxis(idx.reshape(B, KV, nb, tb, n), 2, 0)
    p_blk = pos.reshape(nb, tb)
    gather = jax.vmap(jax.vmap(lambda blocks, ids: blocks[ids]))

    def one_block(args):
        qb, ib, pb = args
        kg = gather(k_b, ib).reshape(B, KV, tb, n * SLC_BLOCK, HD)
        vg = gather(v_b, ib).reshape(B, KV, tb, n * SLC_BLOCK, HD)
        kpos = (ib[..., None] * SLC_BLOCK + jnp.arange(SLC_BLOCK)).reshape(B, KV, tb, n * SLC_BLOCK)
        mask = (kpos <= pb[:, None])[:, :, None]
        s = jnp.einsum('bkgtd,bktsd->bkgts', qb, kg, preferred_element_type=F32) * (HD ** -0.5)
        p = jax.nn.softmax(jnp.where(mask, s, NEG), axis=-1)
        return jnp.einsum('bkgts,bktsd->bkgtd', p.astype(vg.dtype), vg)

    o = lax.map(one_block, (q_blk, i_blk, p_blk))
    return jnp.moveaxis(o, 0, 3).reshape(B, KV, G, T, HD)


def nsa_global(q_raw, q_rot, ctx, pos, w1, pe, w2):
    B, L = ctx.shape[:2]
    k_c = compress_blocks(ctx[:, :, 0], w1[0], pe[0], w2[0])
    v_c = compress_blocks(ctx[:, :, 1], w1[1], pe[1], w2[1])
    n_cmp = k_c.shape[1]
    cmp_start = jnp.arange(n_cmp) * CMP_STRIDE
    valid = (cmp_start + CMP_BLOCK - 1)[None, :] <= pos[:, None]
    s = jnp.einsum('bkgtd,bmkd->bkgtm', q_raw, k_c, preferred_element_type=F32) * (NSA_HEAD_DIM ** -0.5)
    p = jax.nn.softmax(jnp.where(valid, s, NEG), axis=-1) * valid
    o_cmp = jnp.einsum('bkgtm,bmkd->bkgtd', p.astype(v_c.dtype), v_c)
    n_slc = -(-L // SLC_BLOCK)
    slc_start = jnp.arange(n_slc) * SLC_BLOCK
    overlap = ((cmp_start[:, None] <= slc_start[None, :] + SLC_BLOCK - 1)
               & (cmp_start[:, None] + CMP_BLOCK - 1 >= slc_start[None, :])).astype(F32)
    imp = jnp.einsum('bkgtm,mj->bktj', p, overlap)
    cur = (pos // SLC_BLOCK)[:, None]
    jj = jnp.arange(n_slc)[None, :]
    forced = (jj == 0) | (jj == cur) | (jj == cur - 1)
    causal = slc_start[None, :] <= pos[:, None]
    score = jnp.where(causal, imp + FORCE_BONUS * forced, -1.0)
    _, idx = lax.top_k(score, min(SLC_TOP_N, n_slc))
    pad = n_slc * SLC_BLOCK - L

    def to_blocks(r):
        r = jnp.pad(r, ((0, 0), (0, pad), (0, 0), (0, 0)))
        return r.reshape(B, n_slc, SLC_BLOCK, NSA_KV_HEADS, NSA_HEAD_DIM).transpose(0, 3, 1, 2, 4)

    o_slc = selected_attention(q_rot, idx, pos, to_blocks(ctx[:, :, 2]), to_blocks(ctx[:, :, 3]))
    return o_cmp, o_slc


def window_banded(qg, k, v):
    B, KV, G, T, HD = qg.shape
    tb = math.gcd(T, Q_BLOCK)
    nb = T // tb
    span = tb + WINDOW
    padw = ((0, 0), (WINDOW, 0), (0, 0), (0, 0))
    idx = (jnp.arange(nb) * tb)[:, None] + jnp.arange(span)[None, :]
    kb = jnp.pad(k, padw)[:, idx]
    vb = jnp.pad(v, padw)[:, idx]
    kpos = idx - WINDOW
    qpos = (jnp.arange(nb) * tb)[:, None] + jnp.arange(tb)[None, :]
    diff = qpos[:, :, None] - kpos[:, None, :]
    mask = (diff >= 0) & (diff < WINDOW) & (kpos[:, None, :] >= 0)
    qb = qg.reshape(B, KV, G, nb, tb, HD).transpose(0, 3, 1, 2, 4, 5)
    o = masked_attention(qb, kb, vb, mask[None, :, None, None])
    return o.transpose(0, 2, 3, 1, 4, 5).reshape(B, KV, G, T, HD)


def nsa_mixer(h, pos, a, P, past, past_len):
    B, T, _ = h.shape
    z = h @ P['nsa_w_in'][a]
    q = z[..., :NSA_Q_WIDTH].reshape(B, T, NSA_HEADS, NSA_HEAD_DIM)
    kv = z[..., NSA_Q_WIDTH:NSA_Q_WIDTH + NSA_KV_WIDTH].reshape(B, T, 6, NSA_KV_HEADS, NSA_HEAD_DIM)
    gates = jax.nn.sigmoid(z[..., NSA_Q_WIDTH + NSA_KV_WIDTH:].astype(F32)).reshape(B, T, NSA_HEADS, 3)
    q_rot = group_heads(partial_rope(q, pos))
    q_raw = group_heads(q)
    rows = jnp.stack([kv[:, :, 0], kv[:, :, 1], partial_rope(kv[:, :, 2], pos), kv[:, :, 3]], axis=2)
    win = jnp.stack([partial_rope(kv[:, :, 4], pos), kv[:, :, 5]], axis=2)
    if past is None:
        ctx = rows
        o_win = window_banded(q_rot, win[:, :, 0], win[:, :, 1])
        new_win = win[:, -min(WINDOW, T):]
    else:
        past_rows = past['cache_nsa_kv'][past['page_table'], :, a]
        n_b, n_p, p_s = past_rows.shape[:3]
        past_rows = past_rows.reshape((n_b, n_p * p_s) + past_rows.shape[3:]).astype(rows.dtype)
        ctx = jnp.concatenate([past_rows, rows], axis=1)
        buf = jnp.concatenate([past['state_nsa_win'][a].astype(win.dtype), win], axis=1)
        n_buf = buf.shape[1]
        k_pos = past_len + T - n_buf + jnp.arange(n_buf)
        diff = pos[:, None] - k_pos[None, :]
        mask = (diff >= 0) & (diff < WINDOW)
        o_win = masked_attention(q_rot, buf[:, :, 0], buf[:, :, 1], mask)
        new_win = buf[:, -min(WINDOW, n_buf):]
    o_cmp, o_slc = nsa_global(q_raw, q_rot, ctx, pos, P['nsa_cmp_w1'][a], P['nsa_cmp_pe'][a], P['nsa_cmp_w2'][a])
    mix = (gates[..., 0:1] * ungroup_heads(o_cmp) + gates[..., 1:2] * ungroup_heads(o_slc)
           + gates[..., 2:3] * ungroup_heads(o_win))
    y = mix.reshape(B, T, NSA_Q_WIDTH).astype(h.dtype) @ P['nsa_w_out'][a]
    return y, rows, new_win


def hgrn_scan(q, k, v, logf, S0):
    B, H, T, DK = q.shape
    DV = v.shape[-1]
    c = math.gcd(T, CHUNK)
    n = T // c

    def chunks(t):
        return jnp.moveaxis(t.reshape(B, H, n, c, t.shape[-1]), 2, 0)

    tri = jnp.tril(jnp.ones((c, c), dtype=bool))

    def step(S, xs):
        qc, kc, vc, lc = xs
        b = jnp.cumsum(lc, axis=2)
        inter = jnp.einsum('bhtk,bhkv->bhtv', qc * jnp.exp(b), S)
        dec = jnp.exp(jnp.where(tri[:, :, None], b[:, :, :, None, :] - b[:, :, None, :, :], NEG))
        att = jnp.sum(qc[:, :, :, None, :] * kc[:, :, None, :, :] * dec, axis=-1)
        out = inter + jnp.einsum('bhts,bhsv->bhtv', att, vc)
        bl = b[:, :, -1:, :]
        S = jnp.exp(bl[:, :, 0])[..., None] * S + jnp.einsum('bhsk,bhsv->bhkv', kc * jnp.exp(bl - b), vc)
        return S, out

    S, out = lax.scan(step, S0, (chunks(q), chunks(k), chunks(v), chunks(logf)))
    return jnp.moveaxis(out, 0, 2).reshape(B, H, T, DV), S


def hgrn_mixer(h, lb, a, P, S0):
    B, T, _ = h.shape
    z = (h @ P['hgrn_w_in'][a]).astype(F32)
    qz, fz, iz, gz = jnp.split(z, 4, axis=-1)
    f = lb + (1.0 - lb) * jax.nn.sigmoid(fz)

    def heads(t):
        return t.reshape(B, T, HG_HEADS, HG_HEAD_DIM).transpose(0, 2, 1, 3)

    o, S = hgrn_scan(heads(jax.nn.silu(qz)), heads(1.0 - f), heads(iz), heads(jnp.log(f)), S0)
    o = o.transpose(0, 2, 1, 3).reshape(B, T, HG_WIDTH)
    o = head_rmsnorm(o, P['hgrn_norm'][a], HG_HEADS) * jax.nn.silu(gz)
    return o.astype(h.dtype) @ P['hgrn_w_out'][a], S


def mlstm_scan(q, k, v, ig, logf, C0, n0, m0):
    B, H, T, DK = q.shape
    DV = v.shape[-1]
    c = math.gcd(T, CHUNK)
    n = T // c

    def chunks(t):
        return jnp.moveaxis(t.reshape((B, H, n, c) + t.shape[3:]), 2, 0)

    tri = jnp.tril(jnp.ones((c, c), dtype=bool))

    def step(carry, xs):
        C, nv, m = carry
        qc, kc, vc, ic, lc = xs
        b = jnp.cumsum(lc, axis=-1)
        logw = jnp.where(tri, b[..., :, None] - b[..., None, :] + ic[..., None, :], NEG)
        m_prev = b + m[..., None]
        m_t = jnp.maximum(m_prev, jnp.max(logw, axis=-1))
        w = jnp.exp(logw - m_t[..., None])
        w0 = jnp.exp(m_prev - m_t)
        sqk = jnp.einsum('bhtd,bhsd->bhts', qc, kc) * w
        num = w0[..., None] * jnp.einsum('bhtk,bhkv->bhtv', qc, C) + jnp.einsum('bhts,bhsv->bhtv', sqk, vc)
        den = w0 * jnp.einsum('bhtk,bhk->bht', qc, nv) + jnp.sum(sqk, axis=-1)
        hc = num / jnp.maximum(jnp.abs(den), jnp.exp(-m_t))[..., None]
        m_new = m_t[..., -1]
        wk = jnp.exp(logw[..., -1, :] - m_new[..., None])
        decay = jnp.exp(b[..., -1] + m - m_new)
        C = decay[..., None, None] * C + jnp.einsum('bhsk,bhsv->bhkv', kc * wk[..., None], vc)
        nv = decay[..., None] * nv + jnp.einsum('bhs,bhsk->bhk', wk, kc)
        return (C, nv, m_new), hc

    (C, nv, m), out = lax.scan(step, (C0, n0, m0), (chunks(q), chunks(k), chunks(v), chunks(ig), chunks(logf)))
    return jnp.moveaxis(out, 0, 2).reshape(B, H, T, DV), C, nv, m


def mlstm_mixer(h, a, P, C0, n0, m0):
    B, T, _ = h.shape
    qk = ML_HEADS * ML_DK
    vw = ML_HEADS * ML_DV
    z = h @ P['ml_w_in'][a]

    def heads(t, d):
        return t.astype(F32).reshape(B, T, ML_HEADS, d).transpose(0, 2, 1, 3)

    q = heads(z[..., :qk], ML_DK)
    k = heads(z[..., qk:2 * qk], ML_DK) * (ML_DK ** -0.5)
    v = heads(z[..., 2 * qk:2 * qk + vw], ML_DV)
    o_gate = jax.nn.sigmoid(z[..., 2 * qk + vw:2 * qk + 2 * vw].astype(F32))
    g = z[..., 2 * qk + 2 * vw:].astype(F32) + P['ml_b_gate'][a].astype(F32)
    g = ML_GATE_CAP * jnp.tanh(g / ML_GATE_CAP)
    ig = g[..., :ML_HEADS].transpose(0, 2, 1)
    logf = jax.nn.log_sigmoid(g[..., ML_HEADS:]).transpose(0, 2, 1)
    hc, C, nv, m = mlstm_scan(q, k, v, ig, logf, C0, n0, m0)
    y = o_gate * hc.transpose(0, 2, 1, 3).reshape(B, T, vw)
    return y.astype(h.dtype) @ P['ml_w_out'][a], C, nv, m


def conv_ffn(h, buf, w_up, conv_w, conv_b, w_down):
    T = h.shape[1]
    a, u = jnp.split(h @ w_up, 2, axis=-1)
    ext = jnp.concatenate([buf.astype(a.dtype), a], axis=1)
    conv = conv_b
    for j in range(CONV_W):
        conv = conv + ext[:, j:j + T] * conv_w[j]
    y = (jax.nn.silu(conv) * u) @ w_down
    return y, ext[:, -(CONV_W - 1):]


def run_trunk(x, P, past, past_len):
    B, T, _ = x.shape
    pos = past_len + jnp.arange(T, dtype=jnp.int32)
    lbs = jnp.cumsum(jax.nn.softmax(P['hgrn_lb'].astype(F32), axis=0), axis=0)
    lbs = lbs - lbs[0]
    nsa_rows, nsa_win, hg_s, ml_c, ml_n, ml_m, ffn_buf = [], [], [], [], [], [], []
    for i in range(DEPTH):
        kind, a = i % N_MIXERS, i // N_MIXERS
        h = rmsnorm(x, P['norm_mix'][i])
        if kind == 0:
            y, rows, win = nsa_mixer(h, pos, a, P, past, past_len)
            nsa_rows.append(rows)
            nsa_win.append(win)
        elif kind == 1:
            if past is None:
                S0 = jnp.zeros((B, HG_HEADS, HG_HEAD_DIM, HG_HEAD_DIM), F32)
            else:
                S0 = past['state_hgrn'][a].astype(F32)
            y, S = hgrn_mixer(h, lbs[i], a, P, S0)
            hg_s.append(S)
        else:
            if past is None:
                C0 = jnp.zeros((B, ML_HEADS, ML_DK, ML_DV), F32)
                n0 = jnp.zeros((B, ML_HEADS, ML_DK), F32)
                m0 = jnp.zeros((B, ML_HEADS), F32)
            else:
                C0 = past['state_mlstm_c'][a].astype(F32)
                n0 = past['state_mlstm_n'][a].astype(F32)
                m0 = past['state_mlstm_m'][a].astype(F32)
            y, C, nv, m = mlstm_mixer(h, a, P, C0, n0, m0)
            ml_c.append(C)
            ml_n.append(nv)
            ml_m.append(m)
        x = x + y.astype(x.dtype)
        h = rmsnorm(x, P['norm_ffn'][i])
        if past is None:
            buf0 = jnp.zeros((B, CONV_W - 1, D_FF), x.dtype)
        else:
            buf0 = past['state_ffn_conv'][i]
        y, buf = conv_ffn(h, buf0, P['ffn_w_up'][i], P['ffn_conv_w'][i], P['ffn_conv_b'][i], P['ffn_w_down'][i])
        ffn_buf.append(buf)
        x = x + y.astype(x.dtype)
    out = rmsnorm(x, P['norm_out'])
    states = dict(nsa_kv=jnp.stack(nsa_rows, axis=2), nsa_win=jnp.stack(nsa_win, axis=0),
                  hgrn=jnp.stack(hg_s, axis=0), mlstm_c=jnp.stack(ml_c, axis=0),
                  mlstm_n=jnp.stack(ml_n, axis=0), mlstm_m=jnp.stack(ml_m, axis=0),
                  ffn_conv=jnp.stack(ffn_buf, axis=0))
    return out, states


def setup_inputs(seed: int = 0) -> dict:
    key = jax.random.key(seed)
    keys = iter(jax.random.split(key, 48))

    def nrm(shape, scale=1.0):
        return scale * jax.random.normal(next(keys), shape, F32)

    n_pages = PAST_LEN // PAGE_SIZE
    n_used = DEC_BATCH * n_pages
    n_pool = n_used + max(1, n_used // 4)
    win_buf = min(WINDOW, PAST_LEN)
    page_table = jax.random.permutation(next(keys), n_pool)[:n_used].reshape(DEC_BATCH, n_pages).astype(jnp.int32)
    gate_offset = jnp.concatenate([jnp.zeros((ML_HEADS,), F32), jnp.full((ML_HEADS,), 3.0, F32)])
    return {
        'x_prompt': nrm((BATCH, SEQ, D_MODEL)),
        'x_sample': nrm((DEC_BATCH, DEC_SEQ, D_MODEL)),
        'cache_nsa_kv': nrm((n_pool, PAGE_SIZE, N_NSA, 4, NSA_KV_HEADS, NSA_HEAD_DIM)),
        'state_nsa_win': nrm((N_NSA, DEC_BATCH, win_buf, 2, NSA_KV_HEADS, NSA_HEAD_DIM)),
        'state_hgrn': nrm((N_HGRN, DEC_BATCH, HG_HEADS, HG_HEAD_DIM, HG_HEAD_DIM), 0.5),
        'state_mlstm_c': nrm((N_MLSTM, DEC_BATCH, ML_HEADS, ML_DK, ML_DV), 0.3),
        'state_mlstm_n': nrm((N_MLSTM, DEC_BATCH, ML_HEADS, ML_DK), 0.3),
        'state_mlstm_m': nrm((N_MLSTM, DEC_BATCH, ML_HEADS), 0.5),
        'state_ffn_conv': nrm((DEPTH, DEC_BATCH, CONV_W - 1, D_FF)),
        'page_table': page_table,
        'norm_mix': 1.0 + nrm((DEPTH, D_MODEL), 0.02),
        'norm_ffn': 1.0 + nrm((DEPTH, D_MODEL), 0.02),
        'norm_out': 1.0 + nrm((D_MODEL,), 0.02),
        'nsa_w_in': nrm((N_NSA, D_MODEL, NSA_IN), D_MODEL ** -0.5),
        'nsa_cmp_w1': nrm((N_NSA, 2, CMP_BLOCK, NSA_HEAD_DIM, CMP_HIDDEN), (CMP_BLOCK * NSA_HEAD_DIM) ** -0.5),
        'nsa_cmp_pe': nrm((N_NSA, 2, CMP_BLOCK, NSA_HEAD_DIM), 0.02),
        'nsa_cmp_w2': nrm((N_NSA, 2, CMP_HIDDEN, NSA_HEAD_DIM), CMP_HIDDEN ** -0.5),
        'nsa_w_out': nrm((N_NSA, NSA_Q_WIDTH, D_MODEL), NSA_Q_WIDTH ** -0.5),
        'hgrn_w_in': nrm((N_HGRN, D_MODEL, 4 * HG_WIDTH), D_MODEL ** -0.5),
        'hgrn_lb': nrm((DEPTH, HG_WIDTH), 1.0),
        'hgrn_norm': 1.0 + nrm((N_HGRN, HG_WIDTH), 0.02),
        'hgrn_w_out': nrm((N_HGRN, HG_WIDTH, D_MODEL), HG_WIDTH ** -0.5),
        'ml_w_in': nrm((N_MLSTM, D_MODEL, ML_IN), D_MODEL ** -0.5),
        'ml_b_gate': nrm((N_MLSTM, 2 * ML_HEADS), 0.1) + gate_offset,
        'ml_w_out': nrm((N_MLSTM, ML_HEADS * ML_DV, D_MODEL), (ML_HEADS * ML_DV) ** -0.5),
        'ffn_w_up': nrm((DEPTH, D_MODEL, 2 * D_FF), D_MODEL ** -0.5),
        'ffn_conv_w': nrm((DEPTH, CONV_W, D_FF), CONV_W ** -0.5),
        'ffn_conv_b': nrm((DEPTH, D_FF), 0.01),
        'ffn_w_down': nrm((DEPTH, D_FF, D_MODEL), D_FF ** -0.5),
    }


def reference(x_prompt, x_sample, cache_nsa_kv, state_nsa_win, state_hgrn, state_mlstm_c, state_mlstm_n,
              state_mlstm_m, state_ffn_conv, page_table, norm_mix, norm_ffn, norm_out, nsa_w_in, nsa_cmp_w1,
              nsa_cmp_pe, nsa_cmp_w2, nsa_w_out, hgrn_w_in, hgrn_lb, hgrn_norm, hgrn_w_out, ml_w_in, ml_b_gate,
              ml_w_out, ffn_w_up, ffn_conv_w, ffn_conv_b, ffn_w_down):
    P = dict(norm_mix=norm_mix, norm_ffn=norm_ffn, norm_out=norm_out, nsa_w_in=nsa_w_in,
             nsa_cmp_w1=nsa_cmp_w1, nsa_cmp_pe=nsa_cmp_pe, nsa_cmp_w2=nsa_cmp_w2, nsa_w_out=nsa_w_out,
             hgrn_w_in=hgrn_w_in, hgrn_lb=hgrn_lb, hgrn_norm=hgrn_norm, hgrn_w_out=hgrn_w_out,
             ml_w_in=ml_w_in, ml_b_gate=ml_b_gate, ml_w_out=ml_w_out, ffn_w_up=ffn_w_up,
             ffn_conv_w=ffn_conv_w, ffn_conv_b=ffn_conv_b, ffn_w_down=ffn_w_down)
    past = dict(cache_nsa_kv=cache_nsa_kv, page_table=page_table, state_nsa_win=state_nsa_win,
                state_hgrn=state_hgrn, state_mlstm_c=state_mlstm_c, state_mlstm_n=state_mlstm_n,
                state_mlstm_m=state_mlstm_m, state_ffn_conv=state_ffn_conv)
    past_len = page_table.shape[1] * PAGE_SIZE
    y_prompt, sp = run_trunk(x_prompt, P, None, 0)
    y_sample, ss = run_trunk(x_sample, P, past, past_len)
    return (y_prompt, y_sample, sp['nsa_kv'], ss['nsa_kv'], sp['nsa_win'], ss['nsa_win'], sp['hgrn'], ss['hgrn'],
            sp['mlstm_c'], ss['mlstm_c'], sp['mlstm_n'], ss['mlstm_n'], sp['mlstm_m'], ss['mlstm_m'],
            sp['ffn_conv'], ss['ffn_conv'])
```

```python
import functools
import math

import jax
import jax.numpy as jnp
from jax import lax
from jax.experimental import pallas as pl
from jax.experimental.pallas import tpu as pltpu

F32 = jnp.float32
BF16 = jnp.bfloat16

D_MODEL = 2048
DEPTH = 4
PAGE_SIZE = 128
N_MIXERS = 3

HD = 128
NSA_HEADS = D_MODEL // HD
NSA_KVH = 4
NSA_G = NSA_HEADS // NSA_KVH
NSA_QW = NSA_HEADS * HD
NSA_KVW = 6 * NSA_KVH * HD
NSA_MAIN = NSA_QW + NSA_KVW
CMP_STRIDE = 16
CMP_RATIO = 2
CMP_BLOCK = CMP_STRIDE * CMP_RATIO
SLC_BLOCK = 64
SLC_TOP_N = 16
FORCE_BONUS = 1000.0
WINDOW = 512
ROPE_THETA = 500000.0
ROPE_DIM = HD // 4
ROPE_HALF = ROPE_DIM // 2

HG_HEADS = D_MODEL // HD
HG_W = HG_HEADS * HD

ML_HEADS = 8
ML_DK = D_MODEL // (2 * ML_HEADS)
ML_DV = D_MODEL // ML_HEADS
ML_QK = ML_HEADS * ML_DK
ML_VW = ML_HEADS * ML_DV
ML_MAIN = 2 * ML_QK + 2 * ML_VW
ML_GATE_CAP = 15.0
CHUNK = 64
SUB = 16

D_FF = (11 * D_MODEL) // 4
EPS = 1e-6
NEG = -1e30

LANE = 128
VMEM_LIMIT = 56 * 1024 * 1024


def _cparams(sem):
    return pltpu.CompilerParams(dimension_semantics=sem, vmem_limit_bytes=VMEM_LIMIT)


def _dot(a, b):
    return jnp.dot(a, b, preferred_element_type=F32)


def _dot_nt(a, b):
    return lax.dot_general(a, b, (((1,), (1,)), ((), ())), preferred_element_type=F32)


def _dot_tn(a, b):
    return lax.dot_general(a, b, (((0,), (0,)), ((), ())), preferred_element_type=F32)


def _dot_hi(a, b):
    return jnp.dot(a, b, preferred_element_type=F32, precision=lax.Precision.HIGHEST)


def _rms(x, g):
    return x * lax.rsqrt(jnp.mean(x * x, axis=-1, keepdims=True) + EPS) * g


def _sigmoid(x):
    return 1.0 / (1.0 + jnp.exp(-x))


def _pick_tile(n, cands):
    for c in cands:
        if n % c == 0:
            return c
    return n


def _proj_body(*refs, norm, residual):
    if residual:
        x_ref, g_ref, w_ref, r_ref, o_ref, xn_ref = refs
    else:
        x_ref, g_ref, w_ref, o_ref, xn_ref = refs

    @pl.when(pl.program_id(1) == 0)
    def _():
        x = x_ref[...]
        if norm:
            x = _rms(x, g_ref[...])
        xn_ref[...] = x.astype(BF16)

    acc = _dot(xn_ref[...], w_ref[...].astype(BF16))
    if residual:
        acc = acc + r_ref[...]
    o_ref[...] = acc


def proj(x, g, w, residual=None, norm=True):
    M, K = x.shape
    N = w.shape[1]
    tm = _pick_tile(M, (1024, 512, 256, 128, 64, 32, 16))
    tn = _pick_tile(N, (512, 384, 256, 128))
    in_specs = [pl.BlockSpec((tm, K), lambda i, j: (i, 0)),
                pl.BlockSpec((1, K), lambda i, j: (0, 0)),
                pl.BlockSpec((K, tn), lambda i, j: (0, j))]
    args = [x, g.reshape(1, K), w]
    if residual is not None:
        in_specs.append(pl.BlockSpec((tm, tn), lambda i, j: (i, j)))
        args.append(residual)
    return pl.pallas_call(
        functools.partial(_proj_body, norm=norm, residual=residual is not None),
        grid=(M // tm, N // tn),
        in_specs=in_specs,
        out_specs=pl.BlockSpec((tm, tn), lambda i, j: (i, j)),
        out_shape=jax.ShapeDtypeStruct((M, N), F32),
        scratch_shapes=[pltpu.VMEM((tm, K), BF16)],
        compiler_params=_cparams(("parallel", "arbitrary")),
        name="proj",
    )(*args)


def _ffn_body(x_ref, g_ref, wa_ref, wu_ref, cw_ref, cb_ref, wd_ref, go_ref, b0_ref, b1_ref,
              y_ref, sa_ref, xn_ref, acc_ref, carry_ref, *, rows_are_batch, final_norm, nf):
    t = pl.program_id(1)
    f = pl.program_id(2)
    tm = x_ref.shape[1]

    @pl.when(f == 0)
    def _():
        xn_ref[...] = _rms(x_ref[0], g_ref[...]).astype(BF16)
        acc_ref[...] = jnp.zeros_like(acc_ref)

    xn = xn_ref[...]
    a = _dot(xn, wa_ref[...].astype(BF16))
    u = _dot(xn, wu_ref[...].astype(BF16))
    if rows_are_batch:
        am2 = b0_ref[...]
        am1 = b1_ref[...]
    else:
        prev = jnp.where(t == 0, 0.0, carry_ref[f])
        p1 = prev[7:8]
        p2 = prev[6:7]
        row = lax.broadcasted_iota(jnp.int32, (tm, 1), 0)
        am1 = jnp.where(row >= 1, pltpu.roll(a, 1, 0), p1)
        am2 = jnp.where(row >= 2, pltpu.roll(a, 2, 0), jnp.where(row == 1, p1, p2))
        carry_ref[f] = a[tm - 8:tm]
    cw = cw_ref[...]
    conv = cb_ref[...] + am2 * cw[0:1] + am1 * cw[1:2] + a * cw[2:3]
    gt = conv * _sigmoid(conv) * u
    acc_ref[...] += _dot(gt.astype(BF16), wd_ref[...].astype(BF16))
    sa_ref[0] = a[0:8] if rows_are_batch else a[tm - 8:tm]

    @pl.when(f == nf - 1)
    def _():
        y = x_ref[0] + acc_ref[...]
        if final_norm:
            y = _rms(y, go_ref[...])
        y_ref[0] = y


def conv_ffn(x, g, w_up, conv_w, conv_b, w_down, g_out, buf=None, final_norm=False):
    B, T, D = x.shape
    F = w_down.shape[0]
    rows_are_batch = buf is not None
    tm = _pick_tile(T, (512, 256, 128, 64, 32, 16, 8))
    tf = _pick_tile(F, (512, 256, 128))
    nf = F // tf
    if buf is None:
        b0 = b1 = jnp.zeros((8, F), F32)
        bspec = pl.BlockSpec((8, tf), lambda b, t, f: (0, f))
    else:
        b0, b1 = buf[0], buf[1]
        bspec = pl.BlockSpec((tm, tf), lambda b, t, f: (0, f))
    y, sa = pl.pallas_call(
        functools.partial(_ffn_body, rows_are_batch=rows_are_batch, final_norm=final_norm, nf=nf),
        grid=(B, T // tm, nf),
        in_specs=[pl.BlockSpec((1, tm, D), lambda b, t, f: (b, t, 0)),
                  pl.BlockSpec((1, D), lambda b, t, f: (0, 0)),
                  pl.BlockSpec((D, tf), lambda b, t, f: (0, f)),
                  pl.BlockSpec((D, tf), lambda b, t, f: (0, nf + f)),
                  pl.BlockSpec((3, tf), lambda b, t, f: (0, f)),
                  pl.BlockSpec((1, tf), lambda b, t, f: (0, f)),
                  pl.BlockSpec((tf, D), lambda b, t, f: (f, 0)),
                  pl.BlockSpec((1, D), lambda b, t, f: (0, 0)),
                  bspec, bspec],
        out_specs=[pl.BlockSpec((1, tm, D), lambda b, t, f: (b, t, 0)),
                   pl.BlockSpec((1, 8, tf), lambda b, t, f: (b, 0, f))],
        out_shape=[jax.ShapeDtypeStruct((B, T, D), F32), jax.ShapeDtypeStruct((B, 8, F), F32)],
        scratch_shapes=[pltpu.VMEM((tm, D), BF16), pltpu.VMEM((tm, D), F32), pltpu.VMEM((nf, 8, tf), F32)],
        compiler_params=_cparams(("parallel", "arbitrary", "arbitrary")),
        name="conv_ffn",
    )(x, g.reshape(1, D), w_up, w_up, conv_w, conv_b.reshape(1, F), w_down, g_out.reshape(1, D), b0, b1)
    return y, sa


def rope_tables(pos):
    inv = ROPE_THETA ** (-jnp.arange(ROPE_HALF, dtype=F32) / ROPE_HALF)
    ang = pos.astype(F32)[:, None] * inv[None, :]
    c, s = jnp.cos(ang), jnp.sin(ang)
    T = pos.shape[0]
    one = jnp.ones((T, HD - ROPE_DIM), F32)
    zero = jnp.zeros((T, HD - ROPE_DIM), F32)
    zh = jnp.zeros((T, ROPE_HALF), F32)
    cos = jnp.concatenate([c, c, one], axis=1)
    sin_lo = jnp.concatenate([-s, zh, zero], axis=1)
    sin_hi = jnp.concatenate([zh, s, zero], axis=1)
    return cos, sin_lo, sin_hi


def _rope_body(x_ref, c_ref, sl_ref, sh_ref, o_ref):
    c, sl, sh = c_ref[...], sl_ref[...], sh_ref[...]
    for h in range(x_ref.shape[1] // HD):
        x = x_ref[:, h * HD:(h + 1) * HD]
        o_ref[:, h * HD:(h + 1) * HD] = (x * c + pltpu.roll(x, HD - ROPE_HALF, 1) * sl
                                         + pltpu.roll(x, ROPE_HALF, 1) * sh)


def nsa_rope(z, tables, T):
    M = z.shape[0]
    tm = _pick_tile(T, (512, 256, 128, 64, 32, 16, 8))
    nt = T // tm
    W = 4 * HD

    def col(j):
        return j + 2 * (j >= 4).astype(jnp.int32) + (j >= 5).astype(jnp.int32)

    tspec = pl.BlockSpec((tm, HD), lambda i, j: (i % nt, 0))
    return pl.pallas_call(
        _rope_body,
        grid=(M // tm, 6),
        in_specs=[pl.BlockSpec((tm, W), lambda i, j: (i, col(j))), tspec, tspec, tspec],
        out_specs=pl.BlockSpec((tm, W), lambda i, j: (i, j)),
        out_shape=jax.ShapeDtypeStruct((M, 6 * W), F32),
        compiler_params=_cparams(("parallel", "arbitrary")),
        name="nsa_rope",
    )(z, *tables)


def _gelu(x):
    return 0.5 * x * (1.0 + jnp.tanh(math.sqrt(2.0 / math.pi) * (x + 0.044715 * x * x * x)))


def _pe_hidden(pe_ref, w1_ref):
    acc = jnp.zeros((8, HD), F32)
    for l in range(CMP_BLOCK):
        row = jnp.broadcast_to(pe_ref[0, l:l + 1, :], (8, HD))
        acc = acc + _dot(row.astype(BF16), w1_ref[0, l].astype(BF16))
    return acc[0:1]


def _cmp_body(x_ref, w1_ref, pe_ref, w2_ref, o_ref, *, nseg):
    parts = []
    for r in range(CMP_RATIO):
        acc = jnp.zeros((nseg, HD), F32)
        for j in range(CMP_STRIDE):
            xj = x_ref[pl.ds(j, nseg, stride=CMP_STRIDE), :]
            acc = acc + _dot(xj.astype(BF16), w1_ref[0, CMP_STRIDE * r + j].astype(BF16))
        parts.append(acc)
    hid = _pe_hidden(pe_ref, w1_ref) + parts[0] + pltpu.roll(parts[1], nseg - 1, 0)
    o_ref[0, 0, 0] = _dot(_gelu(hid).astype(BF16), w2_ref[0].astype(BF16))


def nsa_compress(z, w1, pe, w2, B, T):
    nseg = T // CMP_STRIDE
    return pl.pallas_call(
        functools.partial(_cmp_body, nseg=nseg),
        grid=(2, B, NSA_KVH),
        in_specs=[pl.BlockSpec((T, HD), lambda s, b, k: (b, NSA_QW // HD + NSA_KVH * s + k)),
                  pl.BlockSpec((1, CMP_BLOCK, HD, HD), lambda s, b, k: (s, 0, 0, 0)),
                  pl.BlockSpec((1, CMP_BLOCK, HD), lambda s, b, k: (s, 0, 0)),
                  pl.BlockSpec((1, HD, HD), lambda s, b, k: (s, 0, 0))],
        out_specs=pl.BlockSpec((1, 1, 1, nseg, HD), lambda s, b, k: (s, b, k, 0, 0)),
        out_shape=jax.ShapeDtypeStruct((2, B, NSA_KVH, nseg, HD), F32),
        compiler_params=_cparams(("parallel", "parallel", "parallel")),
        name="nsa_compress",
    )(z, w1, pe, w2)


def _overlap_t(nsp, nseg, n_slc):
    j = lax.broadcasted_iota(jnp.int32, (nsp, 1), 0)
    m = lax.broadcasted_iota(jnp.int32, (1, nseg), 1)
    ov = ((m * CMP_STRIDE <= j * SLC_BLOCK + SLC_BLOCK - 1) & (m * CMP_STRIDE + CMP_BLOCK - 1 >= j * SLC_BLOCK)
          & (m < nseg - CMP_RATIO + 1) & (j < n_slc))
    return ov.astype(F32)


def _rank_rows(score, n):
    j = lax.broadcasted_iota(jnp.int32, (score.shape[0], 1), 0)
    rank = jnp.zeros(score.shape, F32)
    for i in range(n):
        si = score[i:i + 1, :]
        rank = rank + ((si > score) | ((si == score) & (i < j))).astype(F32)
    return rank


def _nsa_attn_body(qraw_ref, qrot_ref, kc_ref, vc_ref, ks_ref, vs_ref, kw_ref, vw_ref, gate_ref, o_ref,
                   *, T, tq, tk):
    qi = pl.program_id(2)
    t0 = qi * tq
    scale = HD ** -0.5
    nseg = T // CMP_STRIDE
    n_slc = T // SLC_BLOCK
    nsp = max(8, n_slc)
    pos = t0 + lax.broadcasted_iota(jnp.int32, (tq, 1), 0)
    pos_row = t0 + lax.broadcasted_iota(jnp.int32, (1, tq), 1)

    kc = kc_ref[0, 0, 0].astype(BF16)
    vc = vc_ref[0, 0, 0].astype(BF16)
    m_idx = lax.broadcasted_iota(jnp.int32, (1, nseg), 1)
    valid = (m_idx * CMP_STRIDE + CMP_BLOCK - 1 <= pos) & (m_idx < nseg - CMP_RATIO + 1)
    psum = jnp.zeros((tq, nseg), F32)
    o_cmp = []
    for g in range(NSA_G):
        q = qraw_ref[:, g * HD:(g + 1) * HD].astype(BF16)
        s = jnp.where(valid, _dot_nt(q, kc) * scale, NEG)
        e = jnp.exp(s - jnp.max(s, axis=-1, keepdims=True))
        p = jnp.where(valid, e / jnp.sum(e, axis=-1, keepdims=True), 0.0)
        o_cmp.append(_dot(p.astype(BF16), vc))
        psum = psum + p

    imp_t = lax.dot_general(_overlap_t(nsp, nseg, n_slc), psum, (((1,), (1,)), ((), ())),
                            preferred_element_type=F32, precision=lax.Precision.HIGHEST)
    j = lax.broadcasted_iota(jnp.int32, (nsp, 1), 0)
    cur = pos_row // SLC_BLOCK
    forced = (j == 0) | (j == cur) | (j == cur - 1)
    score = jnp.where(j * SLC_BLOCK <= pos_row, imp_t + FORCE_BONUS * forced.astype(F32), -1.0)
    score = jnp.where(j < n_slc, score, -2.0)
    sel_t = ((_rank_rows(score, n_slc) < min(SLC_TOP_N, n_slc)) & (j < n_slc)).astype(F32)
    if nsp < LANE:
        sel_t = jnp.concatenate([sel_t, jnp.zeros((LANE - nsp, tq), F32)], axis=0)
    sel = sel_t.T.astype(BF16)
    blk = lax.broadcasted_iota(jnp.int32, (LANE, 1), 0)

    def attend(q_ref, k_ref, v_ref, k0, width, mask_fn, carry):
        k = k_ref[pl.ds(k0, width), :].astype(BF16)
        v = v_ref[pl.ds(k0, width), :].astype(BF16)
        kpos = k0 + lax.broadcasted_iota(jnp.int32, (1, width), 1)
        msk = mask_fn(kpos)
        out = []
        for g in range(NSA_G):
            m_old, l_old, acc = carry[g]
            q = q_ref[:, g * HD:(g + 1) * HD].astype(BF16)
            s = jnp.where(msk, _dot_nt(q, k) * scale, NEG)
            m_new = jnp.maximum(m_old, jnp.max(s, axis=-1, keepdims=True))
            alpha = jnp.exp(m_old - m_new)
            p = jnp.where(msk, jnp.exp(s - m_new), 0.0)
            out.append((m_new, alpha * l_old + jnp.sum(p, axis=-1, keepdims=True),
                        alpha * acc + _dot(p.astype(BF16), v)))
        return tuple(out)

    init = tuple((jnp.full((tq, 1), NEG, F32), jnp.zeros((tq, 1), F32), jnp.zeros((tq, HD), F32))
                 for _ in range(NSA_G))

    def slc_step(kt, carry):
        k0 = pl.multiple_of(kt * tk, tk)

        def mask_fn(kpos):
            expand = ((kpos // SLC_BLOCK) == blk).astype(BF16)
            return (_dot(sel, expand) > 0.5) & (kpos <= pos)

        return attend(qrot_ref, ks_ref, vs_ref, k0, tk, mask_fn, carry)

    slc = lax.fori_loop(0, (t0 + tq + tk - 1) // tk, slc_step, init)

    def win_step(kt, carry):
        k0 = pl.multiple_of(kt * tq, tq)

        def mask_fn(kpos):
            d = pos - kpos
            return (d >= 0) & (d < WINDOW)

        return attend(qrot_ref, kw_ref, vw_ref, k0, tq, mask_fn, carry)

    win = lax.fori_loop(jnp.maximum(t0 - WINDOW, 0) // tq, qi + 1, win_step, init)

    gate = _sigmoid(gate_ref[...])
    for g in range(NSA_G):
        o_ref[:, g * HD:(g + 1) * HD] = (gate[:, 3 * g:3 * g + 1] * o_cmp[g]
                                         + gate[:, 3 * g + 1:3 * g + 2] * (slc[g][2] / slc[g][1])
                                         + gate[:, 3 * g + 2:3 * g + 3] * (win[g][2] / win[g][1]))


def nsa_attn_prefill(z, rot, cmp, gates, B, T):
    tq = LANE
    tk = _pick_tile(T, (256, 128))
    nq = T // tq
    nseg = T // CMP_STRIDE
    W = NSA_G * HD
    qspec = pl.BlockSpec((tq, W), lambda b, k, q: (b * nq + q, k))
    cspec = lambda s: pl.BlockSpec((1, 1, 1, nseg, HD), lambda b, k, q: (s, b, k, 0, 0))
    kvspec = lambda c0: pl.BlockSpec((T, HD), lambda b, k, q: (b, c0 + k))
    return pl.pallas_call(
        functools.partial(_nsa_attn_body, T=T, tq=tq, tk=tk),
        grid=(B, NSA_KVH, nq),
        in_specs=[qspec, qspec, cspec(0), cspec(1),
                  kvspec(NSA_QW // HD), kvspec((NSA_QW + 3 * W) // HD),
                  kvspec((NSA_QW + W) // HD), kvspec((NSA_QW + 5 * W) // HD),
                  pl.BlockSpec((tq, LANE), lambda b, k, q: (b * nq + q, k))],
        out_specs=qspec,
        out_shape=jax.ShapeDtypeStruct((B * T, NSA_QW), F32),
        compiler_params=_cparams(("parallel", "parallel", "arbitrary")),
        name="nsa_attn",
    )(z, rot, cmp, cmp, rot, z, rot, z, gates)


def _nsa_gate_weight(w_in):
    K = w_in.shape[0]
    wg = w_in[:, NSA_MAIN:].reshape(K, NSA_KVH, 3 * NSA_G)
    return jnp.pad(wg, ((0, 0), (0, 0), (0, LANE - 3 * NSA_G))).reshape(K, NSA_KVH * LANE)


def nsa_layer_prefill(x, g, w_in, w1, pe, w2, w_out, tables, B, T):
    z = proj(x, g, w_in[:, :NSA_MAIN])
    gates = proj(x, g, _nsa_gate_weight(w_in))
    rot = nsa_rope(z, tables, T)
    cmp = nsa_compress(z, w1, pe, w2, B, T)
    mix = nsa_attn_prefill(z, rot, cmp, gates, B, T)
    y = proj(mix, g, w_out, residual=x, norm=False)
    W = NSA_KVH * HD
    rows = jnp.concatenate([z[:, NSA_QW:NSA_QW + 2 * W], rot[:, NSA_QW:NSA_QW + W],
                            z[:, NSA_QW + 3 * W:NSA_QW + 4 * W]], axis=1).reshape(B, T, 4, NSA_KVH, HD)
    win = jnp.concatenate([rot[:, NSA_QW + W:], z[:, NSA_QW + 5 * W:]], axis=1).reshape(B, T, 2, NSA_KVH, HD)
    return y, rows, win[:, -min(WINDOW, T):]


def _tri(c):
    return (lax.broadcasted_iota(jnp.int32, (c, 1), 0) >= lax.broadcasted_iota(jnp.int32, (1, c), 1)).astype(F32)


def _hgrn_body(q_ref, f_ref, i_ref, gz_ref, lb_ref, gn_ref, s0_ref, o_ref, s_ref, st_ref,
               *, C, SC, t_valid, nt, layer):
    t = pl.program_id(2)
    Tt = q_ref.shape[0]

    @pl.when(t == 0)
    def _():
        st_ref[...] = s0_ref[0, 0].T

    lbp = lb_ref[...]
    e = jnp.exp(lbp - jnp.max(lbp, axis=0, keepdims=True))
    lb = jnp.sum(e[1:layer + 1], axis=0, keepdims=True) / jnp.sum(e, axis=0, keepdims=True)
    gn = gn_ref[...]
    tri = _tri(C)
    sub_row = lax.broadcasted_iota(jnp.int32, (SC, 1), 0)

    def chunk(c, carry):
        r0 = pl.multiple_of(c * C, C)
        qz = q_ref[pl.ds(r0, C), :]
        f = lb + (1.0 - lb) * _sigmoid(f_ref[pl.ds(r0, C), :])
        v = i_ref[pl.ds(r0, C), :]
        gz = gz_ref[pl.ds(r0, C), :]
        k = 1.0 - f
        lf = jnp.log(f)
        if t_valid is not None:
            ok = (t * Tt + r0 + lax.broadcasted_iota(jnp.int32, (C, 1), 0)) < t_valid
            k = jnp.where(ok, k, 0.0)
            lf = jnp.where(ok, lf, 0.0)
        q = qz * _sigmoid(qz)
        b = _dot_hi(tri, lf)
        st = st_ref[...]
        out = _dot_nt((q * jnp.exp(b)).astype(BF16), st.astype(BF16))
        pieces = []
        for I in range(C // SC):
            r = I * SC
            bI, qI, kI, vI = b[r:r + SC], q[r:r + SC], k[r:r + SC], v[r:r + SC]
            oI = jnp.zeros((SC, HD), F32)
            if I > 0:
                beta = b[r - 1:r]
                qt = qI * jnp.exp(bI - beta)
                kt = k[0:r] * jnp.exp(beta - b[0:r])
                att = _dot_nt(qt.astype(BF16), kt.astype(BF16))
                oI = oI + _dot(att.astype(BF16), v[0:r].astype(BF16))
            for s in range(SC):
                w = qI * jnp.exp(jnp.minimum(bI - bI[s:s + 1], 0.0)) * kI[s:s + 1]
                att = jnp.sum(jnp.where(sub_row >= s, w, 0.0), axis=-1, keepdims=True)
                oI = oI + att * vI[s:s + 1]
            pieces.append(oI)
        out = out + (jnp.concatenate(pieces, axis=0) if len(pieces) > 1 else pieces[0])
        bl = b[C - 1:C]
        st_ref[...] = st * jnp.exp(bl) + _dot_tn(v.astype(BF16), (k * jnp.exp(bl - b)).astype(BF16))
        o_ref[pl.ds(r0, C), :] = _rms(out, gn) * (gz * _sigmoid(gz))
        return carry

    lax.fori_loop(0, Tt // C, chunk, 0)

    @pl.when(t == nt - 1)
    def _():
        s_ref[0, 0] = st_ref[...].T


def hgrn_scan(z, lb, layer, gn, s0, B, T, t_valid=None):
    depth = lb.shape[0]
    C = min(CHUNK, T)
    SC = min(SUB, C)
    Tt = _pick_tile(T, (512, 256, 128, 64, 32, 16))
    nt = T // Tt
    zspec = lambda c0: pl.BlockSpec((Tt, HD), lambda b, h, t: (b * nt + t, c0 + h))
    vspec = pl.BlockSpec((1, HD), lambda b, h, t: (0, h))
    lspec = pl.BlockSpec((depth, HD), lambda b, h, t: (0, h))
    sspec = pl.BlockSpec((1, 1, HD, HD), lambda b, h, t: (b, h, 0, 0))
    return pl.pallas_call(
        functools.partial(_hgrn_body, C=C, SC=SC, t_valid=t_valid, nt=nt, layer=layer),
        grid=(B, HG_HEADS, nt),
        in_specs=[zspec(0), zspec(HG_HEADS), zspec(2 * HG_HEADS), zspec(3 * HG_HEADS), lspec, vspec, sspec],
        out_specs=[pl.BlockSpec((Tt, HD), lambda b, h, t: (b * nt + t, h)), sspec],
        out_shape=[jax.ShapeDtypeStruct((B * T, HG_W), F32), jax.ShapeDtypeStruct((B, HG_HEADS, HD, HD), F32)],
        scratch_shapes=[pltpu.VMEM((HD, HD), F32)],
        compiler_params=_cparams(("parallel", "parallel", "arbitrary")),
        name="hgrn_scan",
    )(z, z, z, z, lb, gn.reshape(1, HG_W), s0)


def _mlstm_body(q_ref, k_ref, v_ref, og_ref, gt_ref, bg_ref, c0_ref, n0_ref, m0_ref,
                y_ref, c_ref, n_ref, m_ref, cs_ref, ns_ref, ms_ref, *, C, t_valid, nt):
    h = pl.program_id(1)
    t = pl.program_id(2)
    Tt = q_ref.shape[0]

    @pl.when(t == 0)
    def _():
        cs_ref[...] = c0_ref[0, 0]
        ns_ref[...] = n0_ref[0, 0]
        ms_ref[...] = m0_ref[0, 0]

    tri_b = lax.broadcasted_iota(jnp.int32, (C, 1), 0) >= lax.broadcasted_iota(jnp.int32, (1, C), 1)
    upper = (lax.broadcasted_iota(jnp.int32, (C, 1), 0) <= lax.broadcasted_iota(jnp.int32, (1, C), 1)).astype(F32)
    eye = (lax.broadcasted_iota(jnp.int32, (C, 1), 0) == lax.broadcasted_iota(jnp.int32, (1, C), 1)).astype(F32)
    tri = tri_b.astype(F32)
    ones = jnp.ones((C, C), F32)
    lane = lax.broadcasted_iota(jnp.int32, (1, LANE), 1)
    bg = bg_ref[...]

    def chunk(c, carry):
        r0 = pl.multiple_of(c * C, C)
        g = gt_ref[pl.ds(r0, C), :] + bg
        g = ML_GATE_CAP * jnp.tanh(g / ML_GATE_CAP)
        ig = jnp.sum(jnp.where(lane == h, g, 0.0), axis=-1, keepdims=True)
        fg = jnp.sum(jnp.where(lane == ML_HEADS + h, g, 0.0), axis=-1, keepdims=True)
        lf = jnp.minimum(fg, 0.0) - jnp.log(1.0 + jnp.exp(-jnp.abs(fg)))
        if t_valid is not None:
            ok = (t * Tt + r0 + lax.broadcasted_iota(jnp.int32, (C, 1), 0)) < t_valid
            lf = jnp.where(ok, lf, 0.0)
            ig = jnp.where(ok, ig, NEG)
        lf_b = jnp.broadcast_to(lf, (C, C))
        ig_b = jnp.broadcast_to(ig, (C, C))
        b_col = _dot_hi(tri, lf_b)
        b_row = _dot_hi(ones, lf_b * upper)
        i_row = _dot_hi(ones, ig_b * eye)
        logw = jnp.where(tri_b, b_col - b_row + i_row, NEG)
        m_old = ms_ref[:, 0:1]
        b1 = b_col[:, 0:1]
        m_prev = b1 + m_old
        m_t = jnp.maximum(m_prev, jnp.max(logw, axis=-1, keepdims=True))
        w = jnp.exp(logw - m_t)
        w0 = jnp.exp(m_prev - m_t)
        q = q_ref[pl.ds(r0, C), :]
        ks = k_ref[pl.ds(r0, C), :] * (ML_DK ** -0.5)
        v = v_ref[pl.ds(r0, C), :]
        qb, kb, vb = q.astype(BF16), ks.astype(BF16), v.astype(BF16)
        cs = cs_ref[...]
        ns = ns_ref[...]
        sqk = _dot_nt(qb, kb) * w
        num = w0 * _dot(qb, cs.astype(BF16)) + _dot(sqk.astype(BF16), vb)
        den = w0 * jnp.sum(q * ns, axis=-1, keepdims=True) + jnp.sum(sqk, axis=-1, keepdims=True)
        hc = num / jnp.maximum(jnp.abs(den), jnp.exp(-m_t))
        y_ref[pl.ds(r0, C), :] = _sigmoid(og_ref[pl.ds(r0, C), :]) * hc
        m_new = m_t[C - 1:C]
        b_last = b1[C - 1:C]
        wk = jnp.exp(b_last - b1 + ig - m_new)
        decay = jnp.exp(b_last + m_old - m_new)
        kw = ks * wk
        cs_ref[...] = decay * cs + _dot_tn(kw.astype(BF16), vb)
        ns_ref[...] = decay * ns + jnp.sum(kw, axis=0, keepdims=True)
        ms_ref[...] = jnp.broadcast_to(m_new, (1, LANE))
        return carry

    lax.fori_loop(0, Tt // C, chunk, 0)

    @pl.when(t == nt - 1)
    def _():
        c_ref[0, 0] = cs_ref[...]
        n_ref[0, 0] = ns_ref[...]
        m_ref[0, 0] = ms_ref[...]


def mlstm_scan(z, gates, b_gate, c0, n0, m0, B, T, t_valid=None):
    C = min(CHUNK, T)
    Tt = _pick_tile(T, (512, 256, 128, 64, 32, 16))
    nt = T // Tt
    H = ML_HEADS
    row = lambda b, t: b * nt + t
    bg = jnp.pad(b_gate, (0, LANE - 2 * H)).reshape(1, LANE)
    n0 = n0.reshape(B, H, 1, ML_DK)
    m0 = jnp.broadcast_to(m0[:, :, None, None], (B, H, 1, LANE))
    vec = lambda w: pl.BlockSpec((1, 1, 1, w), lambda b, h, t: (b, h, 0, 0))
    cspec = pl.BlockSpec((1, 1, ML_DK, ML_DV), lambda b, h, t: (b, h, 0, 0))
    y, c, n, m = pl.pallas_call(
        functools.partial(_mlstm_body, C=C, t_valid=t_valid, nt=nt),
        grid=(B, H, nt),
        in_specs=[pl.BlockSpec((Tt, ML_DK), lambda b, h, t: (row(b, t), h)),
                  pl.BlockSpec((Tt, ML_DK), lambda b, h, t: (row(b, t), H + h)),
                  pl.BlockSpec((Tt, ML_DV), lambda b, h, t: (row(b, t), 2 * ML_QK // ML_DV + h)),
                  pl.BlockSpec((Tt, ML_DV), lambda b, h, t: (row(b, t), (2 * ML_QK + ML_VW) // ML_DV + h)),
                  pl.BlockSpec((Tt, LANE), lambda b, h, t: (row(b, t), 0)),
                  pl.BlockSpec((1, LANE), lambda b, h, t: (0, 0)),
                  cspec, vec(ML_DK), vec(LANE)],
        out_specs=[pl.BlockSpec((Tt, ML_DV), lambda b, h, t: (row(b, t), h)), cspec, vec(ML_DK), vec(LANE)],
        out_shape=[jax.ShapeDtypeStruct((B * T, ML_VW), F32), jax.ShapeDtypeStruct((B, H, ML_DK, ML_DV), F32),
                   jax.ShapeDtypeStruct((B, H, 1, ML_DK), F32), jax.ShapeDtypeStruct((B, H, 1, LANE), F32)],
        scratch_shapes=[pltpu.VMEM((ML_DK, ML_DV), F32), pltpu.VMEM((1, ML_DK), F32), pltpu.VMEM((1, LANE), F32)],
        compiler_params=_cparams(("parallel", "parallel", "arbitrary")),
        name="mlstm_scan",
    )(z, z, z, z, gates, bg, c0, n0, m0)
    return y, c, n.reshape(B, H, ML_DK), m[:, :, 0, 0]


PAGES_PER_STEP = 8
SEG_PER_PAGE = PAGE_SIZE // CMP_STRIDE
TILE_ROWS = 2 * NSA_KVH


def _cmp_paged_body(pt_ref, *refs):
    P = PAGES_PER_STEP
    x_refs, w_ref, o_ref, lhs_ref = refs[:P], refs[P], refs[P + 1], refs[P + 2]
    rows = SEG_PER_PAGE * TILE_ROWS
    for p in range(P):
        for j in range(CMP_STRIDE):
            xj = x_refs[p][pl.ds(j, SEG_PER_PAGE, stride=CMP_STRIDE)]
            lhs_ref[p * rows:(p + 1) * rows, j * HD:(j + 1) * HD] = xj.reshape(rows, HD).astype(BF16)
    res = _dot(lhs_ref[...], w_ref[...])
    c = lax.broadcasted_iota(jnp.int32, (P * rows, 1), 0) % TILE_ROWS
    o_ref[0] = jnp.where(c < NSA_KVH, res[:, :2 * HD], res[:, 2 * HD:])


def nsa_compress_paged(cache4, page_table, w1, a):
    B, n_pages = page_table.shape
    P = PAGES_PER_STEP
    rows = SEG_PER_PAGE * TILE_ROWS
    w_cat = w1.reshape(2, CMP_RATIO, CMP_STRIDE, HD, HD).transpose(2, 3, 0, 1, 4).reshape(CMP_STRIDE * HD, 4 * HD)
    xspec = lambda i: pl.BlockSpec((PAGE_SIZE, None, TILE_ROWS, HD),
                                   lambda b, s, pt: (pt[b, s * P + i], 2 * a, 0, 0))
    grid_spec = pltpu.PrefetchScalarGridSpec(
        num_scalar_prefetch=1,
        grid=(B, n_pages // P),
        in_specs=[xspec(i) for i in range(P)] + [pl.BlockSpec((CMP_STRIDE * HD, 4 * HD), lambda b, s, pt: (0, 0))],
        out_specs=pl.BlockSpec((1, P * rows, 2 * HD), lambda b, s, pt: (b, s, 0)),
        scratch_shapes=[pltpu.VMEM((P * rows, CMP_STRIDE * HD), BF16)])
    return pl.pallas_call(
        _cmp_paged_body,
        grid_spec=grid_spec,
        out_shape=jax.ShapeDtypeStruct((B, n_pages * rows, 2 * HD), F32),
        compiler_params=_cparams(("parallel", "arbitrary")),
        name="nsa_compress_paged",
    )(page_table, *([cache4] * P), w_cat.astype(BF16))


def _nsa_dec_cmp_body(part_ref, w1_ref, pe_ref, w2_ref, q_ref, o_ref, idx_ref, *, n_seg, n_slc, pos):
    R = n_seg * TILE_ROWS
    nsp = -(-n_slc // LANE) * LANE
    scale = HD ** -0.5
    p = part_ref[0]
    c_row = lax.broadcasted_iota(jnp.int32, (R, 1), 0) % TILE_ROWS
    is_k = c_row < NSA_KVH
    pe_hid = [_pe_hidden(pe_ref.at[pl.ds(s, 1)], w1_ref.at[pl.ds(s, 1)]) for s in range(2)]
    hid = p[:, :HD] + pltpu.roll(p[:, HD:], R - TILE_ROWS, 0) + jnp.where(is_k, pe_hid[0], pe_hid[1])
    ge = _gelu(hid).astype(BF16)
    kv = jnp.where(is_k, _dot(ge, w2_ref[0].astype(BF16)), _dot(ge, w2_ref[1].astype(BF16))).astype(BF16)
    q = q_ref[0].astype(BF16)
    col = lax.broadcasted_iota(jnp.int32, (1, R), 1)
    col_c, col_m = col % TILE_ROWS, col // TILE_ROWS
    head_kv = lax.broadcasted_iota(jnp.int32, (NSA_HEADS, 1), 0) // NSA_G
    valid = ((col_c == head_kv) & (col_m < n_seg - CMP_RATIO + 1)
             & (col_m * CMP_STRIDE + CMP_BLOCK - 1 <= pos))
    s = jnp.where(valid, _dot_nt(q, kv) * scale, NEG)
    e = jnp.exp(s - jnp.max(s, axis=-1, keepdims=True))
    prob = jnp.where(valid, e / jnp.sum(e, axis=-1, keepdims=True), 0.0)
    o_ref[0] = _dot(pltpu.roll(prob, NSA_KVH, 1).astype(BF16), kv)
    group = (lax.broadcasted_iota(jnp.int32, (8, 1), 0)
             == lax.broadcasted_iota(jnp.int32, (1, NSA_HEADS), 1) // NSA_G).astype(F32)
    psum = _dot_hi(group, prob)
    m_r = lax.broadcasted_iota(jnp.int32, (R, 1), 0) // TILE_ROWS
    j = lax.broadcasted_iota(jnp.int32, (1, nsp), 1)
    ov = ((m_r * CMP_STRIDE <= j * SLC_BLOCK + SLC_BLOCK - 1) & (m_r * CMP_STRIDE + CMP_BLOCK - 1 >= j * SLC_BLOCK)
          & (m_r < n_seg - CMP_RATIO + 1) & (j < n_slc)).astype(F32)
    imp = _dot_hi(psum, ov)
    cur = pos // SLC_BLOCK
    forced = (j == 0) | (j == cur) | (j == cur - 1)
    score = jnp.where(j * SLC_BLOCK <= pos, imp + FORCE_BONUS * forced.astype(F32), -1.0)
    score = jnp.where(j < n_slc, score, -2.0)
    jf = j.astype(F32)
    slot = lax.broadcasted_iota(jnp.int32, (1, LANE), 1)
    picked = jnp.zeros((8, LANE), F32)
    for it in range(min(SLC_TOP_N, n_slc)):
        best = jnp.max(score, axis=-1, keepdims=True)
        ix = jnp.min(jnp.where(score == best, jf, float(nsp)), axis=-1, keepdims=True)
        picked = jnp.where(slot == it, ix, picked)
        score = jnp.where(jf == ix, -3.0, score)
    idx_ref[0] = picked.astype(jnp.int32)


def nsa_dec_cmp(part, w1, pe, w2, q_raw, n_seg, n_slc, pos):
    B = part.shape[0]
    R = n_seg * TILE_ROWS
    full = lambda shape: pl.BlockSpec(shape, lambda b: (0,) * len(shape))
    return pl.pallas_call(
        functools.partial(_nsa_dec_cmp_body, n_seg=n_seg, n_slc=n_slc, pos=pos),
        grid=(B,),
        in_specs=[pl.BlockSpec((1, R, 2 * HD), lambda b: (b, 0, 0)),
                  full((2, CMP_BLOCK, HD, HD)), full((2, CMP_BLOCK, HD)), full((2, HD, HD)),
                  pl.BlockSpec((1, NSA_HEADS, HD), lambda b: (b, 0, 0))],
        out_specs=[pl.BlockSpec((1, NSA_HEADS, HD), lambda b: (b, 0, 0)),
                   pl.BlockSpec((1, 8, LANE), lambda b: (b, 0, 0))],
        out_shape=[jax.ShapeDtypeStruct((B, NSA_HEADS, HD), F32), jax.ShapeDtypeStruct((B, 8, LANE), jnp.int32)],
        compiler_params=_cparams(("parallel",)),
        name="nsa_dec_cmp",
    )(part, w1, pe, w2, q_raw)


def _nsa_dec_attn_body(pt_ref, idx_ref, blk_ref, q_ref, new_ref, win_ref, ocmp_ref, gate_ref, o_ref,
                       m_ref, l_ref, acc_ref, *, n_cache_blocks, nsel, win_skip):
    kh = pl.program_id(1)
    n = pl.program_id(2)
    scale = HD ** -0.5
    q = q_ref[0, 0].astype(BF16)
    new = new_ref[0, 0]
    s_new = _dot_nt(q, new.astype(BF16)) * scale

    @pl.when(n == 0)
    def _():
        m_ref[...] = jnp.broadcast_to(s_new[:, 0:1], (8, LANE))
        l_ref[...] = jnp.ones((8, LANE), F32)
        acc_ref[...] = jnp.broadcast_to(new[1:2], (8, HD))

    def masked_scores(x, msk):
        s = jnp.where(msk, _dot_nt(q, x) * scale, NEG)
        return s

    j = idx_ref[pl.program_id(0), kh, n]
    x = blk_ref[...].reshape(SLC_BLOCK * TILE_ROWS, HD).astype(BF16)
    col_c = lax.broadcasted_iota(jnp.int32, (1, SLC_BLOCK * TILE_ROWS), 1) % TILE_ROWS
    msk = (col_c == kh) & (j < n_cache_blocks)
    s = masked_scores(x, msk)
    m_old = m_ref[:, 0:1]
    m_new = jnp.maximum(m_old, jnp.max(s, axis=-1, keepdims=True))
    alpha = jnp.exp(m_old - m_new)
    p = jnp.where(msk, jnp.exp(s - m_new), 0.0)
    l_new = alpha * l_ref[:, 0:1] + jnp.sum(p, axis=-1, keepdims=True)
    acc_new = alpha * acc_ref[...] + _dot(pltpu.roll(p, NSA_KVH, 1).astype(BF16), x)
    m_ref[...] = jnp.broadcast_to(m_new, (8, LANE))
    l_ref[...] = jnp.broadcast_to(l_new, (8, LANE))
    acc_ref[...] = acc_new

    @pl.when(n == nsel - 1)
    def _():
        n_win = win_ref.shape[1]
        xw = win_ref[0].reshape(n_win * TILE_ROWS, HD).astype(BF16)
        colw = lax.broadcasted_iota(jnp.int32, (1, n_win * TILE_ROWS), 1)
        mw = ((colw % TILE_ROWS) == kh) & ((colw // TILE_ROWS) >= win_skip)
        sw = masked_scores(xw, mw)
        s_nw = s_new[:, 2:3]
        mx = jnp.maximum(jnp.max(sw, axis=-1, keepdims=True), s_nw)
        pw = jnp.where(mw, jnp.exp(sw - mx), 0.0)
        pn = jnp.exp(s_nw - mx)
        o_win = ((_dot(pltpu.roll(pw, NSA_KVH, 1).astype(BF16), xw) + pn * new[3:4])
                 / (jnp.sum(pw, axis=-1, keepdims=True) + pn))
        o_slc = acc_new / l_new
        gate = _sigmoid(gate_ref[0, 0])
        lane = lax.broadcasted_iota(jnp.int32, (1, LANE), 1)
        row = lax.broadcasted_iota(jnp.int32, (8, 1), 0)
        pick = lambda c: jnp.sum(jnp.where(lane == 3 * row + c, gate, 0.0), axis=-1, keepdims=True)
        o_ref[0, 0] = pick(0) * ocmp_ref[0, 0] + pick(1) * o_slc + pick(2) * o_win


def nsa_dec_attn(cache5, win4, page_table, idx, q_rot, new_rows, o_cmp, gates, a, n_cache_blocks, win_skip):
    B, _, nsel = idx.shape
    n_win = win4.shape[1]

    def blk_map(b, k, n, pt, ix):
        j = jnp.minimum(ix[b, k, n], n_cache_blocks - 1)
        return (pt[b, j // 2] * 2 + j % 2, 0, 2 * a + 1, 0, 0)

    grp = pl.BlockSpec((1, 1, 8, HD), lambda b, k, n, pt, ix: (b, k, 0, 0))
    grid_spec = pltpu.PrefetchScalarGridSpec(
        num_scalar_prefetch=2,
        grid=(B, NSA_KVH, nsel),
        in_specs=[pl.BlockSpec((None, SLC_BLOCK, None, TILE_ROWS, HD), blk_map), grp, grp,
                  pl.BlockSpec((1, n_win, TILE_ROWS, HD), lambda b, k, n, pt, ix: (a * B + b, 0, 0, 0)),
                  grp, grp],
        out_specs=grp,
        scratch_shapes=[pltpu.VMEM((8, LANE), F32), pltpu.VMEM((8, LANE), F32), pltpu.VMEM((8, HD), F32)])
    return pl.pallas_call(
        functools.partial(_nsa_dec_attn_body, n_cache_blocks=n_cache_blocks, nsel=nsel, win_skip=win_skip),
        grid_spec=grid_spec,
        out_shape=jax.ShapeDtypeStruct((B, NSA_KVH, 8, HD), F32),
        compiler_params=_cparams(("parallel", "parallel", "arbitrary")),
        name="nsa_dec_attn",
    )(page_table, idx, cache5, q_rot, new_rows, win4, o_cmp, gates)


def _group_rows(x, B):
    x = x.reshape(B, NSA_KVH, NSA_G, HD)
    return jnp.pad(x, ((0, 0), (0, 0), (0, 8 - NSA_G), (0, 0)))


def nsa_layer_decode(x, g, w_in, w1, pe, w2, w_out, cache, win_state, page_table, a, B):
    R = x.shape[0]
    n_pool = cache.shape[0]
    n_pages = page_table.shape[1]
    past = n_pages * PAGE_SIZE
    n_seg = (past + 1) // CMP_STRIDE
    n_slc = -(-(past + 1) // SLC_BLOCK)
    W = NSA_KVH * HD
    z = proj(x, g, w_in[:, :NSA_MAIN])
    gates = proj(x, g, _nsa_gate_weight(w_in))
    rot = nsa_rope(z, rope_tables(jnp.full((R,), past, jnp.int32)), R)
    part = nsa_compress_paged(cache.reshape(n_pool * PAGE_SIZE, 4, TILE_ROWS, HD), page_table, w1, a)
    o_cmp, idx = nsa_dec_cmp(part, w1, pe, w2, z[:B, :NSA_QW].reshape(B, NSA_HEADS, HD), n_seg, n_slc, past)
    nsel = min(SLC_TOP_N, n_slc)
    kv = z[:B, NSA_QW:].reshape(B, 6, NSA_KVH, HD)
    ks_new = rot[:B, NSA_QW:NSA_QW + W].reshape(B, NSA_KVH, HD)
    kw_new = rot[:B, NSA_QW + W:].reshape(B, NSA_KVH, HD)
    new_rows = jnp.stack([ks_new, kv[:, 3], kw_new, kv[:, 5]], axis=2)
    new_rows = jnp.pad(new_rows, ((0, 0), (0, 0), (0, 4), (0, 0)))
    n_win = win_state.shape[2]
    mix = nsa_dec_attn(cache.reshape(n_pool * 2, SLC_BLOCK, 4, TILE_ROWS, HD),
                       win_state.reshape(-1, n_win, TILE_ROWS, HD), page_table, idx[:, :NSA_KVH, :nsel],
                       _group_rows(rot[:B, :NSA_QW], B), new_rows, _group_rows(o_cmp.reshape(B, NSA_QW), B),
                       jnp.broadcast_to(gates[:B].reshape(B, NSA_KVH, 1, LANE), (B, NSA_KVH, 8, LANE)),
                       a, past // SLC_BLOCK, max(n_win + 1 - WINDOW, 0))
    mix = jnp.pad(mix[:, :, :NSA_G].reshape(B, NSA_QW), ((0, R - B), (0, 0)))
    y = proj(mix, g, w_out, residual=x, norm=False)
    rows = jnp.stack([kv[:, 0], kv[:, 1], ks_new, kv[:, 3]], axis=1)[:, None]
    win_new = jnp.stack([kw_new, kv[:, 5]], axis=1)[:, None]
    buf = jnp.concatenate([win_state[a], win_new], axis=1)
    return y, rows, buf[:, -min(WINDOW, n_win + 1):]


def kernel(x_prompt, x_sample, cache_nsa_kv, state_nsa_win, state_hgrn, state_mlstm_c, state_mlstm_n, state_mlstm_m, state_ffn_conv, page_table, norm_mix, norm_ffn, norm_out, nsa_w_in, nsa_cmp_w1, nsa_cmp_pe, nsa_cmp_w2, nsa_w_out, hgrn_w_in, hgrn_lb, hgrn_norm, hgrn_w_out, ml_w_in, ml_b_gate, ml_w_out, ffn_w_up, ffn_conv_w, ffn_conv_b, ffn_w_down):
    P = dict(norm_mix=norm_mix, norm_ffn=norm_ffn, norm_out=norm_out, nsa_w_in=nsa_w_in, nsa_cmp_w1=nsa_cmp_w1,
             nsa_cmp_pe=nsa_cmp_pe, nsa_cmp_w2=nsa_cmp_w2, nsa_w_out=nsa_w_out, hgrn_w_in=hgrn_w_in,
             hgrn_lb=hgrn_lb, hgrn_norm=hgrn_norm, hgrn_w_out=hgrn_w_out, ml_w_in=ml_w_in, ml_b_gate=ml_b_gate,
             ml_w_out=ml_w_out, ffn_w_up=ffn_w_up, ffn_conv_w=ffn_conv_w, ffn_conv_b=ffn_conv_b,
             ffn_w_down=ffn_w_down)
    past = dict(cache=cache_nsa_kv, win=state_nsa_win, hgrn=state_hgrn, c=state_mlstm_c, n=state_mlstm_n,
                m=state_mlstm_m, ffn=state_ffn_conv, page_table=page_table)
    yp, sp = _trunk(x_prompt, P, None)
    ys, ss = _trunk(x_sample, P, past)
    return (yp, ys, sp['nsa_kv'], ss['nsa_kv'], sp['nsa_win'], ss['nsa_win'], sp['hgrn'], ss['hgrn'],
            sp['c'], ss['c'], sp['n'], ss['n'], sp['m'], ss['m'], sp['ffn'], ss['ffn'])


DEC_ROWS = 16


def _spread(z, B, T):
    return jnp.zeros((B, T, z.shape[1]), F32).at[:, 0].set(z[:B]).reshape(B * T, z.shape[1])


def _gather_first(o, B, T, R):
    return jnp.pad(o.reshape(B, T, -1)[:, 0], ((0, R - B), (0, 0)))


def _trunk(x_in, P, past):
    B, T, D = x_in.shape
    decode = past is not None
    if decode:
        R = DEC_ROWS
        x = jnp.pad(x_in.reshape(B, D), ((0, R - B), (0, 0)))
        Ts = DEC_ROWS
    else:
        x = x_in.reshape(B * T, D)
        tables = rope_tables(jnp.arange(T, dtype=jnp.int32))
    rows_l, win_l, hg_l, c_l, n_l, m_l, ffn_l = [], [], [], [], [], [], []
    for i in range(DEPTH):
        kind, a = i % N_MIXERS, i // N_MIXERS
        g = P['norm_mix'][i]
        if kind == 0:
            args = (x, g, P['nsa_w_in'][a], P['nsa_cmp_w1'][a], P['nsa_cmp_pe'][a], P['nsa_cmp_w2'][a],
                    P['nsa_w_out'][a])
            if decode:
                x, rows, win = nsa_layer_decode(*args, past['cache'], past['win'], past['page_table'], a, B)
            else:
                x, rows, win = nsa_layer_prefill(*args, tables, B, T)
            rows_l.append(rows)
            win_l.append(win)
        elif kind == 1:
            z = proj(x, g, P['hgrn_w_in'][a])
            if decode:
                o, S = hgrn_scan(_spread(z, B, Ts), P['hgrn_lb'], i, P['hgrn_norm'][a], past['hgrn'][a], B, Ts, 1)
                o = _gather_first(o, B, Ts, R)
            else:
                o, S = hgrn_scan(z, P['hgrn_lb'], i, P['hgrn_norm'][a], jnp.zeros((B, HG_HEADS, HD, HD), F32), B, T)
            x = proj(o, g, P['hgrn_w_out'][a], residual=x, norm=False)
            hg_l.append(S)
        else:
            w = P['ml_w_in'][a]
            z = proj(x, g, w[:, :ML_MAIN])
            gt = proj(x, g, jnp.pad(w[:, ML_MAIN:], ((0, 0), (0, LANE - 2 * ML_HEADS))))
            if decode:
                o, C, nv, m = mlstm_scan(_spread(z, B, Ts), _spread(gt, B, Ts), P['ml_b_gate'][a],
                                         past['c'][a], past['n'][a], past['m'][a], B, Ts, 1)
                o = _gather_first(o, B, Ts, R)
            else:
                o, C, nv, m = mlstm_scan(z, gt, P['ml_b_gate'][a], jnp.zeros((B, ML_HEADS, ML_DK, ML_DV), F32),
                                         jnp.zeros((B, ML_HEADS, ML_DK), F32), jnp.zeros((B, ML_HEADS), F32), B, T)
            x = proj(o, g, P['ml_w_out'][a], residual=x, norm=False)
            c_l.append(C)
            n_l.append(nv)
            m_l.append(m)
        fargs = (P['norm_ffn'][i], P['ffn_w_up'][i], P['ffn_conv_w'][i], P['ffn_conv_b'][i], P['ffn_w_down'][i],
                 P['norm_out'])
        last = i == DEPTH - 1
        if decode:
            buf = past['ffn'][i]
            bufp = jnp.pad(buf.transpose(1, 0, 2), ((0, 0), (0, R - B), (0, 0)))
            y, sa = conv_ffn(x[None], *fargs, buf=bufp, final_norm=last)
            x = y[0]
            ffn_l.append(jnp.stack([buf[:, 1], sa[0, :B]], axis=1))
        else:
            y, sa = conv_ffn(x.reshape(B, T, D), *fargs, final_norm=last)
            x = y.reshape(B * T, D)
            ffn_l.append(sa[:, 6:8])
    out = x[:B].reshape(B, 1, D) if decode else x.reshape(B, T, D)
    states = dict(nsa_kv=jnp.stack(rows_l, axis=2), nsa_win=jnp.stack(win_l, axis=0), hgrn=jnp.stack(hg_l, axis=0),
                  c=jnp.stack(c_l, axis=0), n=jnp.stack(n_l, axis=0), m=jnp.stack(m_l, axis=0),
                  ffn=jnp.stack(ffn_l, axis=0))
    return out, states
```

```python
import functools
import math

import jax
import jax.numpy as jnp
from jax import lax
from jax.experimental import pallas as pl
from jax.experimental.pallas import tpu as pltpu

F32 = jnp.float32
BF16 = jnp.bfloat16

D_MODEL = 2048
DEPTH = 4
PAGE_SIZE = 128
N_MIXERS = 3

HD = 128
NSA_HEADS = D_MODEL // HD
NSA_KVH = 4
NSA_G = NSA_HEADS // NSA_KVH
NSA_QW = NSA_HEADS * HD
NSA_KVW = 6 * NSA_KVH * HD
NSA_MAIN = NSA_QW + NSA_KVW
CMP_STRIDE = 16
CMP_RATIO = 2
CMP_BLOCK = CMP_STRIDE * CMP_RATIO
SLC_BLOCK = 64
SLC_TOP_N = 16
FORCE_BONUS = 1000.0
WINDOW = 512
ROPE_THETA = 500000.0
ROPE_DIM = HD // 4
ROPE_HALF = ROPE_DIM // 2

HG_HEADS = D_MODEL // HD
HG_W = HG_HEADS * HD

ML_HEADS = 8
ML_DK = D_MODEL // (2 * ML_HEADS)
ML_DV = D_MODEL // ML_HEADS
ML_QK = ML_HEADS * ML_DK
ML_VW = ML_HEADS * ML_DV
ML_MAIN = 2 * ML_QK + 2 * ML_VW
ML_GATE_CAP = 15.0
CHUNK = 64
SUB = 16
HG_HEADS_PER_STEP = 4
ML_HEADS_PER_STEP = 8

D_FF = (11 * D_MODEL) // 4
EPS = 1e-6
NEG = -1e30

LANE = 128
VMEM_LIMIT = 56 * 1024 * 1024


def _cparams(sem):
    return pltpu.CompilerParams(dimension_semantics=sem, vmem_limit_bytes=VMEM_LIMIT)


def _dot(a, b):
    return jnp.dot(a, b, preferred_element_type=F32)


def _dot_nt(a, b):
    return lax.dot_general(a, b, (((1,), (1,)), ((), ())), preferred_element_type=F32)


def _dot_tn(a, b):
    return lax.dot_general(a, b, (((0,), (0,)), ((), ())), preferred_element_type=F32)


def _dot_hi(a, b):
    return jnp.dot(a, b, preferred_element_type=F32, precision=lax.Precision.HIGHEST)


def _rms(x, g):
    return x * lax.rsqrt(jnp.mean(x * x, axis=-1, keepdims=True) + EPS) * g


def _sigmoid(x):
    return 1.0 / (1.0 + jnp.exp(-x))


def _pick_tile(n, cands):
    for c in cands:
        if n % c == 0:
            return c
    return n


def _proj_body(*refs, norm, residual):
    if residual:
        x_ref, g_ref, w_ref, r_ref, o_ref, xn_ref = refs
    else:
        x_ref, g_ref, w_ref, o_ref, xn_ref = refs

    @pl.when(pl.program_id(1) == 0)
    def _():
        x = x_ref[...]
        if norm:
            x = _rms(x, g_ref[...])
        xn_ref[...] = x.astype(BF16)

    acc = _dot(xn_ref[...], w_ref[...].astype(BF16))
    if residual:
        acc = acc + r_ref[...]
    o_ref[...] = acc


def proj(x, g, w, residual=None, norm=True, n_out=None):
    M, K = x.shape
    N = w.shape[1] if n_out is None else n_out
    tm = _pick_tile(M, (1024, 512, 256, 128, 64, 32, 16))
    tn = _pick_tile(N, (512, 384, 256, 128))
    in_specs = [pl.BlockSpec((tm, K), lambda i, j: (i, 0)),
                pl.BlockSpec((1, K), lambda i, j: (0, 0)),
                pl.BlockSpec((K, tn), lambda i, j: (0, j))]
    args = [x, g.reshape(1, K), w]
    if residual is not None:
        in_specs.append(pl.BlockSpec((tm, tn), lambda i, j: (i, j)))
        args.append(residual)
    return pl.pallas_call(
        functools.partial(_proj_body, norm=norm, residual=residual is not None),
        grid=(M // tm, N // tn),
        in_specs=in_specs,
        out_specs=pl.BlockSpec((tm, tn), lambda i, j: (i, j)),
        out_shape=jax.ShapeDtypeStruct((M, N), F32),
        scratch_shapes=[pltpu.VMEM((tm, K), BF16)],
        compiler_params=_cparams(("parallel", "arbitrary")),
        name="proj",
    )(*args)


def _ffn_body(x_ref, g_ref, wa_ref, wu_ref, cw_ref, cb_ref, wd_ref, go_ref, b0_ref, b1_ref,
              y_ref, sa_ref, xn_ref, carry_ref, *, rows_are_batch, final_norm, nf):
    t = pl.program_id(1)
    f = pl.program_id(2)
    tm = x_ref.shape[1]

    @pl.when(f == 0)
    def _():
        xn_ref[...] = _rms(x_ref[0], g_ref[...]).astype(BF16)
        y_ref[0] = x_ref[0]

    xn = xn_ref[...]
    a = _dot(xn, wa_ref[...].astype(BF16))
    u = _dot(xn, wu_ref[...].astype(BF16))
    if rows_are_batch:
        am2 = b0_ref[...]
        am1 = b1_ref[...]
    else:
        prev = jnp.where(t == 0, 0.0, carry_ref[f])
        p1 = prev[7:8]
        p2 = prev[6:7]
        row = lax.broadcasted_iota(jnp.int32, (tm, 1), 0)
        am1 = jnp.where(row >= 1, pltpu.roll(a, 1, 0), p1)
        am2 = jnp.where(row >= 2, pltpu.roll(a, 2, 0), jnp.where(row == 1, p1, p2))
        carry_ref[f] = a[tm - 8:tm]
    cw = cw_ref[...]
    conv = cb_ref[...] + am2 * cw[0:1] + am1 * cw[1:2] + a * cw[2:3]
    gt = conv * _sigmoid(conv) * u
    y_ref[0] += _dot(gt.astype(BF16), wd_ref[...].astype(BF16))
    sa_ref[0, 0] = a[0:8] if rows_are_batch else a[tm - 8:tm]

    if final_norm:
        @pl.when(f == nf - 1)
        def _():
            y_ref[0] = _rms(y_ref[0], go_ref[...])


def conv_ffn(x, g, w_up, conv_w, conv_b, w_down, g_out, buf=None, final_norm=False):
    B, T, D = x.shape
    F = w_down.shape[0]
    rows_are_batch = buf is not None
    tm = _pick_tile(T, (512, 256, 128, 64, 32, 16, 8))
    tf = _pick_tile(F, (512, 256, 128))
    nf = F // tf
    if buf is None:
        b0 = b1 = jnp.zeros((8, F), F32)
        bspec = pl.BlockSpec((8, tf), lambda b, t, f: (0, f))
    else:
        b0, b1 = buf[0], buf[1]
        bspec = pl.BlockSpec((tm, tf), lambda b, t, f: (0, f))
    y, sa = pl.pallas_call(
        functools.partial(_ffn_body, rows_are_batch=rows_are_batch, final_norm=final_norm, nf=nf),
        grid=(B, T // tm, nf),
        in_specs=[pl.BlockSpec((1, tm, D), lambda b, t, f: (b, t, 0)),
                  pl.BlockSpec((1, D), lambda b, t, f: (0, 0)),
                  pl.BlockSpec((D, tf), lambda b, t, f: (0, f)),
                  pl.BlockSpec((D, tf), lambda b, t, f: (0, nf + f)),
                  pl.BlockSpec((3, tf), lambda b, t, f: (0, f)),
                  pl.BlockSpec((1, tf), lambda b, t, f: (0, f)),
                  pl.BlockSpec((tf, D), lambda b, t, f: (f, 0)),
                  pl.BlockSpec((1, D), lambda b, t, f: (0, 0)),
                  bspec, bspec],
        out_specs=[pl.BlockSpec((1, tm, D), lambda b, t, f: (b, t, 0)),
                   pl.BlockSpec((1, 1, 8, tf), lambda b, t, f: (b, t, 0, f))],
        out_shape=[jax.ShapeDtypeStruct((B, T, D), F32), jax.ShapeDtypeStruct((B, T // tm, 8, F), F32)],
        scratch_shapes=[pltpu.VMEM((tm, D), BF16), pltpu.VMEM((nf, 8, tf), F32)],
        compiler_params=_cparams(("parallel", "arbitrary", "arbitrary")),
        name="conv_ffn",
    )(x, g.reshape(1, D), w_up, w_up, conv_w, conv_b.reshape(1, F), w_down, g_out.reshape(1, D), b0, b1)
    return y, sa[:, -1]


def rope_tables(pos):
    inv = ROPE_THETA ** (-jnp.arange(ROPE_HALF, dtype=F32) / ROPE_HALF)
    ang = pos.astype(F32)[:, None] * inv[None, :]
    c, s = jnp.cos(ang), jnp.sin(ang)
    T = pos.shape[0]
    one = jnp.ones((T, HD - ROPE_DIM), F32)
    zero = jnp.zeros((T, HD - ROPE_DIM), F32)
    zh = jnp.zeros((T, ROPE_HALF), F32)
    cos = jnp.concatenate([c, c, one], axis=1)
    sin_lo = jnp.concatenate([-s, zh, zero], axis=1)
    sin_hi = jnp.concatenate([zh, s, zero], axis=1)
    return cos, sin_lo, sin_hi


def _rope_body(x_ref, c_ref, sl_ref, sh_ref, o_ref):
    c, sl, sh = c_ref[...], sl_ref[...], sh_ref[...]
    for h in range(x_ref.shape[1] // HD):
        x = x_ref[:, h * HD:(h + 1) * HD]
        o_ref[:, h * HD:(h + 1) * HD] = (x * c + pltpu.roll(x, HD - ROPE_HALF, 1) * sl
                                         + pltpu.roll(x, ROPE_HALF, 1) * sh)


def nsa_rope(z, tables, T):
    M = z.shape[0]
    tm = _pick_tile(T, (512, 256, 128, 64, 32, 16, 8))
    nt = T // tm
    W = 4 * HD

    def col(j):
        return j + 2 * (j >= 4).astype(jnp.int32) + (j >= 5).astype(jnp.int32)

    tspec = pl.BlockSpec((tm, HD), lambda i, j: (i % nt, 0))
    return pl.pallas_call(
        _rope_body,
        grid=(M // tm, 6),
        in_specs=[pl.BlockSpec((tm, W), lambda i, j: (i, col(j))), tspec, tspec, tspec],
        out_specs=pl.BlockSpec((tm, W), lambda i, j: (i, j)),
        out_shape=jax.ShapeDtypeStruct((M, 6 * W), F32),
        compiler_params=_cparams(("parallel", "arbitrary")),
        name="nsa_rope",
    )(z, *tables)


def _gelu(x):
    return 0.5 * x * (1.0 + jnp.tanh(math.sqrt(2.0 / math.pi) * (x + 0.044715 * x * x * x)))


def _pe_hidden(pe_ref, w1_ref):
    acc = jnp.zeros((8, HD), F32)
    for l in range(CMP_BLOCK):
        row = jnp.broadcast_to(pe_ref[0, l:l + 1, :], (8, HD))
        acc = acc + _dot(row.astype(BF16), w1_ref[0, l].astype(BF16))
    return acc[0:1]


def _cmp_body(x_ref, w1_ref, pe_ref, w2_ref, o_ref, *, nseg):
    parts = []
    for r in range(CMP_RATIO):
        acc = jnp.zeros((nseg, HD), F32)
        for j in range(CMP_STRIDE):
            xj = x_ref[pl.ds(j, nseg, stride=CMP_STRIDE), :]
            acc = acc + _dot(xj.astype(BF16), w1_ref[0, CMP_STRIDE * r + j].astype(BF16))
        parts.append(acc)
    hid = _pe_hidden(pe_ref, w1_ref) + parts[0] + pltpu.roll(parts[1], nseg - 1, 0)
    o_ref[0, 0, 0] = _dot(_gelu(hid).astype(BF16), w2_ref[0].astype(BF16))


def nsa_compress(z, w1, pe, w2, B, T):
    nseg = T // CMP_STRIDE
    return pl.pallas_call(
        functools.partial(_cmp_body, nseg=nseg),
        grid=(2, B, NSA_KVH),
        in_specs=[pl.BlockSpec((T, HD), lambda s, b, k: (b, NSA_QW // HD + NSA_KVH * s + k)),
                  pl.BlockSpec((1, CMP_BLOCK, HD, HD), lambda s, b, k: (s, 0, 0, 0)),
                  pl.BlockSpec((1, CMP_BLOCK, HD), lambda s, b, k: (s, 0, 0)),
                  pl.BlockSpec((1, HD, HD), lambda s, b, k: (s, 0, 0))],
        out_specs=pl.BlockSpec((1, 1, 1, nseg, HD), lambda s, b, k: (s, b, k, 0, 0)),
        out_shape=jax.ShapeDtypeStruct((2, B, NSA_KVH, nseg, HD), F32),
        compiler_params=_cparams(("parallel", "parallel", "parallel")),
        name="nsa_compress",
    )(z, w1, pe, w2)


def _overlap_t(nsp, nseg, n_slc):
    j = lax.broadcasted_iota(jnp.int32, (nsp, 1), 0)
    m = lax.broadcasted_iota(jnp.int32, (1, nseg), 1)
    ov = ((m * CMP_STRIDE <= j * SLC_BLOCK + SLC_BLOCK - 1) & (m * CMP_STRIDE + CMP_BLOCK - 1 >= j * SLC_BLOCK)
          & (m < nseg - CMP_RATIO + 1) & (j < n_slc))
    return ov.astype(F32)


def _rank_rows(score, n):
    j = lax.broadcasted_iota(jnp.int32, (score.shape[0], 1), 0)
    rank = jnp.zeros(score.shape, F32)
    for i in range(n):
        si = score[i:i + 1, :]
        rank = rank + ((si > score) | ((si == score) & (i < j))).astype(F32)
    return rank


def _nsa_attn_body(qraw_ref, qrot_ref, kc_ref, vc_ref, ks_ref, vs_ref, kw_ref, vw_ref, gate_ref, o_ref,
                   *kvb_ref, T, tq, tk):
    qi = pl.program_id(2)
    t0 = qi * tq
    scale = HD ** -0.5
    nseg = T // CMP_STRIDE
    n_slc = T // SLC_BLOCK
    nsp = max(8, n_slc)
    pos = t0 + lax.broadcasted_iota(jnp.int32, (tq, 1), 0)
    pos_row = t0 + lax.broadcasted_iota(jnp.int32, (1, tq), 1)

    @pl.when(qi == 0)
    def _():
        for dst, src in zip(kvb_ref, (ks_ref, vs_ref, kw_ref, vw_ref)):
            dst[...] = src[...].astype(BF16)

    ksb_ref, vsb_ref, kwb_ref, vwb_ref = kvb_ref

    def heads_on_rows(ref):
        return jnp.concatenate([ref[:, g * HD:(g + 1) * HD] for g in range(NSA_G)], axis=0)

    def per_head(x):
        return jnp.concatenate([x] * NSA_G, axis=0)

    q_raw = (heads_on_rows(qraw_ref) * scale).astype(BF16)
    q_rot = (heads_on_rows(qrot_ref) * scale).astype(BF16)

    kc = kc_ref[0, 0, 0].astype(BF16)
    vc = vc_ref[0, 0, 0].astype(BF16)
    m_idx = lax.broadcasted_iota(jnp.int32, (1, nseg), 1)
    valid = (m_idx * CMP_STRIDE + CMP_BLOCK - 1 <= pos) & (m_idx < nseg - CMP_RATIO + 1)
    s = _dot_nt(q_raw, kc) + per_head(jnp.where(valid, 0.0, NEG))
    e = jnp.exp(s - jnp.max(s, axis=-1, keepdims=True))
    p = e * (1.0 / jnp.sum(e, axis=-1, keepdims=True)) * per_head(valid.astype(F32))
    o_cmp = _dot(p.astype(BF16), vc)
    psum = p[0:tq]
    for g in range(1, NSA_G):
        psum = psum + p[g * tq:(g + 1) * tq]

    imp_t = lax.dot_general(_overlap_t(nsp, nseg, n_slc), psum, (((1,), (1,)), ((), ())),
                            preferred_element_type=F32, precision=lax.Precision.HIGHEST)
    j = lax.broadcasted_iota(jnp.int32, (nsp, 1), 0)
    cur = pos_row // SLC_BLOCK
    forced = (j == 0) | (j == cur) | (j == cur - 1)
    score = jnp.where(j * SLC_BLOCK <= pos_row, imp_t + FORCE_BONUS * forced.astype(F32), -1.0)
    score = jnp.where(j < n_slc, score, -2.0)
    sel_t = ((_rank_rows(score, n_slc) < min(SLC_TOP_N, n_slc)) & (j < n_slc)).astype(F32)
    if nsp < LANE:
        sel_t = jnp.concatenate([sel_t, jnp.zeros((LANE - nsp, tq), F32)], axis=0)
    sel = sel_t.T
    expand0 = (lax.broadcasted_iota(jnp.int32, (1, tk), 1) // SLC_BLOCK
               == lax.broadcasted_iota(jnp.int32, (LANE, 1), 0)).astype(BF16)

    def attend(k, v, bias, carry):
        m_old, l_old, acc = carry
        s = _dot_nt(q_rot, k) + per_head(bias)
        m_new = jnp.maximum(m_old, jnp.max(s, axis=-1, keepdims=True))
        alpha = jnp.exp(m_old - m_new)
        p = jnp.exp(s - m_new)
        return (m_new, alpha * l_old + jnp.sum(p, axis=-1, keepdims=True), alpha * acc + _dot(p.astype(BF16), v))

    R = NSA_G * tq
    init = (jnp.full((R, 1), NEG, F32), jnp.zeros((R, 1), F32), jnp.zeros((R, HD), F32))

    def slc_step(kt, carry):
        k0 = pl.multiple_of(kt * tk, tk)
        kpos = k0 + lax.broadcasted_iota(jnp.int32, (1, tk), 1)
        sel_kt = pltpu.roll(sel, (LANE - kt * (tk // SLC_BLOCK)) % LANE, 1).astype(BF16)
        bias = jnp.where((_dot(sel_kt, expand0) > 0.5) & (kpos <= pos), 0.0, NEG)
        return attend(ksb_ref[pl.ds(k0, tk), :], vsb_ref[pl.ds(k0, tk), :], bias, carry)

    slc = lax.fori_loop(0, (t0 + tq + tk - 1) // tk, slc_step, init)

    nkw = min(WINDOW + tq, T)
    k0w = pl.multiple_of(jnp.clip(t0 - WINDOW, 0, T - nkw), tq)
    d = pos - (k0w + lax.broadcasted_iota(jnp.int32, (1, nkw), 1))
    win = attend(kwb_ref[pl.ds(k0w, nkw), :], vwb_ref[pl.ds(k0w, nkw), :],
                 jnp.where((d >= 0) & (d < WINDOW), 0.0, NEG), init)

    gate = _sigmoid(gate_ref[...])
    o_slc = slc[2] / slc[1]
    o_win = win[2] / win[1]
    for g in range(NSA_G):
        rows = slice(g * tq, (g + 1) * tq)
        o_ref[:, g * HD:(g + 1) * HD] = (gate[:, 3 * g:3 * g + 1] * o_cmp[rows]
                                         + gate[:, 3 * g + 1:3 * g + 2] * o_slc[rows]
                                         + gate[:, 3 * g + 2:3 * g + 3] * o_win[rows])


def nsa_attn_prefill(z, rot, cmp, gates, B, T):
    tq = LANE
    tk = _pick_tile(T, (512, 256, 128))
    nq = T // tq
    nseg = T // CMP_STRIDE
    W = NSA_G * HD
    qspec = pl.BlockSpec((tq, W), lambda b, k, q: (b * nq + q, k))
    cspec = lambda s: pl.BlockSpec((1, 1, 1, nseg, HD), lambda b, k, q: (s, b, k, 0, 0))
    kvspec = lambda c0: pl.BlockSpec((T, HD), lambda b, k, q: (b, c0 + k))
    return pl.pallas_call(
        functools.partial(_nsa_attn_body, T=T, tq=tq, tk=tk),
        grid=(B, NSA_KVH, nq),
        in_specs=[qspec, qspec, cspec(0), cspec(1),
                  kvspec(NSA_QW // HD), kvspec((NSA_QW + 3 * W) // HD),
                  kvspec((NSA_QW + W) // HD), kvspec((NSA_QW + 5 * W) // HD),
                  pl.BlockSpec((tq, LANE), lambda b, k, q: (b * nq + q, k))],
        out_specs=qspec,
        out_shape=jax.ShapeDtypeStruct((B * T, NSA_QW), F32),
        scratch_shapes=[pltpu.VMEM((T, HD), BF16) for _ in range(4)],
        compiler_params=_cparams(("parallel", "parallel", "arbitrary")),
        name="nsa_attn",
    )(z, rot, cmp, cmp, rot, z, rot, z, gates)


def _nsa_gate_weight(w_in):
    K = w_in.shape[0]
    wg = w_in[:, NSA_MAIN:].reshape(K, NSA_KVH, 3 * NSA_G)
    return jnp.pad(wg, ((0, 0), (0, 0), (0, LANE - 3 * NSA_G))).reshape(K, NSA_KVH * LANE)


def nsa_layer_prefill(x, g, w_in, w1, pe, w2, w_out, tables, B, T):
    z = proj(x, g, w_in, n_out=NSA_MAIN)
    gates = proj(x, g, _nsa_gate_weight(w_in))
    rot = nsa_rope(z, tables, T)
    cmp = nsa_compress(z, w1, pe, w2, B, T)
    mix = nsa_attn_prefill(z, rot, cmp, gates, B, T)
    y = proj(mix, g, w_out, residual=x, norm=False)
    W = NSA_KVH * HD
    rows = jnp.concatenate([z[:, NSA_QW:NSA_QW + 2 * W], rot[:, NSA_QW:NSA_QW + W],
                            z[:, NSA_QW + 3 * W:NSA_QW + 4 * W]], axis=1).reshape(B, T, 4, NSA_KVH, HD)
    win = jnp.concatenate([rot[:, NSA_QW + W:], z[:, NSA_QW + 5 * W:]], axis=1).reshape(B, T, 2, NSA_KVH, HD)
    return y, rows, win[:, -min(WINDOW, T):]


def _round_robin(gens):
    while gens:
        alive = []
        for gen in gens:
            try:
                next(gen)
                alive.append(gen)
            except StopIteration:
                pass
        gens = alive


def _tri(c):
    return (lax.broadcasted_iota(jnp.int32, (c, 1), 0) >= lax.broadcasted_iota(jnp.int32, (1, c), 1)).astype(F32)


def _hgrn_body(q_ref, f_ref, i_ref, gz_ref, lb_ref, gn_ref, s0_ref, o_ref, s_ref, st_ref,
               *, C, SC, HP, t_valid, nt, layer):
    t = pl.program_id(2)
    Tt = q_ref.shape[0]

    @pl.when(t == 0)
    def _():
        for hp in range(HP):
            st_ref[hp] = s0_ref[0, hp].T

    lbp = lb_ref[...]
    e = jnp.exp(lbp - jnp.max(lbp, axis=0, keepdims=True))
    lb_all = jnp.sum(e[1:layer + 1], axis=0, keepdims=True) / jnp.sum(e, axis=0, keepdims=True)
    gn_all = gn_ref[...]
    tri = _tri(C)
    row8 = lax.broadcasted_iota(jnp.int32, (8, 1), 0)

    def head_chunk(hp, r0):
        cols = slice(hp * HD, (hp + 1) * HD)
        lb, gn = lb_all[:, cols], gn_all[:, cols]
        qz = q_ref[pl.ds(r0, C), cols]
        f = lb + (1.0 - lb) * _sigmoid(f_ref[pl.ds(r0, C), cols])
        v = i_ref[pl.ds(r0, C), cols]
        gz = gz_ref[pl.ds(r0, C), cols]
        k = 1.0 - f
        lf = jnp.log(f)
        if t_valid is not None:
            ok = (t * Tt + r0 + lax.broadcasted_iota(jnp.int32, (C, 1), 0)) < t_valid
            k = jnp.where(ok, k, 0.0)
            lf = jnp.where(ok, lf, 0.0)
        q = qz * _sigmoid(qz)
        b = _dot_hi(tri, lf)
        yield
        st = st_ref[hp]
        vb = v.astype(BF16)
        out = _dot_nt((q * jnp.exp(b)).astype(BF16), st.astype(BF16))
        bl = b[C - 1:C]
        st_ref[hp] = st * jnp.exp(bl) + _dot_tn(vb, (k * jnp.exp(bl - b)).astype(BF16))
        atts = []
        for I in range(1, C // SC):
            r = I * SC
            beta = b[r - 1:r]
            qt = q[r:r + SC] * jnp.exp(b[r:r + SC] - beta)
            kt = k[0:r] * jnp.exp(beta - b[0:r])
            atts.append(_dot_nt(qt.astype(BF16), kt.astype(BF16)))
        yield
        offdiag = [_dot(att.astype(BF16), vb[0:I * SC]) for I, att in enumerate(atts, 1)]
        pieces = []
        for I in range(C // SC):
            r = I * SC
            bI, qI, kI, vI = b[r:r + SC], q[r:r + SC], k[r:r + SC], v[r:r + SC]
            groups = [jnp.zeros((8, HD), F32) for _ in range(SC // 8)]
            for s in range(SC):
                for gi in range(s // 8, SC // 8):
                    rows = slice(gi * 8, gi * 8 + 8)
                    d = bI[rows] - bI[s:s + 1]
                    if gi == s // 8:
                        d = jnp.where(row8 >= s % 8, d, NEG)
                    att = jnp.sum(qI[rows] * jnp.exp(d) * kI[s:s + 1], axis=-1, keepdims=True)
                    groups[gi] = groups[gi] + att * vI[s:s + 1]
            pieces.extend(groups)
        yield
        intra = jnp.concatenate(pieces, axis=0)
        if offdiag:
            intra = intra + jnp.concatenate([jnp.zeros((SC, HD), F32)] + offdiag, axis=0)
        out = out + intra
        o_ref[pl.ds(r0, C), cols] = _rms(out, gn) * (gz * _sigmoid(gz))

    def chunk(c, carry):
        r0 = pl.multiple_of(c * C, C)
        _round_robin([head_chunk(hp, r0) for hp in range(HP)])
        return carry

    lax.fori_loop(0, Tt // C, chunk, 0)

    @pl.when(t == nt - 1)
    def _():
        for hp in range(HP):
            s_ref[0, hp] = st_ref[hp].T


def hgrn_scan(z, lb, layer, gn, s0, B, T, t_valid=None):
    depth = lb.shape[0]
    C = min(CHUNK, T)
    SC = min(SUB, C)
    Tt = _pick_tile(T, (512, 256, 128, 64, 32, 16))
    nt = T // Tt
    HP = HG_HEADS_PER_STEP
    ng = HG_HEADS // HP
    W = HP * HD
    zspec = lambda part: pl.BlockSpec((Tt, W), lambda b, h, t: (b * nt + t, part * ng + h))
    vspec = pl.BlockSpec((1, W), lambda b, h, t: (0, h))
    lspec = pl.BlockSpec((depth, W), lambda b, h, t: (0, h))
    sspec = pl.BlockSpec((1, HP, HD, HD), lambda b, h, t: (b, h, 0, 0))
    return pl.pallas_call(
        functools.partial(_hgrn_body, C=C, SC=SC, HP=HP, t_valid=t_valid, nt=nt, layer=layer),
        grid=(B, ng, nt),
        in_specs=[zspec(0), zspec(1), zspec(2), zspec(3), lspec, vspec, sspec],
        out_specs=[pl.BlockSpec((Tt, W), lambda b, h, t: (b * nt + t, h)), sspec],
        out_shape=[jax.ShapeDtypeStruct((B * T, HG_W), F32), jax.ShapeDtypeStruct((B, HG_HEADS, HD, HD), F32)],
        scratch_shapes=[pltpu.VMEM((HP, HD, HD), F32)],
        compiler_params=_cparams(("parallel", "parallel", "arbitrary")),
        name="hgrn_scan",
    )(z, z, z, z, lb, gn.reshape(1, HG_W), s0)


def _mlstm_body(q_ref, k_ref, v_ref, og_ref, gt_ref, bg_ref, c0_ref, n0_ref, m0_ref,
                y_ref, c_ref, n_ref, m_ref, cs_ref, ns_ref, ms_ref, *, C, HP, t_valid, nt):
    t = pl.program_id(2)
    Tt = q_ref.shape[0]

    @pl.when(t == 0)
    def _():
        cs_ref[...] = c0_ref[0]
        ns_ref[...] = n0_ref[0]
        ms_ref[...] = m0_ref[0]

    tri_b = lax.broadcasted_iota(jnp.int32, (C, 1), 0) >= lax.broadcasted_iota(jnp.int32, (1, C), 1)
    upper = (lax.broadcasted_iota(jnp.int32, (C, 1), 0) <= lax.broadcasted_iota(jnp.int32, (1, C), 1)).astype(F32)
    eye = (lax.broadcasted_iota(jnp.int32, (C, 1), 0) == lax.broadcasted_iota(jnp.int32, (1, C), 1)).astype(F32)
    tri = tri_b.astype(F32)
    ones = jnp.ones((C, C), F32)
    lane = lax.broadcasted_iota(jnp.int32, (1, LANE), 1)
    bg = bg_ref[...]

    def head_chunk(hp, r0, g):
        h = pl.program_id(1) * HP + hp
        kcols = slice(hp * ML_DK, (hp + 1) * ML_DK)
        vcols = slice(hp * ML_DV, (hp + 1) * ML_DV)
        ig = jnp.sum(jnp.where(lane == h, g, 0.0), axis=-1, keepdims=True)
        fg = jnp.sum(jnp.where(lane == ML_HEADS + h, g, 0.0), axis=-1, keepdims=True)
        lf = jnp.minimum(fg, 0.0) - jnp.log(1.0 + jnp.exp(-jnp.abs(fg)))
        if t_valid is not None:
            ok = (t * Tt + r0 + lax.broadcasted_iota(jnp.int32, (C, 1), 0)) < t_valid
            lf = jnp.where(ok, lf, 0.0)
            ig = jnp.where(ok, ig, NEG)
        lf_b = jnp.broadcast_to(lf, (C, C))
        ig_b = jnp.broadcast_to(ig, (C, C))
        b_col = _dot_hi(tri, lf_b)
        b_row_i = _dot_hi(ones, lf_b * upper - ig_b * eye)
        q = q_ref[pl.ds(r0, C), kcols]
        ks = k_ref[pl.ds(r0, C), kcols] * (ML_DK ** -0.5)
        v = v_ref[pl.ds(r0, C), vcols]
        qb, kb, vb = q.astype(BF16), ks.astype(BF16), v.astype(BF16)
        cs = cs_ref[hp]
        ns = ns_ref[hp]
        qk = _dot_nt(qb, kb)
        inter = _dot(qb, cs.astype(BF16))
        yield
        logw = jnp.where(tri_b, b_col - b_row_i, NEG)
        m_old = ms_ref[hp][:, 0:1]
        b1 = b_col[:, 0:1]
        m_prev = b1 + m_old
        m_t = jnp.maximum(m_prev, jnp.max(logw, axis=-1, keepdims=True))
        w0 = jnp.exp(m_prev - m_t)
        sqk = qk * jnp.exp(logw - m_t)
        intra = _dot(sqk.astype(BF16), vb)
        m_new = m_t[C - 1:C]
        b_last = b1[C - 1:C]
        wk = jnp.exp(b_last - b1 + ig - m_new)
        decay = jnp.exp(b_last + m_old - m_new)
        kw = ks * wk
        cs_ref[hp] = decay * cs + _dot_tn(kw.astype(BF16), vb)
        ns_ref[hp] = decay * ns + jnp.sum(kw, axis=0, keepdims=True)
        ms_ref[hp] = jnp.broadcast_to(m_new, (1, LANE))
        den = w0 * jnp.sum(q * ns, axis=-1, keepdims=True) + jnp.sum(sqk, axis=-1, keepdims=True)
        yield
        hc = (w0 * inter + intra) / jnp.maximum(jnp.abs(den), jnp.exp(-m_t))
        y_ref[pl.ds(r0, C), vcols] = _sigmoid(og_ref[pl.ds(r0, C), vcols]) * hc

    def chunk(c, carry):
        r0 = pl.multiple_of(c * C, C)
        g = gt_ref[pl.ds(r0, C), :] + bg
        g = ML_GATE_CAP * jnp.tanh(g / ML_GATE_CAP)
        _round_robin([head_chunk(hp, r0, g) for hp in range(HP)])
        return carry

    lax.fori_loop(0, Tt // C, chunk, 0)

    @pl.when(t == nt - 1)
    def _():
        c_ref[0] = cs_ref[...]
        n_ref[0] = ns_ref[...]
        m_ref[0] = ms_ref[...]


def mlstm_scan(z, gates, b_gate, c0, n0, m0, B, T, t_valid=None):
    C = min(CHUNK, T)
    Tt = _pick_tile(T, (512, 256, 128, 64, 32, 16))
    nt = T // Tt
    H = ML_HEADS
    row = lambda b, t: b * nt + t
    bg = jnp.pad(b_gate, (0, LANE - 2 * H)).reshape(1, LANE)
    n0 = n0.reshape(B, H, 1, ML_DK)
    m0 = jnp.broadcast_to(m0[:, :, None, None], (B, H, 1, LANE))
    HP = ML_HEADS_PER_STEP
    ng = H // HP
    kw, vw = HP * ML_DK, HP * ML_DV
    vec = lambda w: pl.BlockSpec((1, HP, 1, w), lambda b, h, t: (b, h, 0, 0))
    cspec = pl.BlockSpec((1, HP, ML_DK, ML_DV), lambda b, h, t: (b, h, 0, 0))
    y, c, n, m = pl.pallas_call(
        functools.partial(_mlstm_body, C=C, HP=HP, t_valid=t_valid, nt=nt),
        grid=(B, ng, nt),
        in_specs=[pl.BlockSpec((Tt, kw), lambda b, h, t: (row(b, t), h)),
                  pl.BlockSpec((Tt, kw), lambda b, h, t: (row(b, t), ng + h)),
                  pl.BlockSpec((Tt, vw), lambda b, h, t: (row(b, t), 2 * ML_QK // vw + h)),
                  pl.BlockSpec((Tt, vw), lambda b, h, t: (row(b, t), (2 * ML_QK + ML_VW) // vw + h)),
                  pl.BlockSpec((Tt, LANE), lambda b, h, t: (row(b, t), 0)),
                  pl.BlockSpec((1, LANE), lambda b, h, t: (0, 0)),
                  cspec, vec(ML_DK), vec(LANE)],
        out_specs=[pl.BlockSpec((Tt, vw), lambda b, h, t: (row(b, t), h)), cspec, vec(ML_DK), vec(LANE)],
        out_shape=[jax.ShapeDtypeStruct((B * T, ML_VW), F32), jax.ShapeDtypeStruct((B, H, ML_DK, ML_DV), F32),
                   jax.ShapeDtypeStruct((B, H, 1, ML_DK), F32), jax.ShapeDtypeStruct((B, H, 1, LANE), F32)],
        scratch_shapes=[pltpu.VMEM((HP, ML_DK, ML_DV), F32), pltpu.VMEM((HP, 1, ML_DK), F32),
                        pltpu.VMEM((HP, 1, LANE), F32)],
        compiler_params=_cparams(("parallel", "parallel", "arbitrary")),
        name="mlstm_scan",
    )(z, z, z, z, gates, bg, c0, n0, m0)
    return y, c, n.reshape(B, H, ML_DK), m[:, :, 0, 0]


PAGES_PER_STEP = 8
SEG_PER_PAGE = PAGE_SIZE // CMP_STRIDE
TILE_ROWS = 2 * NSA_KVH


def _cmp_paged_body(pt_ref, *refs):
    P = PAGES_PER_STEP
    x_refs, w_ref, o_ref, lhs_ref = refs[:P], refs[P], refs[P + 1], refs[P + 2]
    rows = SEG_PER_PAGE * TILE_ROWS
    for p in range(P):
        for j in range(CMP_STRIDE):
            xj = x_refs[p][pl.ds(j, SEG_PER_PAGE, stride=CMP_STRIDE)]
            lhs_ref[p * rows:(p + 1) * rows, j * HD:(j + 1) * HD] = xj.reshape(rows, HD).astype(BF16)
    res = _dot(lhs_ref[...], w_ref[...])
    c = lax.broadcasted_iota(jnp.int32, (P * rows, 1), 0) % TILE_ROWS
    o_ref[0] = jnp.where(c < NSA_KVH, res[:, :2 * HD], res[:, 2 * HD:])


def nsa_compress_paged(cache4, page_table, w1, a):
    B, n_pages = page_table.shape
    P = PAGES_PER_STEP
    rows = SEG_PER_PAGE * TILE_ROWS
    w_cat = w1.reshape(2, CMP_RATIO, CMP_STRIDE, HD, HD).transpose(2, 3, 0, 1, 4).reshape(CMP_STRIDE * HD, 4 * HD)
    xspec = lambda i: pl.BlockSpec((PAGE_SIZE, None, TILE_ROWS, HD),
                                   lambda b, s, pt: (pt[b, s * P + i], 2 * a, 0, 0))
    grid_spec = pltpu.PrefetchScalarGridSpec(
        num_scalar_prefetch=1,
        grid=(B, n_pages // P),
        in_specs=[xspec(i) for i in range(P)] + [pl.BlockSpec((CMP_STRIDE * HD, 4 * HD), lambda b, s, pt: (0, 0))],
        out_specs=pl.BlockSpec((1, P * rows, 2 * HD), lambda b, s, pt: (b, s, 0)),
        scratch_shapes=[pltpu.VMEM((P * rows, CMP_STRIDE * HD), BF16)])
    return pl.pallas_call(
        _cmp_paged_body,
        grid_spec=grid_spec,
        out_shape=jax.ShapeDtypeStruct((B, n_pages * rows, 2 * HD), F32),
        compiler_params=_cparams(("parallel", "arbitrary")),
        name="nsa_compress_paged",
    )(page_table, *([cache4] * P), w_cat.astype(BF16))


def _nsa_dec_cmp_body(part_ref, w1_ref, pe_ref, w2_ref, q_ref, o_ref, idx_ref, *, n_seg, n_slc, pos):
    R = n_seg * TILE_ROWS
    nsp = -(-n_slc // LANE) * LANE
    scale = HD ** -0.5
    p = part_ref[0]
    c_row = lax.broadcasted_iota(jnp.int32, (R, 1), 0) % TILE_ROWS
    is_k = c_row < NSA_KVH
    pe_hid = [_pe_hidden(pe_ref.at[pl.ds(s, 1)], w1_ref.at[pl.ds(s, 1)]) for s in range(2)]
    hid = p[:, :HD] + pltpu.roll(p[:, HD:], R - TILE_ROWS, 0) + jnp.where(is_k, pe_hid[0], pe_hid[1])
    ge = _gelu(hid).astype(BF16)
    kv = jnp.where(is_k, _dot(ge, w2_ref[0].astype(BF16)), _dot(ge, w2_ref[1].astype(BF16))).astype(BF16)
    q = q_ref[0].astype(BF16)
    col = lax.broadcasted_iota(jnp.int32, (1, R), 1)
    col_c, col_m = col % TILE_ROWS, col // TILE_ROWS
    head_kv = lax.broadcasted_iota(jnp.int32, (NSA_HEADS, 1), 0) // NSA_G
    valid = ((col_c == head_kv) & (col_m < n_seg - CMP_RATIO + 1)
             & (col_m * CMP_STRIDE + CMP_BLOCK - 1 <= pos))
    s = jnp.where(valid, _dot_nt(q, kv) * scale, NEG)
    e = jnp.exp(s - jnp.max(s, axis=-1, keepdims=True))
    prob = jnp.where(valid, e / jnp.sum(e, axis=-1, keepdims=True), 0.0)
    o_ref[0] = _dot(pltpu.roll(prob, NSA_KVH, 1).astype(BF16), kv)
    group = (lax.broadcasted_iota(jnp.int32, (8, 1), 0)
             == lax.broadcasted_iota(jnp.int32, (1, NSA_HEADS), 1) // NSA_G).astype(F32)
    psum = _dot_hi(group, prob)
    m_r = lax.broadcasted_iota(jnp.int32, (R, 1), 0) // TILE_ROWS
    j = lax.broadcasted_iota(jnp.int32, (1, nsp), 1)
    ov = ((m_r * CMP_STRIDE <= j * SLC_BLOCK + SLC_BLOCK - 1) & (m_r * CMP_STRIDE + CMP_BLOCK - 1 >= j * SLC_BLOCK)
          & (m_r < n_seg - CMP_RATIO + 1) & (j < n_slc)).astype(F32)
    imp = _dot_hi(psum, ov)
    cur = pos // SLC_BLOCK
    forced = (j == 0) | (j == cur) | (j == cur - 1)
    score = jnp.where(j * SLC_BLOCK <= pos, imp + FORCE_BONUS * forced.astype(F32), -1.0)
    score = jnp.where(j < n_slc, score, -2.0)
    jf = j.astype(F32)
    slot = lax.broadcasted_iota(jnp.int32, (1, LANE), 1)
    picked = jnp.zeros((8, LANE), F32)
    for it in range(min(SLC_TOP_N, n_slc)):
        best = jnp.max(score, axis=-1, keepdims=True)
        ix = jnp.min(jnp.where(score == best, jf, float(nsp)), axis=-1, keepdims=True)
        picked = jnp.where(slot == it, ix, picked)
        score = jnp.where(jf == ix, -3.0, score)
    idx_ref[0] = picked.astype(jnp.int32)


def nsa_dec_cmp(part, w1, pe, w2, q_raw, n_seg, n_slc, pos):
    B = part.shape[0]
    R = n_seg * TILE_ROWS
    full = lambda shape: pl.BlockSpec(shape, lambda b: (0,) * len(shape))
    return pl.pallas_call(
        functools.partial(_nsa_dec_cmp_body, n_seg=n_seg, n_slc=n_slc, pos=pos),
        grid=(B,),
        in_specs=[pl.BlockSpec((1, R, 2 * HD), lambda b: (b, 0, 0)),
                  full((2, CMP_BLOCK, HD, HD)), full((2, CMP_BLOCK, HD)), full((2, HD, HD)),
                  pl.BlockSpec((1, NSA_HEADS, HD), lambda b: (b, 0, 0))],
        out_specs=[pl.BlockSpec((1, NSA_HEADS, HD), lambda b: (b, 0, 0)),
                   pl.BlockSpec((1, 8, LANE), lambda b: (b, 0, 0))],
        out_shape=[jax.ShapeDtypeStruct((B, NSA_HEADS, HD), F32), jax.ShapeDtypeStruct((B, 8, LANE), jnp.int32)],
        compiler_params=_cparams(("parallel",)),
        name="nsa_dec_cmp",
    )(part, w1, pe, w2, q_raw)


def _nsa_dec_attn_body(pt_ref, idx_ref, *refs, n_cache_blocks, nsel, win_skip):
    blk_refs = refs[:nsel]
    q_ref, new_ref, win_ref, ocmp_ref, gate_ref, o_ref = refs[nsel:]
    b = pl.program_id(0)
    kh = pl.program_id(1)
    scale = HD ** -0.5
    q = q_ref[0, 0].astype(BF16)
    new = new_ref[0, 0]
    s_new = _dot_nt(q, new.astype(BF16)) * scale

    def attend(x, msk, s_own, v_own):
        s = [jnp.where(m, _dot_nt(q, xi) * scale, NEG) for xi, m in zip(x, msk)]
        mx = s_own
        for si in s:
            mx = jnp.maximum(mx, jnp.max(si, axis=-1, keepdims=True))
        p_own = jnp.exp(s_own - mx)
        num = p_own * v_own
        den = p_own
        for si, xi in zip(s, x):
            p = jnp.exp(si - mx)
            num = num + _dot(pltpu.roll(p, NSA_KVH, 1).astype(BF16), xi)
            den = den + jnp.sum(p, axis=-1, keepdims=True)
        return num / den

    col_c = lax.broadcasted_iota(jnp.int32, (1, SLC_BLOCK * TILE_ROWS), 1) % TILE_ROWS
    xs = [r[...].reshape(SLC_BLOCK * TILE_ROWS, HD).astype(BF16) for r in blk_refs]
    ms = [(col_c == kh) & (idx_ref[b, kh, n] < n_cache_blocks) for n in range(nsel)]
    o_slc = attend(xs, ms, s_new[:, 0:1], new[1:2])

    n_win = win_ref.shape[1]
    xw = win_ref[0].reshape(n_win * TILE_ROWS, HD).astype(BF16)
    colw = lax.broadcasted_iota(jnp.int32, (1, n_win * TILE_ROWS), 1)
    mw = ((colw % TILE_ROWS) == kh) & ((colw // TILE_ROWS) >= win_skip)
    o_win = attend([xw], [mw], s_new[:, 2:3], new[3:4])

    gate = _sigmoid(gate_ref[0, 0])
    lane = lax.broadcasted_iota(jnp.int32, (1, LANE), 1)
    row = lax.broadcasted_iota(jnp.int32, (8, 1), 0)
    pick = lambda c: jnp.sum(jnp.where(lane == 3 * row + c, gate, 0.0), axis=-1, keepdims=True)
    o_ref[0, 0] = pick(0) * ocmp_ref[0, 0] + pick(1) * o_slc + pick(2) * o_win


def nsa_dec_attn(cache5, win4, page_table, idx, q_rot, new_rows, o_cmp, gates, a, n_cache_blocks, win_skip):
    B, _, nsel = idx.shape
    n_win = win4.shape[1]

    def blk_spec(n):
        def blk_map(b, k, pt, ix):
            j = jnp.minimum(ix[b, k, n], n_cache_blocks - 1)
            return (pt[b, j // 2] * 2 + j % 2, 0, 2 * a + 1, 0, 0)
        return pl.BlockSpec((None, SLC_BLOCK, None, TILE_ROWS, HD), blk_map)

    grp = pl.BlockSpec((1, 1, 8, HD), lambda b, k, pt, ix: (b, k, 0, 0))
    grid_spec = pltpu.PrefetchScalarGridSpec(
        num_scalar_prefetch=2,
        grid=(B, NSA_KVH),
        in_specs=[blk_spec(n) for n in range(nsel)] + [
            grp, grp, pl.BlockSpec((1, n_win, TILE_ROWS, HD), lambda b, k, pt, ix: (a * B + b, 0, 0, 0)), grp, grp],
        out_specs=grp)
    return pl.pallas_call(
        functools.partial(_nsa_dec_attn_body, n_cache_blocks=n_cache_blocks, nsel=nsel, win_skip=win_skip),
        grid_spec=grid_spec,
        out_shape=jax.ShapeDtypeStruct((B, NSA_KVH, 8, HD), F32),
        compiler_params=_cparams(("parallel", "arbitrary")),
        name="nsa_dec_attn",
    )(page_table, idx, *([cache5] * nsel), q_rot, new_rows, win4, o_cmp, gates)


def _group_rows(x, B):
    x = x.reshape(B, NSA_KVH, NSA_G, HD)
    return jnp.pad(x, ((0, 0), (0, 0), (0, 8 - NSA_G), (0, 0)))


def nsa_layer_decode(x, g, w_in, w1, pe, w2, w_out, cache, win_state, page_table, a, B):
    R = x.shape[0]
    n_pool = cache.shape[0]
    n_pages = page_table.shape[1]
    past = n_pages * PAGE_SIZE
    n_seg = (past + 1) // CMP_STRIDE
    n_slc = -(-(past + 1) // SLC_BLOCK)
    W = NSA_KVH * HD
    z = proj(x, g, w_in, n_out=NSA_MAIN)
    gates = proj(x, g, _nsa_gate_weight(w_in))
    rot = nsa_rope(z, rope_tables(jnp.full((R,), past, jnp.int32)), R)
    part = nsa_compress_paged(cache.reshape(n_pool * PAGE_SIZE, 4, TILE_ROWS, HD), page_table, w1, a)
    o_cmp, idx = nsa_dec_cmp(part, w1, pe, w2, z[:B, :NSA_QW].reshape(B, NSA_HEADS, HD), n_seg, n_slc, past)
    nsel = min(SLC_TOP_N, n_slc)
    kv = z[:B, NSA_QW:].reshape(B, 6, NSA_KVH, HD)
    ks_new = rot[:B, NSA_QW:NSA_QW + W].reshape(B, NSA_KVH, HD)
    kw_new = rot[:B, NSA_QW + W:].reshape(B, NSA_KVH, HD)
    new_rows = jnp.stack([ks_new, kv[:, 3], kw_new, kv[:, 5]], axis=2)
    new_rows = jnp.pad(new_rows, ((0, 0), (0, 0), (0, 4), (0, 0)))
    n_win = win_state.shape[2]
    mix = nsa_dec_attn(cache.reshape(n_pool * 2, SLC_BLOCK, 4, TILE_ROWS, HD),
                       win_state.reshape(-1, n_win, TILE_ROWS, HD), page_table, idx[:, :NSA_KVH, :nsel],
                       _group_rows(rot[:B, :NSA_QW], B), new_rows, _group_rows(o_cmp.reshape(B, NSA_QW), B),
                       jnp.broadcast_to(gates[:B].reshape(B, NSA_KVH, 1, LANE), (B, NSA_KVH, 8, LANE)),
                       a, past // SLC_BLOCK, max(n_win + 1 - WINDOW, 0))
    mix = jnp.pad(mix[:, :, :NSA_G].reshape(B, NSA_QW), ((0, R - B), (0, 0)))
    y = proj(mix, g, w_out, residual=x, norm=False)
    rows = jnp.stack([kv[:, 0], kv[:, 1], ks_new, kv[:, 3]], axis=1)[:, None]
    win_new = jnp.stack([kw_new, kv[:, 5]], axis=1)[:, None]
    buf = jnp.concatenate([win_state[a], win_new], axis=1)
    return y, rows, buf[:, -min(WINDOW, n_win + 1):]


def kernel(x_prompt, x_sample, cache_nsa_kv, state_nsa_win, state_hgrn, state_mlstm_c, state_mlstm_n, state_mlstm_m, state_ffn_conv, page_table, norm_mix, norm_ffn, norm_out, nsa_w_in, nsa_cmp_w1, nsa_cmp_pe, nsa_cmp_w2, nsa_w_out, hgrn_w_in, hgrn_lb, hgrn_norm, hgrn_w_out, ml_w_in, ml_b_gate, ml_w_out, ffn_w_up, ffn_conv_w, ffn_conv_b, ffn_w_down):
    P = dict(norm_mix=norm_mix, norm_ffn=norm_ffn, norm_out=norm_out, nsa_w_in=nsa_w_in, nsa_cmp_w1=nsa_cmp_w1,
             nsa_cmp_pe=nsa_cmp_pe, nsa_cmp_w2=nsa_cmp_w2, nsa_w_out=nsa_w_out, hgrn_w_in=hgrn_w_in,
             hgrn_lb=hgrn_lb, hgrn_norm=hgrn_norm, hgrn_w_out=hgrn_w_out, ml_w_in=ml_w_in, ml_b_gate=ml_b_gate,
             ml_w_out=ml_w_out, ffn_w_up=ffn_w_up, ffn_conv_w=ffn_conv_w, ffn_conv_b=ffn_conv_b,
             ffn_w_down=ffn_w_down)
    past = dict(cache=cache_nsa_kv, win=state_nsa_win, hgrn=state_hgrn, c=state_mlstm_c, n=state_mlstm_n,
                m=state_mlstm_m, ffn=state_ffn_conv, page_table=page_table)
    yp, sp = _trunk(x_prompt, P, None)
    ys, ss = _trunk(x_sample, P, past)
    return (yp, ys, sp['nsa_kv'], ss['nsa_kv'], sp['nsa_win'], ss['nsa_win'], sp['hgrn'], ss['hgrn'],
            sp['c'], ss['c'], sp['n'], ss['n'], sp['m'], ss['m'], sp['ffn'], ss['ffn'])


DEC_ROWS = 16


def _spread(z, B, T):
    return jnp.zeros((B, T, z.shape[1]), F32).at[:, 0].set(z[:B]).reshape(B * T, z.shape[1])


def _gather_first(o, B, T, R):
    return jnp.pad(o.reshape(B, T, -1)[:, 0], ((0, R - B), (0, 0)))


def _trunk(x_in, P, past):
    B, T, D = x_in.shape
    decode = past is not None
    if decode:
        R = DEC_ROWS
        x = jnp.pad(x_in.reshape(B, D), ((0, R - B), (0, 0)))
        Ts = DEC_ROWS
    else:
        x = x_in.reshape(B * T, D)
        tables = rope_tables(jnp.arange(T, dtype=jnp.int32))
    rows_l, win_l, hg_l, c_l, n_l, m_l, ffn_l = [], [], [], [], [], [], []
    for i in range(DEPTH):
        kind, a = i % N_MIXERS, i // N_MIXERS
        g = P['norm_mix'][i]
        if kind == 0:
            args = (x, g, P['nsa_w_in'][a], P['nsa_cmp_w1'][a], P['nsa_cmp_pe'][a], P['nsa_cmp_w2'][a],
                    P['nsa_w_out'][a])
            if decode:
                x, rows, win = nsa_layer_decode(*args, past['cache'], past['win'], past['page_table'], a, B)
            else:
                x, rows, win = nsa_layer_prefill(*args, tables, B, T)
            rows_l.append(rows)
            win_l.append(win)
        elif kind == 1:
            z = proj(x, g, P['hgrn_w_in'][a])
            if decode:
                o, S = hgrn_scan(_spread(z, B, Ts), P['hgrn_lb'], i, P['hgrn_norm'][a], past['hgrn'][a], B, Ts, 1)
                o = _gather_first(o, B, Ts, R)
            else:
                o, S = hgrn_scan(z, P['hgrn_lb'], i, P['hgrn_norm'][a], jnp.zeros((B, HG_HEADS, HD, HD), F32), B, T)
            x = proj(o, g, P['hgrn_w_out'][a], residual=x, norm=False)
            hg_l.append(S)
        else:
            w = P['ml_w_in'][a]
            z = proj(x, g, w, n_out=ML_MAIN)
            gt = proj(x, g, jnp.pad(w[:, ML_MAIN:], ((0, 0), (0, LANE - 2 * ML_HEADS))))
            if decode:
                o, C, nv, m = mlstm_scan(_spread(z, B, Ts), _spread(gt, B, Ts), P['ml_b_gate'][a],
                                         past['c'][a], past['n'][a], past['m'][a], B, Ts, 1)
                o = _gather_first(o, B, Ts, R)
            else:
                o, C, nv, m = mlstm_scan(z, gt, P['ml_b_gate'][a], jnp.zeros((B, ML_HEADS, ML_DK, ML_DV), F32),
                                         jnp.zeros((B, ML_HEADS, ML_DK), F32), jnp.zeros((B, ML_HEADS), F32), B, T)
            x = proj(o, g, P['ml_w_out'][a], residual=x, norm=False)
            c_l.append(C)
            n_l.append(nv)
            m_l.append(m)
        fargs = (P['norm_ffn'][i], P['ffn_w_up'][i], P['ffn_conv_w'][i], P['ffn_conv_b'][i], P['ffn_w_down'][i],
                 P['norm_out'])
        last = i == DEPTH - 1
        if decode:
            buf = past['ffn'][i]
            bufp = jnp.pad(buf.transpose(1, 0, 2), ((0, 0), (0, R - B), (0, 0)))
            y, sa = conv_ffn(x[None], *fargs, buf=bufp, final_norm=last)
            x = y[0]
            ffn_l.append(jnp.stack([buf[:, 1], sa[0, :B]], axis=1))
        else:
            y, sa = conv_ffn(x.reshape(B, T, D), *fargs, final_norm=last)
            x = y.reshape(B * T, D)
            ffn_l.append(sa[:, 6:8])
    out = x[:B].reshape(B, 1, D) if decode else x.reshape(B, T, D)
    states = dict(nsa_kv=jnp.stack(rows_l, axis=2), nsa_win=jnp.stack(win_l, axis=0), hgrn=jnp.stack(hg_l, axis=0),
                  c=jnp.stack(c_l, axis=0), n=jnp.stack(n_l, axis=0), m=jnp.stack(m_l, axis=0),
                  ffn=jnp.stack(ffn_l, axis=0))
    return out, states
```

```python
import functools
import math

import jax
import jax.numpy as jnp
from jax import lax
from jax.experimental import pallas as pl
from jax.experimental.pallas import tpu as pltpu

F32 = jnp.float32
BF16 = jnp.bfloat16

D_MODEL = 2048
DEPTH = 4
PAGE_SIZE = 128
N_MIXERS = 3

HD = 128
NSA_HEADS = D_MODEL // HD
NSA_KVH = 4
NSA_G = NSA_HEADS // NSA_KVH
NSA_QW = NSA_HEADS * HD
NSA_KVW = 6 * NSA_KVH * HD
NSA_MAIN = NSA_QW + NSA_KVW
CMP_STRIDE = 16
CMP_RATIO = 2
CMP_BLOCK = CMP_STRIDE * CMP_RATIO
SLC_BLOCK = 64
SLC_TOP_N = 16
FORCE_BONUS = 1000.0
WINDOW = 512
ROPE_THETA = 500000.0
ROPE_DIM = HD // 4
ROPE_HALF = ROPE_DIM // 2

HG_HEADS = D_MODEL // HD
HG_W = HG_HEADS * HD

ML_HEADS = 8
ML_DK = D_MODEL // (2 * ML_HEADS)
ML_DV = D_MODEL // ML_HEADS
ML_QK = ML_HEADS * ML_DK
ML_VW = ML_HEADS * ML_DV
ML_MAIN = 2 * ML_QK + 2 * ML_VW
ML_GATE_CAP = 15.0
CHUNK = 64
SUB = 16
HG_HEADS_PER_STEP = 4
ML_HEADS_PER_STEP = 8

D_FF = (11 * D_MODEL) // 4
EPS = 1e-6
NEG = -1e30

LANE = 128
VMEM_LIMIT = 56 * 1024 * 1024
FFN_VMEM_LIMIT = 62 * 1024 * 1024


def _cparams(sem, vmem_limit=VMEM_LIMIT):
    return pltpu.CompilerParams(dimension_semantics=sem, vmem_limit_bytes=vmem_limit)


def _dot(a, b):
    return jnp.dot(a, b, preferred_element_type=F32)


def _dot_nt(a, b):
    return lax.dot_general(a, b, (((1,), (1,)), ((), ())), preferred_element_type=F32)


def _dot_tn(a, b):
    return lax.dot_general(a, b, (((0,), (0,)), ((), ())), preferred_element_type=F32)


def _dot_hi(a, b):
    return jnp.dot(a, b, preferred_element_type=F32, precision=lax.Precision.HIGHEST)


def _rms(x, g):
    return x * lax.rsqrt(jnp.mean(x * x, axis=-1, keepdims=True) + EPS) * g


def _sigmoid(x):
    return 1.0 / (1.0 + jnp.exp(-x))


def _pick_tile(n, cands):
    for c in cands:
        if n % c == 0:
            return c
    return n


def _proj_body(*refs, norm, residual):
    if residual:
        x_ref, g_ref, w_ref, r_ref, o_ref, xn_ref = refs
    else:
        x_ref, g_ref, w_ref, o_ref, xn_ref = refs

    @pl.when(pl.program_id(1) == 0)
    def _():
        x = x_ref[...]
        if norm:
            x = _rms(x, g_ref[...])
        xn_ref[...] = x.astype(BF16)

    acc = _dot(xn_ref[...], w_ref[...].astype(BF16))
    if residual:
        acc = acc + r_ref[...]
    o_ref[...] = acc


def proj(x, g, w, residual=None, norm=True, n_out=None, layer=None):
    M, K = x.shape
    N = w.shape[-1] if n_out is None else n_out
    tm = _pick_tile(M, (1024, 512, 256, 128, 64, 32, 16))
    tn = _pick_tile(N, (512, 384, 256, 128))
    if layer is None:
        wspec = pl.BlockSpec((K, tn), lambda i, j: (0, j))
    else:
        wspec = pl.BlockSpec((None, K, tn), lambda i, j: (layer, 0, j))
    in_specs = [pl.BlockSpec((tm, K), lambda i, j: (i, 0)),
                pl.BlockSpec((1, K), lambda i, j: (0, 0)),
                wspec]
    args = [x, g.reshape(1, K), w]
    if residual is not None:
        in_specs.append(pl.BlockSpec((tm, tn), lambda i, j: (i, j)))
        args.append(residual)
    return pl.pallas_call(
        functools.partial(_proj_body, norm=norm, residual=residual is not None),
        grid=(M // tm, N // tn),
        in_specs=in_specs,
        out_specs=pl.BlockSpec((tm, tn), lambda i, j: (i, j)),
        out_shape=jax.ShapeDtypeStruct((M, N), F32),
        scratch_shapes=[pltpu.VMEM((tm, K), BF16)],
        compiler_params=_cparams(("parallel", "arbitrary")),
        name="proj",
    )(*args)


def _ffn_body(x_ref, g_ref, wa_ref, wu_ref, cw_ref, cb_ref, wd_ref, go_ref, b0_ref, b1_ref,
              y_ref, sa_ref, xn_ref, carry_ref, *, rows_are_batch, final_norm, nf):
    t = pl.program_id(0)
    f = pl.program_id(1)
    S, tm = x_ref.shape[0], x_ref.shape[1]

    @pl.when(f == 0)
    def _():
        for s in range(S):
            xn_ref[s] = _rms(x_ref[s], g_ref[...]).astype(BF16)
            y_ref[s] = x_ref[s]

    wa = wa_ref[...].astype(BF16)
    wu = wu_ref[...].astype(BF16)
    wd = wd_ref[...].astype(BF16)
    cw = cw_ref[...]
    cb = cb_ref[...]
    row = lax.broadcasted_iota(jnp.int32, (tm, 1), 0)
    for s in range(S):
        xn = xn_ref[s]
        a = _dot(xn, wa)
        u = _dot(xn, wu)
        if rows_are_batch:
            am2 = b0_ref[...]
            am1 = b1_ref[...]
        else:
            prev = jnp.where(t == 0, 0.0, carry_ref[f, s])
            p1 = prev[7:8]
            p2 = prev[6:7]
            am1 = jnp.where(row >= 1, pltpu.roll(a, 1, 0), p1)
            am2 = jnp.where(row >= 2, pltpu.roll(a, 2, 0), jnp.where(row == 1, p1, p2))
            carry_ref[f, s] = a[tm - 8:tm]
        conv = cb + am2 * cw[0:1] + am1 * cw[1:2] + a * cw[2:3]
        gt = conv * _sigmoid(conv) * u
        y_ref[s] += _dot(gt.astype(BF16), wd)
        sa_ref[s, 0] = a[0:8] if rows_are_batch else a[tm - 8:tm]

    if final_norm:
        @pl.when(f == nf - 1)
        def _():
            for s in range(S):
                y_ref[s] = _rms(y_ref[s], go_ref[...])


def conv_ffn(x, g, w_up, conv_w, conv_b, w_down, layer, g_out, buf=None, final_norm=False):
    B, T, D = x.shape
    F = w_down.shape[1]
    rows_are_batch = buf is not None
    tm = _pick_tile(T, (512, 256, 128, 64, 32, 16, 8))
    tf = _pick_tile(F, (512, 256, 128))
    nf = F // tf
    if buf is None:
        b0 = b1 = jnp.zeros((8, F), F32)
        bspec = pl.BlockSpec((8, tf), lambda t, f: (0, f))
    else:
        b0, b1 = buf[0], buf[1]
        bspec = pl.BlockSpec((tm, tf), lambda t, f: (0, f))
    y, sa = pl.pallas_call(
        functools.partial(_ffn_body, rows_are_batch=rows_are_batch, final_norm=final_norm, nf=nf),
        grid=(T // tm, nf),
        in_specs=[pl.BlockSpec((B, tm, D), lambda t, f: (0, t, 0), pipeline_mode=pl.Buffered(1)),
                  pl.BlockSpec((1, D), lambda t, f: (0, 0)),
                  pl.BlockSpec((None, D, tf), lambda t, f: (layer, 0, f)),
                  pl.BlockSpec((None, D, tf), lambda t, f: (layer, 0, nf + f)),
                  pl.BlockSpec((None, 3, tf), lambda t, f: (layer, 0, f)),
                  pl.BlockSpec((None, 1, tf), lambda t, f: (layer, 0, f)),
                  pl.BlockSpec((None, tf, D), lambda t, f: (layer, f, 0)),
                  pl.BlockSpec((1, D), lambda t, f: (0, 0)),
                  bspec, bspec],
        out_specs=[pl.BlockSpec((B, tm, D), lambda t, f: (0, t, 0), pipeline_mode=pl.Buffered(1)),
                   pl.BlockSpec((B, 1, 8, tf), lambda t, f: (0, t, 0, f))],
        out_shape=[jax.ShapeDtypeStruct((B, T, D), F32), jax.ShapeDtypeStruct((B, T // tm, 8, F), F32)],
        scratch_shapes=[pltpu.VMEM((B, tm, D), BF16), pltpu.VMEM((nf, B, 8, tf), F32)],
        compiler_params=_cparams(("arbitrary", "arbitrary"), FFN_VMEM_LIMIT),
        name="conv_ffn",
    )(x, g.reshape(1, D), w_up, w_up, conv_w, conv_b[:, None, :], w_down, g_out.reshape(1, D), b0, b1)
    return y, sa[:, -1]


def rope_tables(pos):
    inv = ROPE_THETA ** (-jnp.arange(ROPE_HALF, dtype=F32) / ROPE_HALF)
    ang = pos.astype(F32)[:, None] * inv[None, :]
    c, s = jnp.cos(ang), jnp.sin(ang)
    T = pos.shape[0]
    one = jnp.ones((T, HD - ROPE_DIM), F32)
    zero = jnp.zeros((T, HD - ROPE_DIM), F32)
    zh = jnp.zeros((T, ROPE_HALF), F32)
    cos = jnp.concatenate([c, c, one], axis=1)
    sin_lo = jnp.concatenate([-s, zh, zero], axis=1)
    sin_hi = jnp.concatenate([zh, s, zero], axis=1)
    return cos, sin_lo, sin_hi


def _rope_body(x_ref, c_ref, sl_ref, sh_ref, o_ref):
    c, sl, sh = c_ref[...], sl_ref[...], sh_ref[...]
    for h in range(x_ref.shape[1] // HD):
        x = x_ref[:, h * HD:(h + 1) * HD]
        o_ref[:, h * HD:(h + 1) * HD] = (x * c + pltpu.roll(x, HD - ROPE_HALF, 1) * sl
                                         + pltpu.roll(x, ROPE_HALF, 1) * sh)


def nsa_rope(z, tables, T):
    M = z.shape[0]
    tm = _pick_tile(T, (512, 256, 128, 64, 32, 16, 8))
    nt = T // tm
    W = 4 * HD

    def col(j):
        return j + 2 * (j >= 4).astype(jnp.int32) + (j >= 5).astype(jnp.int32)

    tspec = pl.BlockSpec((tm, HD), lambda i, j: (i % nt, 0))
    return pl.pallas_call(
        _rope_body,
        grid=(M // tm, 6),
        in_specs=[pl.BlockSpec((tm, W), lambda i, j: (i, col(j))), tspec, tspec, tspec],
        out_specs=pl.BlockSpec((tm, W), lambda i, j: (i, j)),
        out_shape=jax.ShapeDtypeStruct((M, 6 * W), F32),
        compiler_params=_cparams(("parallel", "arbitrary")),
        name="nsa_rope",
    )(z, *tables)


def _gelu(x):
    return 0.5 * x * (1.0 + jnp.tanh(math.sqrt(2.0 / math.pi) * (x + 0.044715 * x * x * x)))


def _pe_hidden(pe_ref, w1_ref):
    acc = jnp.zeros((8, HD), F32)
    for l in range(CMP_BLOCK):
        row = jnp.broadcast_to(pe_ref[0, l:l + 1, :], (8, HD))
        acc = acc + _dot(row.astype(BF16), w1_ref[0, l].astype(BF16))
    return acc[0:1]


def _cmp_body(x_ref, w1_ref, pe_ref, w2_ref, o_ref, *, nseg):
    parts = []
    for r in range(CMP_RATIO):
        acc = jnp.zeros((nseg, HD), F32)
        for j in range(CMP_STRIDE):
            xj = x_ref[pl.ds(j, nseg, stride=CMP_STRIDE), :]
            acc = acc + _dot(xj.astype(BF16), w1_ref[0, CMP_STRIDE * r + j].astype(BF16))
        parts.append(acc)
    hid = _pe_hidden(pe_ref, w1_ref) + parts[0] + pltpu.roll(parts[1], nseg - 1, 0)
    o_ref[0, 0, 0] = _dot(_gelu(hid).astype(BF16), w2_ref[0].astype(BF16))


def nsa_compress(z, w1, pe, w2, B, T):
    nseg = T // CMP_STRIDE
    return pl.pallas_call(
        functools.partial(_cmp_body, nseg=nseg),
        grid=(2, B, NSA_KVH),
        in_specs=[pl.BlockSpec((T, HD), lambda s, b, k: (b, NSA_QW // HD + NSA_KVH * s + k)),
                  pl.BlockSpec((1, CMP_BLOCK, HD, HD), lambda s, b, k: (s, 0, 0, 0)),
                  pl.BlockSpec((1, CMP_BLOCK, HD), lambda s, b, k: (s, 0, 0)),
                  pl.BlockSpec((1, HD, HD), lambda s, b, k: (s, 0, 0))],
        out_specs=pl.BlockSpec((1, 1, 1, nseg, HD), lambda s, b, k: (s, b, k, 0, 0)),
        out_shape=jax.ShapeDtypeStruct((2, B, NSA_KVH, nseg, HD), F32),
        compiler_params=_cparams(("parallel", "parallel", "parallel")),
        name="nsa_compress",
    )(z, w1, pe, w2)


def _overlap_t(nsp, nseg, n_slc):
    j = lax.broadcasted_iota(jnp.int32, (nsp, 1), 0)
    m = lax.broadcasted_iota(jnp.int32, (1, nseg), 1)
    ov = ((m * CMP_STRIDE <= j * SLC_BLOCK + SLC_BLOCK - 1) & (m * CMP_STRIDE + CMP_BLOCK - 1 >= j * SLC_BLOCK)
          & (m < nseg - CMP_RATIO + 1) & (j < n_slc))
    return ov.astype(F32)


def _rank_rows(score, n):
    j = lax.broadcasted_iota(jnp.int32, (score.shape[0], 1), 0)
    rank = jnp.zeros(score.shape, F32)
    for i in range(n):
        si = score[i:i + 1, :]
        rank = rank + ((si > score) | ((si == score) & (i < j))).astype(F32)
    return rank


def _nsa_attn_body(qraw_ref, qrot_ref, kc_ref, vc_ref, ks_ref, vs_ref, kw_ref, vw_ref, gate_ref, o_ref,
                   *kvb_ref, T, tq, tk):
    qi = pl.program_id(2)
    t0 = qi * tq
    scale = HD ** -0.5
    nseg = T // CMP_STRIDE
    n_slc = T // SLC_BLOCK
    nsp = max(8, n_slc)
    pos = t0 + lax.broadcasted_iota(jnp.int32, (tq, 1), 0)
    pos_row = t0 + lax.broadcasted_iota(jnp.int32, (1, tq), 1)

    @pl.when(qi == 0)
    def _():
        for dst, src in zip(kvb_ref, (ks_ref, vs_ref, kw_ref, vw_ref)):
            dst[...] = src[...].astype(BF16)

    ksb_ref, vsb_ref, kwb_ref, vwb_ref = kvb_ref

    def heads_on_rows(ref):
        return jnp.concatenate([ref[:, g * HD:(g + 1) * HD] for g in range(NSA_G)], axis=0)

    def per_head(x):
        return jnp.concatenate([x] * NSA_G, axis=0)

    q_raw = (heads_on_rows(qraw_ref) * scale).astype(BF16)
    q_rot = (heads_on_rows(qrot_ref) * scale).astype(BF16)

    kc = kc_ref[0, 0, 0].astype(BF16)
    vc = vc_ref[0, 0, 0].astype(BF16)
    m_idx = lax.broadcasted_iota(jnp.int32, (1, nseg), 1)
    valid = (m_idx * CMP_STRIDE + CMP_BLOCK - 1 <= pos) & (m_idx < nseg - CMP_RATIO + 1)
    s = _dot_nt(q_raw, kc) + per_head(jnp.where(valid, 0.0, NEG))
    e = jnp.exp(s - jnp.max(s, axis=-1, keepdims=True))
    p = e * (1.0 / jnp.sum(e, axis=-1, keepdims=True)) * per_head(valid.astype(F32))
    o_cmp = _dot(p.astype(BF16), vc)
    psum = p[0:tq]
    for g in range(1, NSA_G):
        psum = psum + p[g * tq:(g + 1) * tq]

    imp_t = lax.dot_general(_overlap_t(nsp, nseg, n_slc), psum, (((1,), (1,)), ((), ())),
                            preferred_element_type=F32, precision=lax.Precision.HIGHEST)
    j = lax.broadcasted_iota(jnp.int32, (nsp, 1), 0)
    cur = pos_row // SLC_BLOCK
    forced = (j == 0) | (j == cur) | (j == cur - 1)
    score = jnp.where(j * SLC_BLOCK <= pos_row, imp_t + FORCE_BONUS * forced.astype(F32), -1.0)
    score = jnp.where(j < n_slc, score, -2.0)
    sel_t = ((_rank_rows(score, n_slc) < min(SLC_TOP_N, n_slc)) & (j < n_slc)).astype(F32)
    if nsp < LANE:
        sel_t = jnp.concatenate([sel_t, jnp.zeros((LANE - nsp, tq), F32)], axis=0)
    sel = sel_t.T
    expand_t = (lax.broadcasted_iota(jnp.int32, (tk, 1), 0) // SLC_BLOCK
                == lax.broadcasted_iota(jnp.int32, (1, LANE), 1)).astype(BF16)

    def update(s, v, carry):
        m_old, l_old, acc = carry
        m_new = jnp.maximum(m_old, jnp.max(s, axis=-1, keepdims=True))
        alpha = jnp.exp(m_old - m_new)
        p = jnp.exp(s - m_new)
        return (m_new, alpha * l_old + jnp.sum(p, axis=-1, keepdims=True), alpha * acc + _dot(p.astype(BF16), v))

    R = NSA_G * tq
    init = (jnp.full((R, 1), NEG, F32), jnp.zeros((R, 1), F32), jnp.zeros((R, HD), F32))

    def slc_scores(kt):
        k0 = pl.multiple_of(kt * tk, tk)
        sel_kt = pltpu.roll(sel, (LANE - kt * (tk // SLC_BLOCK)) % LANE, 1)
        q_aug = jnp.concatenate([q_rot, per_head(((sel_kt - 1.0) * -NEG).astype(BF16))], axis=1)
        k_aug = jnp.concatenate([ksb_ref[pl.ds(k0, tk), :], expand_t], axis=1)
        return _dot_nt(q_aug, k_aug), vsb_ref[pl.ds(k0, tk), :]

    def slc_step(kt, carry):
        s, v = slc_scores(kt)
        return update(s, v, carry)

    kt_diag = t0 // tk
    slc = lax.fori_loop(0, kt_diag, slc_step, init)
    s, v = slc_scores(kt_diag)
    kpos = kt_diag * tk + lax.broadcasted_iota(jnp.int32, (1, tk), 1)
    slc = update(s + per_head(jnp.where(kpos <= pos, 0.0, NEG)), v, slc)

    nkw = min(WINDOW + tq, T)
    k0w = pl.multiple_of(jnp.clip(t0 - WINDOW, 0, T - nkw), tq)
    d = pos - (k0w + lax.broadcasted_iota(jnp.int32, (1, nkw), 1))
    win = update(_dot_nt(q_rot, kwb_ref[pl.ds(k0w, nkw), :]) + per_head(jnp.where((d >= 0) & (d < WINDOW), 0.0, NEG)),
                 vwb_ref[pl.ds(k0w, nkw), :], init)

    gate = _sigmoid(gate_ref[...])
    o_slc = slc[2] / slc[1]
    o_win = win[2] / win[1]
    for g in range(NSA_G):
        rows = slice(g * tq, (g + 1) * tq)
        o_ref[:, g * HD:(g + 1) * HD] = (gate[:, 3 * g:3 * g + 1] * o_cmp[rows]
                                         + gate[:, 3 * g + 1:3 * g + 2] * o_slc[rows]
                                         + gate[:, 3 * g + 2:3 * g + 3] * o_win[rows])


def nsa_attn_prefill(z, rot, cmp, gates, B, T):
    tq = LANE
    tk = _pick_tile(T, (512, 256, 128))
    nq = T // tq
    nseg = T // CMP_STRIDE
    W = NSA_G * HD
    qspec = pl.BlockSpec((tq, W), lambda b, k, q: (b * nq + q, k))
    cspec = lambda s: pl.BlockSpec((1, 1, 1, nseg, HD), lambda b, k, q: (s, b, k, 0, 0))
    kvspec = lambda c0: pl.BlockSpec((T, HD), lambda b, k, q: (b, c0 + k))
    return pl.pallas_call(
        functools.partial(_nsa_attn_body, T=T, tq=tq, tk=tk),
        grid=(B, NSA_KVH, nq),
        in_specs=[qspec, qspec, cspec(0), cspec(1),
                  kvspec(NSA_QW // HD), kvspec((NSA_QW + 3 * W) // HD),
                  kvspec((NSA_QW + W) // HD), kvspec((NSA_QW + 5 * W) // HD),
                  pl.BlockSpec((tq, LANE), lambda b, k, q: (b * nq + q, k))],
        out_specs=qspec,
        out_shape=jax.ShapeDtypeStruct((B * T, NSA_QW), F32),
        scratch_shapes=[pltpu.VMEM((T, HD), BF16) for _ in range(4)],
        compiler_params=_cparams(("parallel", "parallel", "arbitrary")),
        name="nsa_attn",
    )(z, rot, cmp, cmp, rot, z, rot, z, gates)


def _nsa_gate_weight(w_in, a):
    K = w_in.shape[1]
    wg = w_in[a, :, NSA_MAIN:].reshape(K, NSA_KVH, 3 * NSA_G)
    return jnp.pad(wg, ((0, 0), (0, 0), (0, LANE - 3 * NSA_G))).reshape(K, NSA_KVH * LANE)


def nsa_layer_prefill(x, g, w_in, w1, pe, w2, w_out, a, tables, B, T):
    z = proj(x, g, w_in, n_out=NSA_MAIN, layer=a)
    gates = proj(x, g, _nsa_gate_weight(w_in, a))
    rot = nsa_rope(z, tables, T)
    cmp = nsa_compress(z, w1, pe, w2, B, T)
    mix = nsa_attn_prefill(z, rot, cmp, gates, B, T)
    y = proj(mix, g, w_out, residual=x, norm=False, layer=a)
    W = NSA_KVH * HD
    rows = jnp.concatenate([z[:, NSA_QW:NSA_QW + 2 * W], rot[:, NSA_QW:NSA_QW + W],
                            z[:, NSA_QW + 3 * W:NSA_QW + 4 * W]], axis=1).reshape(B, T, 4, NSA_KVH, HD)
    win = jnp.concatenate([rot[:, NSA_QW + W:], z[:, NSA_QW + 5 * W:]], axis=1).reshape(B, T, 2, NSA_KVH, HD)
    return y, rows, win[:, -min(WINDOW, T):]


def _round_robin(gens):
    while gens:
        alive = []
        for gen in gens:
            try:
                next(gen)
                alive.append(gen)
            except StopIteration:
                pass
        gens = alive


def _tri(c):
    return (lax.broadcasted_iota(jnp.int32, (c, 1), 0) >= lax.broadcasted_iota(jnp.int32, (1, c), 1)).astype(F32)


def _hgrn_body(q_ref, f_ref, i_ref, gz_ref, lb_ref, gn_ref, s0_ref, o_ref, s_ref, st_ref,
               *, C, SC, HP, t_valid, nt, layer):
    t = pl.program_id(2)
    Tt = q_ref.shape[0]

    @pl.when(t == 0)
    def _():
        for hp in range(HP):
            st_ref[hp] = s0_ref[0, hp].T

    lbp = lb_ref[...]
    e = jnp.exp(lbp - jnp.max(lbp, axis=0, keepdims=True))
    lb_all = jnp.sum(e[1:layer + 1], axis=0, keepdims=True) / jnp.sum(e, axis=0, keepdims=True)
    gn_all = gn_ref[...]
    tri = _tri(C)
    row8 = lax.broadcasted_iota(jnp.int32, (8, 1), 0)

    def head_chunk(hp, r0):
        cols = slice(hp * HD, (hp + 1) * HD)
        lb, gn = lb_all[:, cols], gn_all[:, cols]
        qz = q_ref[pl.ds(r0, C), cols]
        f = lb + (1.0 - lb) * _sigmoid(f_ref[pl.ds(r0, C), cols])
        v = i_ref[pl.ds(r0, C), cols]
        gz = gz_ref[pl.ds(r0, C), cols]
        k = 1.0 - f
        lf = jnp.log(f)
        if t_valid is not None:
            ok = (t * Tt + r0 + lax.broadcasted_iota(jnp.int32, (C, 1), 0)) < t_valid
            k = jnp.where(ok, k, 0.0)
            lf = jnp.where(ok, lf, 0.0)
        q = qz * _sigmoid(qz)
        b = _dot_hi(tri, lf)
        yield
        st = st_ref[hp]
        vb = v.astype(BF16)
        out = _dot_nt((q * jnp.exp(b)).astype(BF16), st.astype(BF16))
        bl = b[C - 1:C]
        st_ref[hp] = st * jnp.exp(bl) + _dot_tn(vb, (k * jnp.exp(bl - b)).astype(BF16))
        atts = []
        for I in range(1, C // SC):
            r = I * SC
            beta = b[r - 1:r]
            qt = q[r:r + SC] * jnp.exp(b[r:r + SC] - beta)
            kt = k[0:r] * jnp.exp(beta - b[0:r])
            atts.append(_dot_nt(qt.astype(BF16), kt.astype(BF16)))
        yield
        offdiag = [_dot(att.astype(BF16), vb[0:I * SC]) for I, att in enumerate(atts, 1)]
        pieces = []
        for I in range(C // SC):
            r = I * SC
            bI, qI, kI, vI = b[r:r + SC], q[r:r + SC], k[r:r + SC], v[r:r + SC]
            groups = [jnp.zeros((8, HD), F32) for _ in range(SC // 8)]
            for s in range(SC):
                for gi in range(s // 8, SC // 8):
                    rows = slice(gi * 8, gi * 8 + 8)
                    d = bI[rows] - bI[s:s + 1]
                    if gi == s // 8:
                        d = jnp.where(row8 >= s % 8, d, NEG)
                    att = jnp.sum(qI[rows] * jnp.exp(d) * kI[s:s + 1], axis=-1, keepdims=True)
                    groups[gi] = groups[gi] + att * vI[s:s + 1]
            pieces.extend(groups)
        yield
        intra = jnp.concatenate(pieces, axis=0)
        if offdiag:
            intra = intra + jnp.concatenate([jnp.zeros((SC, HD), F32)] + offdiag, axis=0)
        out = out + intra
        o_ref[pl.ds(r0, C), cols] = _rms(out, gn) * (gz * _sigmoid(gz))

    def chunk(c, carry):
        r0 = pl.multiple_of(c * C, C)
        _round_robin([head_chunk(hp, r0) for hp in range(HP)])
        return carry

    lax.fori_loop(0, Tt // C, chunk, 0)

    @pl.when(t == nt - 1)
    def _():
        for hp in range(HP):
            s_ref[0, hp] = st_ref[hp].T


def hgrn_scan(z, lb, layer, gn, s0, B, T, t_valid=None):
    depth = lb.shape[0]
    C = min(CHUNK, T)
    SC = min(SUB, C)
    Tt = _pick_tile(T, (512, 256, 128, 64, 32, 16))
    nt = T // Tt
    HP = HG_HEADS_PER_STEP
    ng = HG_HEADS // HP
    W = HP * HD
    zspec = lambda part: pl.BlockSpec((Tt, W), lambda b, h, t: (b * nt + t, part * ng + h))
    vspec = pl.BlockSpec((1, W), lambda b, h, t: (0, h))
    lspec = pl.BlockSpec((depth, W), lambda b, h, t: (0, h))
    sspec = pl.BlockSpec((1, HP, HD, HD), lambda b, h, t: (b, h, 0, 0))
    return pl.pallas_call(
        functools.partial(_hgrn_body, C=C, SC=SC, HP=HP, t_valid=t_valid, nt=nt, layer=layer),
        grid=(B, ng, nt),
        in_specs=[zspec(0), zspec(1), zspec(2), zspec(3), lspec, vspec, sspec],
        out_specs=[pl.BlockSpec((Tt, W), lambda b, h, t: (b * nt + t, h)), sspec],
        out_shape=[jax.ShapeDtypeStruct((B * T, HG_W), F32), jax.ShapeDtypeStruct((B, HG_HEADS, HD, HD), F32)],
        scratch_shapes=[pltpu.VMEM((HP, HD, HD), F32)],
        compiler_params=_cparams(("parallel", "parallel", "arbitrary")),
        name="hgrn_scan",
    )(z, z, z, z, lb, gn.reshape(1, HG_W), s0)


def _mlstm_body(q_ref, k_ref, v_ref, og_ref, gt_ref, bg_ref, c0_ref, n0_ref, m0_ref,
                y_ref, c_ref, n_ref, m_ref, cs_ref, ns_ref, ms_ref, *, C, HP, t_valid, nt):
    t = pl.program_id(2)
    Tt = q_ref.shape[0]

    @pl.when(t == 0)
    def _():
        cs_ref[...] = c0_ref[0]
        ns_ref[...] = n0_ref[0]
        ms_ref[...] = m0_ref[0]

    tri_b = lax.broadcasted_iota(jnp.int32, (C, 1), 0) >= lax.broadcasted_iota(jnp.int32, (1, C), 1)
    upper = (lax.broadcasted_iota(jnp.int32, (C, 1), 0) <= lax.broadcasted_iota(jnp.int32, (1, C), 1)).astype(F32)
    eye = (lax.broadcasted_iota(jnp.int32, (C, 1), 0) == lax.broadcasted_iota(jnp.int32, (1, C), 1)).astype(F32)
    tri = tri_b.astype(F32)
    ones = jnp.ones((C, C), F32)
    lane = lax.broadcasted_iota(jnp.int32, (1, LANE), 1)
    bg = bg_ref[...]

    def head_chunk(hp, r0, g):
        h = pl.program_id(1) * HP + hp
        kcols = slice(hp * ML_DK, (hp + 1) * ML_DK)
        vcols = slice(hp * ML_DV, (hp + 1) * ML_DV)
        ig = jnp.sum(jnp.where(lane == h, g, 0.0), axis=-1, keepdims=True)
        fg = jnp.sum(jnp.where(lane == ML_HEADS + h, g, 0.0), axis=-1, keepdims=True)
        lf = jnp.minimum(fg, 0.0) - jnp.log(1.0 + jnp.exp(-jnp.abs(fg)))
        if t_valid is not None:
            ok = (t * Tt + r0 + lax.broadcasted_iota(jnp.int32, (C, 1), 0)) < t_valid
            lf = jnp.where(ok, lf, 0.0)
            ig = jnp.where(ok, ig, NEG)
        lf_b = jnp.broadcast_to(lf, (C, C))
        ig_b = jnp.broadcast_to(ig, (C, C))
        b_col = _dot_hi(tri, lf_b)
        b_row_i = _dot_hi(ones, lf_b * upper - ig_b * eye)
        q = q_ref[pl.ds(r0, C), kcols]
        ks = k_ref[pl.ds(r0, C), kcols] * (ML_DK ** -0.5)
        v = v_ref[pl.ds(r0, C), vcols]
        qb, kb, vb = q.astype(BF16), ks.astype(BF16), v.astype(BF16)
        cs = cs_ref[hp]
        ns = ns_ref[hp]
        qk = _dot_nt(qb, kb)
        inter = _dot(qb, cs.astype(BF16))
        yield
        logw = jnp.where(tri_b, b_col - b_row_i, NEG)
        m_old = ms_ref[hp][:, 0:1]
        b1 = b_col[:, 0:1]
        m_prev = b1 + m_old
        m_t = jnp.maximum(m_prev, jnp.max(logw, axis=-1, keepdims=True))
        w0 = jnp.exp(m_prev - m_t)
        sqk = qk * jnp.exp(logw - m_t)
        intra = _dot(sqk.astype(BF16), vb)
        m_new = m_t[C - 1:C]
        b_last = b1[C - 1:C]
        wk = jnp.exp(b_last - b1 + ig - m_new)
        decay = jnp.exp(b_last + m_old - m_new)
        kw = ks * wk
        cs_ref[hp] = decay * cs + _dot_tn(kw.astype(BF16), vb)
        ns_ref[hp] = decay * ns + jnp.sum(kw, axis=0, keepdims=True)
        ms_ref[hp] = jnp.broadcast_to(m_new, (1, LANE))
        den = w0 * jnp.sum(q * ns, axis=-1, keepdims=True) + jnp.sum(sqk, axis=-1, keepdims=True)
        yield
        hc = (w0 * inter + intra) / jnp.maximum(jnp.abs(den), jnp.exp(-m_t))
        y_ref[pl.ds(r0, C), vcols] = _sigmoid(og_ref[pl.ds(r0, C), vcols]) * hc

    def chunk(c, carry):
        r0 = pl.multiple_of(c * C, C)
        g = gt_ref[pl.ds(r0, C), :] + bg
        g = ML_GATE_CAP * jnp.tanh(g / ML_GATE_CAP)
        _round_robin([head_chunk(hp, r0, g) for hp in range(HP)])
        return carry

    lax.fori_loop(0, Tt // C, chunk, 0)

    @pl.when(t == nt - 1)
    def _():
        c_ref[0] = cs_ref[...]
        n_ref[0] = ns_ref[...]
        m_ref[0] = ms_ref[...]


def mlstm_scan(z, gates, b_gate, c0, n0, m0, B, T, t_valid=None):
    C = min(CHUNK, T)
    Tt = _pick_tile(T, (512, 256, 128, 64, 32, 16))
    nt = T // Tt
    H = ML_HEADS
    row = lambda b, t: b * nt + t
    bg = jnp.pad(b_gate, (0, LANE - 2 * H)).reshape(1, LANE)
    n0 = n0.reshape(B, H, 1, ML_DK)
    m0 = jnp.broadcast_to(m0[:, :, None, None], (B, H, 1, LANE))
    HP = ML_HEADS_PER_STEP
    ng = H // HP
    kw, vw = HP * ML_DK, HP * ML_DV
    vec = lambda w: pl.BlockSpec((1, HP, 1, w), lambda b, h, t: (b, h, 0, 0))
    cspec = pl.BlockSpec((1, HP, ML_DK, ML_DV), lambda b, h, t: (b, h, 0, 0))
    y, c, n, m = pl.pallas_call(
        functools.partial(_mlstm_body, C=C, HP=HP, t_valid=t_valid, nt=nt),
        grid=(B, ng, nt),
        in_specs=[pl.BlockSpec((Tt, kw), lambda b, h, t: (row(b, t), h)),
                  pl.BlockSpec((Tt, kw), lambda b, h, t: (row(b, t), ng + h)),
                  pl.BlockSpec((Tt, vw), lambda b, h, t: (row(b, t), 2 * ML_QK // vw + h)),
                  pl.BlockSpec((Tt, vw), lambda b, h, t: (row(b, t), (2 * ML_QK + ML_VW) // vw + h)),
                  pl.BlockSpec((Tt, LANE), lambda b, h, t: (row(b, t), 0)),
                  pl.BlockSpec((1, LANE), lambda b, h, t: (0, 0)),
                  cspec, vec(ML_DK), vec(LANE)],
        out_specs=[pl.BlockSpec((Tt, vw), lambda b, h, t: (row(b, t), h)), cspec, vec(ML_DK), vec(LANE)],
        out_shape=[jax.ShapeDtypeStruct((B * T, ML_VW), F32), jax.ShapeDtypeStruct((B, H, ML_DK, ML_DV), F32),
                   jax.ShapeDtypeStruct((B, H, 1, ML_DK), F32), jax.ShapeDtypeStruct((B, H, 1, LANE), F32)],
        scratch_shapes=[pltpu.VMEM((HP, ML_DK, ML_DV), F32), pltpu.VMEM((HP, 1, ML_DK), F32),
                        pltpu.VMEM((HP, 1, LANE), F32)],
        compiler_params=_cparams(("parallel", "parallel", "arbitrary")),
        name="mlstm_scan",
    )(z, z, z, z, gates, bg, c0, n0, m0)
    return y, c, n.reshape(B, H, ML_DK), m[:, :, 0, 0]


PAGES_PER_STEP = 16
SEG_PER_PAGE = PAGE_SIZE // CMP_STRIDE
TILE_ROWS = 2 * NSA_KVH


def _cmp_paged_body(pt_ref, *refs):
    P = PAGES_PER_STEP
    x_refs, w_ref, o_ref, lhs_ref = refs[:P], refs[P], refs[P + 1], refs[P + 2]
    rows = SEG_PER_PAGE * TILE_ROWS
    for p in range(P):
        for j in range(CMP_STRIDE):
            xj = x_refs[p][pl.ds(j, SEG_PER_PAGE, stride=CMP_STRIDE)]
            lhs_ref[p * rows:(p + 1) * rows, j * HD:(j + 1) * HD] = xj.reshape(rows, HD).astype(BF16)
    res = _dot(lhs_ref[...], w_ref[...])
    c = lax.broadcasted_iota(jnp.int32, (P * rows, 1), 0) % TILE_ROWS
    o_ref[0] = jnp.where(c < NSA_KVH, res[:, :2 * HD], res[:, 2 * HD:])


def nsa_compress_paged(cache4, page_table, w1, a):
    B, n_pages = page_table.shape
    P = PAGES_PER_STEP
    rows = SEG_PER_PAGE * TILE_ROWS
    w_cat = w1.reshape(2, CMP_RATIO, CMP_STRIDE, HD, HD).transpose(2, 3, 0, 1, 4).reshape(CMP_STRIDE * HD, 4 * HD)
    xspec = lambda i: pl.BlockSpec((PAGE_SIZE, None, TILE_ROWS, HD),
                                   lambda b, s, pt: (pt[b, s * P + i], 2 * a, 0, 0))
    grid_spec = pltpu.PrefetchScalarGridSpec(
        num_scalar_prefetch=1,
        grid=(B, n_pages // P),
        in_specs=[xspec(i) for i in range(P)] + [pl.BlockSpec((CMP_STRIDE * HD, 4 * HD), lambda b, s, pt: (0, 0))],
        out_specs=pl.BlockSpec((1, P * rows, 2 * HD), lambda b, s, pt: (b, s, 0)),
        scratch_shapes=[pltpu.VMEM((P * rows, CMP_STRIDE * HD), BF16)])
    return pl.pallas_call(
        _cmp_paged_body,
        grid_spec=grid_spec,
        out_shape=jax.ShapeDtypeStruct((B, n_pages * rows, 2 * HD), F32),
        compiler_params=_cparams(("parallel", "arbitrary")),
        name="nsa_compress_paged",
    )(page_table, *([cache4] * P), w_cat.astype(BF16))


def _nsa_dec_cmp_body(part_ref, w1_ref, pe_ref, w2_ref, q_ref, o_ref, idx_ref, *, n_seg, n_slc, pos):
    R = n_seg * TILE_ROWS
    nsp = -(-n_slc // LANE) * LANE
    scale = HD ** -0.5
    p = part_ref[0]
    c_row = lax.broadcasted_iota(jnp.int32, (R, 1), 0) % TILE_ROWS
    is_k = c_row < NSA_KVH
    pe_hid = [_pe_hidden(pe_ref.at[pl.ds(s, 1)], w1_ref.at[pl.ds(s, 1)]) for s in range(2)]
    hid = p[:, :HD] + pltpu.roll(p[:, HD:], R - TILE_ROWS, 0) + jnp.where(is_k, pe_hid[0], pe_hid[1])
    ge = _gelu(hid).astype(BF16)
    kv = jnp.where(is_k, _dot(ge, w2_ref[0].astype(BF16)), _dot(ge, w2_ref[1].astype(BF16))).astype(BF16)
    q = q_ref[0].astype(BF16)
    col = lax.broadcasted_iota(jnp.int32, (1, R), 1)
    col_c, col_m = col % TILE_ROWS, col // TILE_ROWS
    head_kv = lax.broadcasted_iota(jnp.int32, (NSA_HEADS, 1), 0) // NSA_G
    valid = ((col_c == head_kv) & (col_m < n_seg - CMP_RATIO + 1)
             & (col_m * CMP_STRIDE + CMP_BLOCK - 1 <= pos))
    s = jnp.where(valid, _dot_nt(q, kv) * scale, NEG)
    e = jnp.exp(s - jnp.max(s, axis=-1, keepdims=True))
    prob = jnp.where(valid, e / jnp.sum(e, axis=-1, keepdims=True), 0.0)
    o_ref[0] = _dot(pltpu.roll(prob, NSA_KVH, 1).astype(BF16), kv)
    group = (lax.broadcasted_iota(jnp.int32, (8, 1), 0)
             == lax.broadcasted_iota(jnp.int32, (1, NSA_HEADS), 1) // NSA_G).astype(F32)
    psum = _dot_hi(group, prob)
    m_r = lax.broadcasted_iota(jnp.int32, (R, 1), 0) // TILE_ROWS
    j = lax.broadcasted_iota(jnp.int32, (1, nsp), 1)
    ov = ((m_r * CMP_STRIDE <= j * SLC_BLOCK + SLC_BLOCK - 1) & (m_r * CMP_STRIDE + CMP_BLOCK - 1 >= j * SLC_BLOCK)
          & (m_r < n_seg - CMP_RATIO + 1) & (j < n_slc)).astype(F32)
    imp = _dot_hi(psum, ov)
    cur = pos // SLC_BLOCK
    forced = (j == 0) | (j == cur) | (j == cur - 1)
    score = jnp.where(j * SLC_BLOCK <= pos, imp + FORCE_BONUS * forced.astype(F32), -1.0)
    score = jnp.where(j < n_slc, score, -2.0)
    jf = j.astype(F32)
    slot = lax.broadcasted_iota(jnp.int32, (1, LANE), 1)
    picked = jnp.zeros((8, LANE), F32)
    for it in range(min(SLC_TOP_N, n_slc)):
        best = jnp.max(score, axis=-1, keepdims=True)
        ix = jnp.min(jnp.where(score == best, jf, float(nsp)), axis=-1, keepdims=True)
        picked = jnp.where(slot == it, ix, picked)
        score = jnp.where(jf == ix, -3.0, score)
    idx_ref[0] = picked.astype(jnp.int32)


def nsa_dec_cmp(part, w1, pe, w2, q_raw, n_seg, n_slc, pos):
    B = part.shape[0]
    R = n_seg * TILE_ROWS
    full = lambda shape: pl.BlockSpec(shape, lambda b: (0,) * len(shape))
    return pl.pallas_call(
        functools.partial(_nsa_dec_cmp_body, n_seg=n_seg, n_slc=n_slc, pos=pos),
        grid=(B,),
        in_specs=[pl.BlockSpec((1, R, 2 * HD), lambda b: (b, 0, 0)),
                  full((2, CMP_BLOCK, HD, HD)), full((2, CMP_BLOCK, HD)), full((2, HD, HD)),
                  pl.BlockSpec((1, NSA_HEADS, HD), lambda b: (b, 0, 0))],
        out_specs=[pl.BlockSpec((1, NSA_HEADS, HD), lambda b: (b, 0, 0)),
                   pl.BlockSpec((1, 8, LANE), lambda b: (b, 0, 0))],
        out_shape=[jax.ShapeDtypeStruct((B, NSA_HEADS, HD), F32), jax.ShapeDtypeStruct((B, 8, LANE), jnp.int32)],
        compiler_params=_cparams(("parallel",)),
        name="nsa_dec_cmp",
    )(part, w1, pe, w2, q_raw)


def _nsa_dec_attn_body(pt_ref, idx_ref, *refs, n_cache_blocks, nsel, win_skip):
    blk_refs = refs[:nsel]
    q_ref, new_ref, win_ref, ocmp_ref, gate_ref, o_ref = refs[nsel:]
    b = pl.program_id(0)
    kh = pl.program_id(1)
    scale = HD ** -0.5
    q = q_ref[0, 0].astype(BF16)
    new = new_ref[0, 0]
    s_new = _dot_nt(q, new.astype(BF16)) * scale

    def attend(x, msk, s_own, v_own):
        s = [jnp.where(m, _dot_nt(q, xi) * scale, NEG) for xi, m in zip(x, msk)]
        mx = s_own
        for si in s:
            mx = jnp.maximum(mx, jnp.max(si, axis=-1, keepdims=True))
        p_own = jnp.exp(s_own - mx)
        num = p_own * v_own
        den = p_own
        for si, xi in zip(s, x):
            p = jnp.exp(si - mx)
            num = num + _dot(pltpu.roll(p, NSA_KVH, 1).astype(BF16), xi)
            den = den + jnp.sum(p, axis=-1, keepdims=True)
        return num / den

    col_c = lax.broadcasted_iota(jnp.int32, (1, SLC_BLOCK * TILE_ROWS), 1) % TILE_ROWS
    xs = [r[...].reshape(SLC_BLOCK * TILE_ROWS, HD).astype(BF16) for r in blk_refs]
    ms = [(col_c == kh) & (idx_ref[b, kh, n] < n_cache_blocks) for n in range(nsel)]
    o_slc = attend(xs, ms, s_new[:, 0:1], new[1:2])

    n_win = win_ref.shape[1]
    xw = win_ref[0].reshape(n_win * TILE_ROWS, HD).astype(BF16)
    colw = lax.broadcasted_iota(jnp.int32, (1, n_win * TILE_ROWS), 1)
    mw = ((colw % TILE_ROWS) == kh) & ((colw // TILE_ROWS) >= win_skip)
    o_win = attend([xw], [mw], s_new[:, 2:3], new[3:4])

    gate = _sigmoid(gate_ref[0, 0])
    lane = lax.broadcasted_iota(jnp.int32, (1, LANE), 1)
    row = lax.broadcasted_iota(jnp.int32, (8, 1), 0)
    pick = lambda c: jnp.sum(jnp.where(lane == 3 * row + c, gate, 0.0), axis=-1, keepdims=True)
    o_ref[0, 0] = pick(0) * ocmp_ref[0, 0] + pick(1) * o_slc + pick(2) * o_win


def nsa_dec_attn(cache5, win4, page_table, idx, q_rot, new_rows, o_cmp, gates, a, n_cache_blocks, win_skip):
    B, _, nsel = idx.shape
    n_win = win4.shape[1]

    def blk_spec(n):
        def blk_map(b, k, pt, ix):
            j = jnp.minimum(ix[b, k, n], n_cache_blocks - 1)
            return (pt[b, j // 2] * 2 + j % 2, 0, 2 * a + 1, 0, 0)
        return pl.BlockSpec((None, SLC_BLOCK, None, TILE_ROWS, HD), blk_map)

    grp = pl.BlockSpec((1, 1, 8, HD), lambda b, k, pt, ix: (b, k, 0, 0))
    grid_spec = pltpu.PrefetchScalarGridSpec(
        num_scalar_prefetch=2,
        grid=(B, NSA_KVH),
        in_specs=[blk_spec(n) for n in range(nsel)] + [
            grp, grp, pl.BlockSpec((1, n_win, TILE_ROWS, HD), lambda b, k, pt, ix: (a * B + b, 0, 0, 0)), grp, grp],
        out_specs=grp)
    return pl.pallas_call(
        functools.partial(_nsa_dec_attn_body, n_cache_blocks=n_cache_blocks, nsel=nsel, win_skip=win_skip),
        grid_spec=grid_spec,
        out_shape=jax.ShapeDtypeStruct((B, NSA_KVH, 8, HD), F32),
        compiler_params=_cparams(("parallel", "arbitrary")),
        name="nsa_dec_attn",
    )(page_table, idx, *([cache5] * nsel), q_rot, new_rows, win4, o_cmp, gates)


def _group_rows(x, B):
    x = x.reshape(B, NSA_KVH, NSA_G, HD)
    return jnp.pad(x, ((0, 0), (0, 0), (0, 8 - NSA_G), (0, 0)))


def nsa_layer_decode(x, g, w_in, w1, pe, w2, w_out, cache, win_state, page_table, a, B):
    R = x.shape[0]
    n_pool = cache.shape[0]
    n_pages = page_table.shape[1]
    past = n_pages * PAGE_SIZE
    n_seg = (past + 1) // CMP_STRIDE
    n_slc = -(-(past + 1) // SLC_BLOCK)
    W = NSA_KVH * HD
    z = proj(x, g, w_in, n_out=NSA_MAIN, layer=a)
    gates = proj(x, g, _nsa_gate_weight(w_in, a))
    rot = nsa_rope(z, rope_tables(jnp.full((R,), past, jnp.int32)), R)
    part = nsa_compress_paged(cache.reshape(n_pool * PAGE_SIZE, 4, TILE_ROWS, HD), page_table, w1, a)
    o_cmp, idx = nsa_dec_cmp(part, w1, pe, w2, z[:B, :NSA_QW].reshape(B, NSA_HEADS, HD), n_seg, n_slc, past)
    nsel = min(SLC_TOP_N, n_slc)
    kv = z[:B, NSA_QW:].reshape(B, 6, NSA_KVH, HD)
    ks_new = rot[:B, NSA_QW:NSA_QW + W].reshape(B, NSA_KVH, HD)
    kw_new = rot[:B, NSA_QW + W:].reshape(B, NSA_KVH, HD)
    new_rows = jnp.stack([ks_new, kv[:, 3], kw_new, kv[:, 5]], axis=2)
    new_rows = jnp.pad(new_rows, ((0, 0), (0, 0), (0, 4), (0, 0)))
    n_win = win_state.shape[2]
    mix = nsa_dec_attn(cache.reshape(n_pool * 2, SLC_BLOCK, 4, TILE_ROWS, HD),
                       win_state.reshape(-1, n_win, TILE_ROWS, HD), page_table, idx[:, :NSA_KVH, :nsel],
                       _group_rows(rot[:B, :NSA_QW], B), new_rows, _group_rows(o_cmp.reshape(B, NSA_QW), B),
                       jnp.broadcast_to(gates[:B].reshape(B, NSA_KVH, 1, LANE), (B, NSA_KVH, 8, LANE)),
                       a, past // SLC_BLOCK, max(n_win + 1 - WINDOW, 0))
    mix = jnp.pad(mix[:, :, :NSA_G].reshape(B, NSA_QW), ((0, R - B), (0, 0)))
    y = proj(mix, g, w_out, residual=x, norm=False, layer=a)
    rows = jnp.stack([kv[:, 0], kv[:, 1], ks_new, kv[:, 3]], axis=1)[:, None]
    win_new = jnp.stack([kw_new, kv[:, 5]], axis=1)[:, None]
    buf = jnp.concatenate([win_state[a], win_new], axis=1)
    return y, rows, buf[:, -min(WINDOW, n_win + 1):]


def kernel(x_prompt, x_sample, cache_nsa_kv, state_nsa_win, state_hgrn, state_mlstm_c, state_mlstm_n, state_mlstm_m, state_ffn_conv, page_table, norm_mix, norm_ffn, norm_out, nsa_w_in, nsa_cmp_w1, nsa_cmp_pe, nsa_cmp_w2, nsa_w_out, hgrn_w_in, hgrn_lb, hgrn_norm, hgrn_w_out, ml_w_in, ml_b_gate, ml_w_out, ffn_w_up, ffn_conv_w, ffn_conv_b, ffn_w_down):
    P = dict(norm_mix=norm_mix, norm_ffn=norm_ffn, norm_out=norm_out, nsa_w_in=nsa_w_in, nsa_cmp_w1=nsa_cmp_w1,
             nsa_cmp_pe=nsa_cmp_pe, nsa_cmp_w2=nsa_cmp_w2, nsa_w_out=nsa_w_out, hgrn_w_in=hgrn_w_in,
             hgrn_lb=hgrn_lb, hgrn_norm=hgrn_norm, hgrn_w_out=hgrn_w_out, ml_w_in=ml_w_in, ml_b_gate=ml_b_gate,
             ml_w_out=ml_w_out, ffn_w_up=ffn_w_up, ffn_conv_w=ffn_conv_w, ffn_conv_b=ffn_conv_b,
             ffn_w_down=ffn_w_down)
    past = dict(cache=cache_nsa_kv, win=state_nsa_win, hgrn=state_hgrn, c=state_mlstm_c, n=state_mlstm_n,
                m=state_mlstm_m, ffn=state_ffn_conv, page_table=page_table)
    yp, sp = _trunk(x_prompt, P, None)
    ys, ss = _trunk(x_sample, P, past)
    return (yp, ys, sp['nsa_kv'], ss['nsa_kv'], sp['nsa_win'], ss['nsa_win'], sp['hgrn'], ss['hgrn'],
            sp['c'], ss['c'], sp['n'], ss['n'], sp['m'], ss['m'], sp['ffn'], ss['ffn'])


DEC_ROWS = 16


def _spread(z, B, T):
    return jnp.zeros((B, T, z.shape[1]), F32).at[:, 0].set(z[:B]).reshape(B * T, z.shape[1])


def _gather_first(o, B, T, R):
    return jnp.pad(o.reshape(B, T, -1)[:, 0], ((0, R - B), (0, 0)))


def _trunk(x_in, P, past):
    B, T, D = x_in.shape
    decode = past is not None
    if decode:
        R = DEC_ROWS
        x = jnp.pad(x_in.reshape(B, D), ((0, R - B), (0, 0)))
        Ts = DEC_ROWS
    else:
        x = x_in.reshape(B * T, D)
        tables = rope_tables(jnp.arange(T, dtype=jnp.int32))
    rows_l, win_l, hg_l, c_l, n_l, m_l, ffn_l = [], [], [], [], [], [], []
    for i in range(DEPTH):
        kind, a = i % N_MIXERS, i // N_MIXERS
        g = P['norm_mix'][i]
        if kind == 0:
            args = (x, g, P['nsa_w_in'], P['nsa_cmp_w1'][a], P['nsa_cmp_pe'][a], P['nsa_cmp_w2'][a], P['nsa_w_out'])
            if decode:
                x, rows, win = nsa_layer_decode(*args, past['cache'], past['win'], past['page_table'], a, B)
            else:
                x, rows, win = nsa_layer_prefill(*args, a, tables, B, T)
            rows_l.append(rows)
            win_l.append(win)
        elif kind == 1:
            z = proj(x, g, P['hgrn_w_in'], layer=a)
            if decode:
                o, S = hgrn_scan(_spread(z, B, Ts), P['hgrn_lb'], i, P['hgrn_norm'][a], past['hgrn'][a], B, Ts, 1)
                o = _gather_first(o, B, Ts, R)
            else:
                o, S = hgrn_scan(z, P['hgrn_lb'], i, P['hgrn_norm'][a], jnp.zeros((B, HG_HEADS, HD, HD), F32), B, T)
            x = proj(o, g, P['hgrn_w_out'], residual=x, norm=False, layer=a)
            hg_l.append(S)
        else:
            w = P['ml_w_in']
            z = proj(x, g, w, n_out=ML_MAIN, layer=a)
            gt = proj(x, g, jnp.pad(w[a, :, ML_MAIN:], ((0, 0), (0, LANE - 2 * ML_HEADS))))
            if decode:
                o, C, nv, m = mlstm_scan(_spread(z, B, Ts), _spread(gt, B, Ts), P['ml_b_gate'][a],
                                         past['c'][a], past['n'][a], past['m'][a], B, Ts, 1)
                o = _gather_first(o, B, Ts, R)
            else:
                o, C, nv, m = mlstm_scan(z, gt, P['ml_b_gate'][a], jnp.zeros((B, ML_HEADS, ML_DK, ML_DV), F32),
                                         jnp.zeros((B, ML_HEADS, ML_DK), F32), jnp.zeros((B, ML_HEADS), F32), B, T)
            x = proj(o, g, P['ml_w_out'], residual=x, norm=False, layer=a)
            c_l.append(C)
            n_l.append(nv)
            m_l.append(m)
        fargs = (P['norm_ffn'][i], P['ffn_w_up'], P['ffn_conv_w'], P['ffn_conv_b'], P['ffn_w_down'], i,
                 P['norm_out'])
        last = i == DEPTH - 1
        if decode:
            buf = past['ffn'][i]
            bufp = jnp.pad(buf.transpose(1, 0, 2), ((0, 0), (0, R - B), (0, 0)))
            y, sa = conv_ffn(x[None], *fargs, buf=bufp, final_norm=last)
            x = y[0]
            ffn_l.append(jnp.stack([buf[:, 1], sa[0, :B]], axis=1))
        else:
            y, sa = conv_ffn(x.reshape(B, T, D), *fargs, final_norm=last)
            x = y.reshape(B * T, D)
            ffn_l.append(sa[:, 6:8])
    out = x[:B].reshape(B, 1, D) if decode else x.reshape(B, T, D)
    states = dict(nsa_kv=jnp.stack(rows_l, axis=2), nsa_win=jnp.stack(win_l, axis=0), hgrn=jnp.stack(hg_l, axis=0),
                  c=jnp.stack(c_l, axis=0), n=jnp.stack(n_l, axis=0), m=jnp.stack(m_l, axis=0),
                  ffn=jnp.stack(ffn_l, axis=0))
    return out, states
```

```python
import functools
import math

import jax
import jax.numpy as jnp
from jax import lax
from jax.experimental import pallas as pl
from jax.experimental.pallas import tpu as pltpu

F32 = jnp.float32
BF16 = jnp.bfloat16

D_MODEL = 2048
DEPTH = 4
PAGE_SIZE = 128
N_MIXERS = 3

HD = 128
NSA_HEADS = D_MODEL // HD
NSA_KVH = 4
NSA_G = NSA_HEADS // NSA_KVH
NSA_QW = NSA_HEADS * HD
NSA_KVW = 6 * NSA_KVH * HD
NSA_MAIN = NSA_QW + NSA_KVW
CMP_STRIDE = 16
CMP_RATIO = 2
CMP_BLOCK = CMP_STRIDE * CMP_RATIO
SLC_BLOCK = 64
SLC_TOP_N = 16
FORCE_BONUS = 1000.0
WINDOW = 512
ROPE_THETA = 500000.0
ROPE_DIM = HD // 4
ROPE_HALF = ROPE_DIM // 2

HG_HEADS = D_MODEL // HD
HG_W = HG_HEADS * HD

ML_HEADS = 8
ML_DK = D_MODEL // (2 * ML_HEADS)
ML_DV = D_MODEL // ML_HEADS
ML_QK = ML_HEADS * ML_DK
ML_VW = ML_HEADS * ML_DV
ML_MAIN = 2 * ML_QK + 2 * ML_VW
ML_GATE_CAP = 15.0
CHUNK = 64
SUB = 16
HG_HEADS_PER_STEP = 4
ML_HEADS_PER_STEP = 8

D_FF = (11 * D_MODEL) // 4
EPS = 1e-6
NEG = -1e30

LANE = 128
VMEM_LIMIT = 56 * 1024 * 1024
FFN_VMEM_LIMIT = 62 * 1024 * 1024


def _cparams(sem, vmem_limit=VMEM_LIMIT):
    return pltpu.CompilerParams(dimension_semantics=sem, vmem_limit_bytes=vmem_limit)


def _dot(a, b):
    return jnp.dot(a, b, preferred_element_type=F32)


def _dot_nt(a, b):
    return lax.dot_general(a, b, (((1,), (1,)), ((), ())), preferred_element_type=F32)


def _dot_tn(a, b):
    return lax.dot_general(a, b, (((0,), (0,)), ((), ())), preferred_element_type=F32)


def _dot_hi(a, b):
    return jnp.dot(a, b, preferred_element_type=F32, precision=lax.Precision.HIGHEST)


def _rms(x, g):
    return x * lax.rsqrt(jnp.mean(x * x, axis=-1, keepdims=True) + EPS) * g


def _sigmoid(x):
    return 1.0 / (1.0 + jnp.exp(-x))


def _pick_tile(n, cands):
    for c in cands:
        if n % c == 0:
            return c
    return n


def _proj_body(*refs, norm, residual, w_rows_out):
    if residual:
        x_ref, g_ref, w_ref, r_ref, o_ref, xn_ref = refs
    else:
        x_ref, g_ref, w_ref, o_ref, xn_ref = refs

    @pl.when(pl.program_id(1) == 0)
    def _():
        x = x_ref[...]
        if norm:
            x = _rms(x, g_ref[...])
        xn_ref[...] = x.astype(BF16)

    w = w_ref[...].astype(BF16)
    acc = _dot_nt(xn_ref[...], w) if w_rows_out else _dot(xn_ref[...], w)
    if residual:
        acc = acc + r_ref[...]
    o_ref[...] = acc


def proj(x, g, w, residual=None, norm=True, n_out=None, layer=None, w_rows_out=False):
    M, K = x.shape
    N = w.shape[1 if w_rows_out else -1] if n_out is None else n_out
    tm = _pick_tile(M, (1024, 512, 256, 128, 64, 32, 16))
    tn = _pick_tile(N, (512, 384, 256, 128))
    if w_rows_out:
        wspec = pl.BlockSpec((None, tn, K), lambda i, j: (layer, j, 0))
    elif layer is None:
        wspec = pl.BlockSpec((K, tn), lambda i, j: (0, j))
    else:
        wspec = pl.BlockSpec((None, K, tn), lambda i, j: (layer, 0, j))
    in_specs = [pl.BlockSpec((tm, K), lambda i, j: (i, 0)),
                pl.BlockSpec((1, K), lambda i, j: (0, 0)),
                wspec]
    args = [x, g.reshape(1, K), w]
    if residual is not None:
        in_specs.append(pl.BlockSpec((tm, tn), lambda i, j: (i, j)))
        args.append(residual)
    return pl.pallas_call(
        functools.partial(_proj_body, norm=norm, residual=residual is not None, w_rows_out=w_rows_out),
        grid=(M // tm, N // tn),
        in_specs=in_specs,
        out_specs=pl.BlockSpec((tm, tn), lambda i, j: (i, j)),
        out_shape=jax.ShapeDtypeStruct((M, N), F32),
        scratch_shapes=[pltpu.VMEM((tm, K), BF16)],
        compiler_params=_cparams(("parallel", "arbitrary")),
        name="proj",
    )(*args)


def _ffn_body(x_ref, g_ref, wa_ref, wu_ref, cw_ref, cb_ref, wd_ref, go_ref, b0_ref, b1_ref,
              y_ref, sa_ref, xn_ref, carry_ref, *, rows_are_batch, final_norm, nf):
    t = pl.program_id(0)
    f = pl.program_id(1)
    S, tm = x_ref.shape[0], x_ref.shape[1]

    @pl.when(f == 0)
    def _():
        for s in range(S):
            xn_ref[s] = _rms(x_ref[s], g_ref[...]).astype(BF16)
            y_ref[s] = x_ref[s]

    wa = wa_ref[...].astype(BF16)
    wu = wu_ref[...].astype(BF16)
    wd = wd_ref[...].astype(BF16)
    cw = cw_ref[...]
    cb = cb_ref[...]
    row = lax.broadcasted_iota(jnp.int32, (tm, 1), 0)
    for s in range(S):
        xn = xn_ref[s]
        a = _dot(xn, wa)
        u = _dot(xn, wu)
        if rows_are_batch:
            am2 = b0_ref[...]
            am1 = b1_ref[...]
        else:
            prev = jnp.where(t == 0, 0.0, carry_ref[f, s])
            p1 = prev[7:8]
            p2 = prev[6:7]
            am1 = jnp.where(row >= 1, pltpu.roll(a, 1, 0), p1)
            am2 = jnp.where(row >= 2, pltpu.roll(a, 2, 0), jnp.where(row == 1, p1, p2))
            carry_ref[f, s] = a[tm - 8:tm]
        conv = cb + am2 * cw[0:1] + am1 * cw[1:2] + a * cw[2:3]
        gt = conv * _sigmoid(conv) * u
        y_ref[s] += _dot(gt.astype(BF16), wd)
        sa_ref[s, 0] = a[0:8] if rows_are_batch else a[tm - 8:tm]

    if final_norm:
        @pl.when(f == nf - 1)
        def _():
            for s in range(S):
                y_ref[s] = _rms(y_ref[s], go_ref[...])


def conv_ffn(x, g, w_up, conv_w, conv_b, w_down, layer, g_out, buf=None, final_norm=False):
    B, T, D = x.shape
    F = w_down.shape[1]
    rows_are_batch = buf is not None
    tm = _pick_tile(T, (512, 256, 128, 64, 32, 16, 8))
    tf = _pick_tile(F, (512, 256, 128))
    nf = F // tf
    if buf is None:
        b0 = b1 = jnp.zeros((8, F), F32)
        bspec = pl.BlockSpec((8, tf), lambda t, f: (0, f))
    else:
        b0, b1 = buf[0], buf[1]
        bspec = pl.BlockSpec((tm, tf), lambda t, f: (0, f))
    y, sa = pl.pallas_call(
        functools.partial(_ffn_body, rows_are_batch=rows_are_batch, final_norm=final_norm, nf=nf),
        grid=(T // tm, nf),
        in_specs=[pl.BlockSpec((B, tm, D), lambda t, f: (0, t, 0), pipeline_mode=pl.Buffered(1)),
                  pl.BlockSpec((1, D), lambda t, f: (0, 0)),
                  pl.BlockSpec((None, D, tf), lambda t, f: (layer, 0, f)),
                  pl.BlockSpec((None, D, tf), lambda t, f: (layer, 0, nf + f)),
                  pl.BlockSpec((None, 3, tf), lambda t, f: (layer, 0, f)),
                  pl.BlockSpec((None, 1, tf), lambda t, f: (layer, 0, f)),
                  pl.BlockSpec((None, tf, D), lambda t, f: (layer, f, 0)),
                  pl.BlockSpec((1, D), lambda t, f: (0, 0)),
                  bspec, bspec],
        out_specs=[pl.BlockSpec((B, tm, D), lambda t, f: (0, t, 0), pipeline_mode=pl.Buffered(1)),
                   pl.BlockSpec((B, 1, 8, tf), lambda t, f: (0, t, 0, f))],
        out_shape=[jax.ShapeDtypeStruct((B, T, D), F32), jax.ShapeDtypeStruct((B, T // tm, 8, F), F32)],
        scratch_shapes=[pltpu.VMEM((B, tm, D), BF16), pltpu.VMEM((nf, B, 8, tf), F32)],
        compiler_params=_cparams(("arbitrary", "arbitrary"), FFN_VMEM_LIMIT),
        name="conv_ffn",
    )(x, g.reshape(1, D), w_up, w_up, conv_w, conv_b[:, None, :], w_down, g_out.reshape(1, D), b0, b1)
    return y, sa[:, -1]


def rope_tables(pos):
    inv = ROPE_THETA ** (-jnp.arange(ROPE_HALF, dtype=F32) / ROPE_HALF)
    ang = pos.astype(F32)[:, None] * inv[None, :]
    c, s = jnp.cos(ang), jnp.sin(ang)
    T = pos.shape[0]
    one = jnp.ones((T, HD - ROPE_DIM), F32)
    zero = jnp.zeros((T, HD - ROPE_DIM), F32)
    zh = jnp.zeros((T, ROPE_HALF), F32)
    cos = jnp.concatenate([c, c, one], axis=1)
    sin_lo = jnp.concatenate([-s, zh, zero], axis=1)
    sin_hi = jnp.concatenate([zh, s, zero], axis=1)
    return cos, sin_lo, sin_hi


def _rope_body(x_ref, c_ref, sl_ref, sh_ref, o_ref):
    c, sl, sh = c_ref[...], sl_ref[...], sh_ref[...]
    for h in range(x_ref.shape[1] // HD):
        x = x_ref[:, h * HD:(h + 1) * HD]
        o_ref[:, h * HD:(h + 1) * HD] = (x * c + pltpu.roll(x, HD - ROPE_HALF, 1) * sl
                                         + pltpu.roll(x, ROPE_HALF, 1) * sh)


def nsa_rope(z, tables, T):
    M = z.shape[0]
    tm = _pick_tile(T, (512, 256, 128, 64, 32, 16, 8))
    nt = T // tm
    W = 4 * HD

    def col(j):
        return j + 2 * (j >= 4).astype(jnp.int32) + (j >= 5).astype(jnp.int32)

    tspec = pl.BlockSpec((tm, HD), lambda i, j: (i % nt, 0))
    return pl.pallas_call(
        _rope_body,
        grid=(M // tm, 6),
        in_specs=[pl.BlockSpec((tm, W), lambda i, j: (i, col(j))), tspec, tspec, tspec],
        out_specs=pl.BlockSpec((tm, W), lambda i, j: (i, j)),
        out_shape=jax.ShapeDtypeStruct((M, 6 * W), F32),
        compiler_params=_cparams(("parallel", "arbitrary")),
        name="nsa_rope",
    )(z, *tables)


def _gelu(x):
    return 0.5 * x * (1.0 + jnp.tanh(math.sqrt(2.0 / math.pi) * (x + 0.044715 * x * x * x)))


def _pe_hidden(pe_ref, w1_ref):
    acc = jnp.zeros((8, HD), F32)
    for l in range(CMP_BLOCK):
        row = jnp.broadcast_to(pe_ref[0, l:l + 1, :], (8, HD))
        acc = acc + _dot(row.astype(BF16), w1_ref[0, l].astype(BF16))
    return acc[0:1]


def _cmp_body(x_ref, w1_ref, pe_ref, w2_ref, o_ref, *, nseg):
    parts = []
    for r in range(CMP_RATIO):
        acc = jnp.zeros((nseg, HD), F32)
        for j in range(CMP_STRIDE):
            xj = x_ref[pl.ds(j, nseg, stride=CMP_STRIDE), :]
            acc = acc + _dot(xj.astype(BF16), w1_ref[0, CMP_STRIDE * r + j].astype(BF16))
        parts.append(acc)
    hid = _pe_hidden(pe_ref, w1_ref) + parts[0] + pltpu.roll(parts[1], nseg - 1, 0)
    o_ref[0, 0, 0] = _dot(_gelu(hid).astype(BF16), w2_ref[0].astype(BF16))


def nsa_compress(z, w1, pe, w2, B, T):
    nseg = T // CMP_STRIDE
    return pl.pallas_call(
        functools.partial(_cmp_body, nseg=nseg),
        grid=(2, B, NSA_KVH),
        in_specs=[pl.BlockSpec((T, HD), lambda s, b, k: (b, NSA_QW // HD + NSA_KVH * s + k)),
                  pl.BlockSpec((1, CMP_BLOCK, HD, HD), lambda s, b, k: (s, 0, 0, 0)),
                  pl.BlockSpec((1, CMP_BLOCK, HD), lambda s, b, k: (s, 0, 0)),
                  pl.BlockSpec((1, HD, HD), lambda s, b, k: (s, 0, 0))],
        out_specs=pl.BlockSpec((1, 1, 1, nseg, HD), lambda s, b, k: (s, b, k, 0, 0)),
        out_shape=jax.ShapeDtypeStruct((2, B, NSA_KVH, nseg, HD), F32),
        compiler_params=_cparams(("parallel", "parallel", "parallel")),
        name="nsa_compress",
    )(z, w1, pe, w2)


def _overlap_t(nsp, nseg, n_slc):
    j = lax.broadcasted_iota(jnp.int32, (nsp, 1), 0)
    m = lax.broadcasted_iota(jnp.int32, (1, nseg), 1)
    ov = ((m * CMP_STRIDE <= j * SLC_BLOCK + SLC_BLOCK - 1) & (m * CMP_STRIDE + CMP_BLOCK - 1 >= j * SLC_BLOCK)
          & (m < nseg - CMP_RATIO + 1) & (j < n_slc))
    return ov.astype(F32)


def _rank_rows(score, n):
    j = lax.broadcasted_iota(jnp.int32, (score.shape[0], 1), 0)
    rank = jnp.zeros(score.shape, F32)
    for i in range(n):
        si = score[i:i + 1, :]
        rank = rank + ((si > score) | ((si == score) & (i < j))).astype(F32)
    return rank


def _nsa_attn_body(qraw_ref, qrot_ref, kc_ref, vc_ref, ks_ref, vs_ref, kw_ref, vw_ref, gate_ref, o_ref,
                   ksb_ref, vsb_ref, kwb_ref, vwb_ref, *, T, tq, tk):
    qi = pl.program_id(2)
    t0 = qi * tq
    scale = HD ** -0.5
    nseg = T // CMP_STRIDE
    n_slc = T // SLC_BLOCK
    nsp = max(8, n_slc)
    pos = t0 + lax.broadcasted_iota(jnp.int32, (tq, 1), 0)
    pos_row = t0 + lax.broadcasted_iota(jnp.int32, (1, tq), 1)

    @pl.when(qi == 0)
    def _():
        ksb_ref[:, :HD] = ks_ref[...].astype(BF16)
        ksb_ref[:, HD:] = (lax.broadcasted_iota(jnp.int32, (T, 1), 0) // SLC_BLOCK
                           == lax.broadcasted_iota(jnp.int32, (1, LANE), 1)).astype(BF16)
        for dst, src in zip((vsb_ref, kwb_ref, vwb_ref), (vs_ref, kw_ref, vw_ref)):
            dst[...] = src[...].astype(BF16)

    def heads_on_rows(ref):
        return jnp.concatenate([ref[:, g * HD:(g + 1) * HD] for g in range(NSA_G)], axis=0)

    def per_head(x):
        return jnp.concatenate([x] * NSA_G, axis=0)

    q_raw = (heads_on_rows(qraw_ref) * scale).astype(BF16)
    q_rot = (heads_on_rows(qrot_ref) * scale).astype(BF16)

    kc = kc_ref[0, 0, 0].astype(BF16)
    vc = vc_ref[0, 0, 0].astype(BF16)
    m_idx = lax.broadcasted_iota(jnp.int32, (1, nseg), 1)
    valid = (m_idx * CMP_STRIDE + CMP_BLOCK - 1 <= pos) & (m_idx < nseg - CMP_RATIO + 1)
    s = _dot_nt(q_raw, kc) + per_head(jnp.where(valid, 0.0, NEG))
    e = jnp.exp(s - jnp.max(s, axis=-1, keepdims=True))
    p = e * (1.0 / jnp.sum(e, axis=-1, keepdims=True)) * per_head(valid.astype(F32))
    o_cmp = _dot(p.astype(BF16), vc)
    psum = p[0:tq]
    for g in range(1, NSA_G):
        psum = psum + p[g * tq:(g + 1) * tq]

    imp_t = lax.dot_general(_overlap_t(nsp, nseg, n_slc), psum, (((1,), (1,)), ((), ())),
                            preferred_element_type=F32, precision=lax.Precision.HIGHEST)
    j = lax.broadcasted_iota(jnp.int32, (nsp, 1), 0)
    cur = pos_row // SLC_BLOCK
    forced = (j == 0) | (j == cur) | (j == cur - 1)
    score = jnp.where(j * SLC_BLOCK <= pos_row, imp_t + FORCE_BONUS * forced.astype(F32), -1.0)
    score = jnp.where(j < n_slc, score, -2.0)
    sel_t = ((_rank_rows(score, n_slc) < min(SLC_TOP_N, n_slc)) & (j < n_slc)).astype(F32)
    if nsp < LANE:
        sel_t = jnp.concatenate([sel_t, jnp.zeros((LANE - nsp, tq), F32)], axis=0)
    sel = sel_t.T
    def update(s, v, carry):
        m_old, l_old, acc = carry
        m_new = jnp.maximum(m_old, jnp.max(s, axis=-1, keepdims=True))
        alpha = jnp.exp(m_old - m_new)
        p = jnp.exp(s - m_new)
        return (m_new, alpha * l_old + jnp.sum(p, axis=-1, keepdims=True), alpha * acc + _dot(p.astype(BF16), v))

    R = NSA_G * tq
    init = (jnp.full((R, 1), NEG, F32), jnp.zeros((R, 1), F32), jnp.zeros((R, HD), F32))

    q_aug = jnp.concatenate([q_rot, per_head(((sel - 1.0) * -NEG).astype(BF16))], axis=1)

    def slc_scores(kt):
        k0 = pl.multiple_of(kt * tk, tk)
        return _dot_nt(q_aug, ksb_ref[pl.ds(k0, tk), :]), vsb_ref[pl.ds(k0, tk), :]

    def slc_step(kt, carry):
        s, v = slc_scores(kt)
        return update(s, v, carry)

    kt_diag = t0 // tk
    slc = lax.fori_loop(0, kt_diag, slc_step, init)
    s, v = slc_scores(kt_diag)
    kpos = kt_diag * tk + lax.broadcasted_iota(jnp.int32, (1, tk), 1)
    slc = update(s + per_head(jnp.where(kpos <= pos, 0.0, NEG)), v, slc)

    nkw = min(WINDOW + tq, T)
    k0w = pl.multiple_of(jnp.clip(t0 - WINDOW, 0, T - nkw), tq)
    d = pos - (k0w + lax.broadcasted_iota(jnp.int32, (1, nkw), 1))
    win = update(_dot_nt(q_rot, kwb_ref[pl.ds(k0w, nkw), :]) + per_head(jnp.where((d >= 0) & (d < WINDOW), 0.0, NEG)),
                 vwb_ref[pl.ds(k0w, nkw), :], init)

    gate = _sigmoid(gate_ref[...])
    o_slc = slc[2] / slc[1]
    o_win = win[2] / win[1]
    for g in range(NSA_G):
        rows = slice(g * tq, (g + 1) * tq)
        o_ref[:, g * HD:(g + 1) * HD] = (gate[:, 3 * g:3 * g + 1] * o_cmp[rows]
                                         + gate[:, 3 * g + 1:3 * g + 2] * o_slc[rows]
                                         + gate[:, 3 * g + 2:3 * g + 3] * o_win[rows])


def nsa_attn_prefill(z, rot, cmp, gates, B, T):
    assert T % LANE == 0 and T // SLC_BLOCK <= LANE, "selection blocks must fit one lane group"
    tq = LANE
    tk = _pick_tile(T, (512, 256, 128))
    nq = T // tq
    nseg = T // CMP_STRIDE
    W = NSA_G * HD
    qspec = pl.BlockSpec((tq, W), lambda b, k, q: (b * nq + q, k))
    cspec = lambda s: pl.BlockSpec((1, 1, 1, nseg, HD), lambda b, k, q: (s, b, k, 0, 0))
    kvspec = lambda c0: pl.BlockSpec((T, HD), lambda b, k, q: (b, c0 + k))
    return pl.pallas_call(
        functools.partial(_nsa_attn_body, T=T, tq=tq, tk=tk),
        grid=(B, NSA_KVH, nq),
        in_specs=[qspec, qspec, cspec(0), cspec(1),
                  kvspec(NSA_QW // HD), kvspec((NSA_QW + 3 * W) // HD),
                  kvspec((NSA_QW + W) // HD), kvspec((NSA_QW + 5 * W) // HD),
                  pl.BlockSpec((tq, LANE), lambda b, k, q: (b * nq + q, k))],
        out_specs=qspec,
        out_shape=jax.ShapeDtypeStruct((B * T, NSA_QW), F32),
        scratch_shapes=[pltpu.VMEM((T, 2 * HD), BF16)] + [pltpu.VMEM((T, HD), BF16) for _ in range(3)],
        compiler_params=_cparams(("parallel", "parallel", "arbitrary")),
        name="nsa_attn",
    )(z, rot, cmp, cmp, rot, z, rot, z, gates)


def _nsa_gate_weight(w_in, a):
    K = w_in.shape[1]
    wg = w_in[a, :, NSA_MAIN:].reshape(K, NSA_KVH, 3 * NSA_G)
    return jnp.pad(wg, ((0, 0), (0, 0), (0, LANE - 3 * NSA_G))).reshape(K, NSA_KVH * LANE)


def nsa_layer_prefill(x, g, w_in, w1, pe, w2, w_out, a, tables, B, T):
    z = proj(x, g, jnp.swapaxes(w_in, 1, 2), n_out=NSA_MAIN, layer=a, w_rows_out=True)
    gates = proj(x, g, _nsa_gate_weight(w_in, a))
    rot = nsa_rope(z, tables, T)
    cmp = nsa_compress(z, w1, pe, w2, B, T)
    mix = nsa_attn_prefill(z, rot, cmp, gates, B, T)
    y = proj(mix, g, w_out, residual=x, norm=False, layer=a)
    W = NSA_KVH * HD
    rows = jnp.concatenate([z[:, NSA_QW:NSA_QW + 2 * W], rot[:, NSA_QW:NSA_QW + W],
                            z[:, NSA_QW + 3 * W:NSA_QW + 4 * W]], axis=1).reshape(B, T, 4, NSA_KVH, HD)
    Tw = min(WINDOW, T)
    win = jnp.concatenate([rot.reshape(B, T, -1)[:, T - Tw:, NSA_QW + W:], z.reshape(B, T, -1)[:, T - Tw:, NSA_QW + 5 * W:]],
                          axis=2).reshape(B, Tw, 2, NSA_KVH, HD)
    return y, rows, win


def _round_robin(gens):
    while gens:
        alive = []
        for gen in gens:
            try:
                next(gen)
                alive.append(gen)
            except StopIteration:
                pass
        gens = alive


def _tri(c):
    return (lax.broadcasted_iota(jnp.int32, (c, 1), 0) >= lax.broadcasted_iota(jnp.int32, (1, c), 1)).astype(F32)


def _hgrn_body(q_ref, f_ref, i_ref, gz_ref, lb_ref, gn_ref, s0_ref, o_ref, s_ref, st_ref,
               *, C, SC, HP, t_valid, nt, layer):
    t = pl.program_id(2)
    Tt = q_ref.shape[0]

    @pl.when(t == 0)
    def _():
        for hp in range(HP):
            st_ref[hp] = s0_ref[0, hp].T

    lbp = lb_ref[...]
    e = jnp.exp(lbp - jnp.max(lbp, axis=0, keepdims=True))
    lb_all = jnp.sum(e[1:layer + 1], axis=0, keepdims=True) / jnp.sum(e, axis=0, keepdims=True)
    gn_all = gn_ref[...]
    tri = _tri(C)
    row8 = lax.broadcasted_iota(jnp.int32, (8, 1), 0)

    def head_chunk(hp, r0):
        cols = slice(hp * HD, (hp + 1) * HD)
        lb, gn = lb_all[:, cols], gn_all[:, cols]
        qz = q_ref[pl.ds(r0, C), cols]
        f = lb + (1.0 - lb) * _sigmoid(f_ref[pl.ds(r0, C), cols])
        v = i_ref[pl.ds(r0, C), cols]
        gz = gz_ref[pl.ds(r0, C), cols]
        k = 1.0 - f
        lf = jnp.log(f)
        if t_valid is not None:
            ok = (t * Tt + r0 + lax.broadcasted_iota(jnp.int32, (C, 1), 0)) < t_valid
            k = jnp.where(ok, k, 0.0)
            lf = jnp.where(ok, lf, 0.0)
        q = qz * _sigmoid(qz)
        b = _dot_hi(tri, lf)
        yield
        st = st_ref[hp]
        vb = v.astype(BF16)
        out = _dot_nt((q * jnp.exp(b)).astype(BF16), st.astype(BF16))
        bl = b[C - 1:C]
        st_ref[hp] = st * jnp.exp(bl) + _dot_tn(vb, (k * jnp.exp(bl - b)).astype(BF16))
        atts = []
        for I in range(1, C // SC):
            r = I * SC
            beta = b[r - 1:r]
            qt = q[r:r + SC] * jnp.exp(b[r:r + SC] - beta)
            kt = k[0:r] * jnp.exp(beta - b[0:r])
            atts.append(_dot_nt(qt.astype(BF16), kt.astype(BF16)))
        yield
        offdiag = [_dot(att.astype(BF16), vb[0:I * SC]) for I, att in enumerate(atts, 1)]
        pieces = []
        for I in range(C // SC):
            r = I * SC
            bI, qI, kI, vI = b[r:r + SC], q[r:r + SC], k[r:r + SC], v[r:r + SC]
            groups = [jnp.zeros((8, HD), F32) for _ in range(SC // 8)]
            for s in range(SC):
                for gi in range(s // 8, SC // 8):
                    rows = slice(gi * 8, gi * 8 + 8)
                    d = bI[rows] - bI[s:s + 1]
                    if gi == s // 8:
                        d = jnp.where(row8 >= s % 8, d, NEG)
                    att = jnp.sum(qI[rows] * jnp.exp(d) * kI[s:s + 1], axis=-1, keepdims=True)
                    groups[gi] = groups[gi] + att * vI[s:s + 1]
            pieces.extend(groups)
        yield
        intra = jnp.concatenate(pieces, axis=0)
        if offdiag:
            intra = intra + jnp.concatenate([jnp.zeros((SC, HD), F32)] + offdiag, axis=0)
        out = out + intra
        o_ref[pl.ds(r0, C), cols] = _rms(out, gn) * (gz * _sigmoid(gz))

    def chunk(c, carry):
        r0 = pl.multiple_of(c * C, C)
        _round_robin([head_chunk(hp, r0) for hp in range(HP)])
        return carry

    lax.fori_loop(0, Tt // C, chunk, 0)

    @pl.when(t == nt - 1)
    def _():
        for hp in range(HP):
            s_ref[0, hp] = st_ref[hp].T


def hgrn_scan(z, lb, layer, gn, s0, B, T, t_valid=None):
    depth = lb.shape[0]
    C = min(CHUNK, T)
    SC = min(SUB, C)
    Tt = _pick_tile(T, (512, 256, 128, 64, 32, 16))
    nt = T // Tt
    HP = HG_HEADS_PER_STEP
    ng = HG_HEADS // HP
    W = HP * HD
    zspec = lambda part: pl.BlockSpec((Tt, W), lambda b, h, t: (b * nt + t, part * ng + h))
    vspec = pl.BlockSpec((1, W), lambda b, h, t: (0, h))
    lspec = pl.BlockSpec((depth, W), lambda b, h, t: (0, h))
    sspec = pl.BlockSpec((1, HP, HD, HD), lambda b, h, t: (b, h, 0, 0))
    return pl.pallas_call(
        functools.partial(_hgrn_body, C=C, SC=SC, HP=HP, t_valid=t_valid, nt=nt, layer=layer),
        grid=(B, ng, nt),
        in_specs=[zspec(0), zspec(1), zspec(2), zspec(3), lspec, vspec, sspec],
        out_specs=[pl.BlockSpec((Tt, W), lambda b, h, t: (b * nt + t, h)), sspec],
        out_shape=[jax.ShapeDtypeStruct((B * T, HG_W), F32), jax.ShapeDtypeStruct((B, HG_HEADS, HD, HD), F32)],
        scratch_shapes=[pltpu.VMEM((HP, HD, HD), F32)],
        compiler_params=_cparams(("parallel", "parallel", "arbitrary")),
        name="hgrn_scan",
    )(z, z, z, z, lb, gn.reshape(1, HG_W), s0)


def _mlstm_body(q_ref, k_ref, v_ref, og_ref, gt_ref, bg_ref, c0_ref, n0_ref, m0_ref,
                y_ref, c_ref, n_ref, m_ref, cs_ref, ns_ref, ms_ref, *, C, HP, t_valid, nt):
    t = pl.program_id(2)
    Tt = q_ref.shape[0]

    @pl.when(t == 0)
    def _():
        cs_ref[...] = c0_ref[0]
        ns_ref[...] = n0_ref[0]
        ms_ref[...] = m0_ref[0]

    tri_b = lax.broadcasted_iota(jnp.int32, (C, 1), 0) >= lax.broadcasted_iota(jnp.int32, (1, C), 1)
    upper = (lax.broadcasted_iota(jnp.int32, (C, 1), 0) <= lax.broadcasted_iota(jnp.int32, (1, C), 1)).astype(F32)
    eye = (lax.broadcasted_iota(jnp.int32, (C, 1), 0) == lax.broadcasted_iota(jnp.int32, (1, C), 1)).astype(F32)
    tri = tri_b.astype(F32)
    ones = jnp.ones((C, C), F32)
    lane = lax.broadcasted_iota(jnp.int32, (1, LANE), 1)
    bg = bg_ref[...]

    def head_chunk(hp, r0, g):
        h = pl.program_id(1) * HP + hp
        kcols = slice(hp * ML_DK, (hp + 1) * ML_DK)
        vcols = slice(hp * ML_DV, (hp + 1) * ML_DV)
        ig = jnp.sum(jnp.where(lane == h, g, 0.0), axis=-1, keepdims=True)
        fg = jnp.sum(jnp.where(lane == ML_HEADS + h, g, 0.0), axis=-1, keepdims=True)
        lf = jnp.minimum(fg, 0.0) - jnp.log(1.0 + jnp.exp(-jnp.abs(fg)))
        if t_valid is not None:
            ok = (t * Tt + r0 + lax.broadcasted_iota(jnp.int32, (C, 1), 0)) < t_valid
            lf = jnp.where(ok, lf, 0.0)
            ig = jnp.where(ok, ig, NEG)
        lf_b = jnp.broadcast_to(lf, (C, C))
        ig_b = jnp.broadcast_to(ig, (C, C))
        b_col = _dot_hi(tri, lf_b)
        b_row_i = _dot_hi(ones, lf_b * upper - ig_b * eye)
        q = q_ref[pl.ds(r0, C), kcols]
        ks = k_ref[pl.ds(r0, C), kcols] * (ML_DK ** -0.5)
        v = v_ref[pl.ds(r0, C), vcols]
        qb, kb, vb = q.astype(BF16), ks.astype(BF16), v.astype(BF16)
        cs = cs_ref[hp]
        ns = ns_ref[hp]
        qk = _dot_nt(qb, kb)
        inter = _dot(qb, cs.astype(BF16))
        yield
        logw = jnp.where(tri_b, b_col - b_row_i, NEG)
        m_old = ms_ref[hp][:, 0:1]
        b1 = b_col[:, 0:1]
        m_prev = b1 + m_old
        m_t = jnp.maximum(m_prev, jnp.max(logw, axis=-1, keepdims=True))
        w0 = jnp.exp(m_prev - m_t)
        sqk = qk * jnp.exp(logw - m_t)
        intra = _dot(sqk.astype(BF16), vb)
        m_new = m_t[C - 1:C]
        b_last = b1[C - 1:C]
        wk = jnp.exp(b_last - b1 + ig - m_new)
        decay = jnp.exp(b_last + m_old - m_new)
        kw = ks * wk
        cs_ref[hp] = decay * cs + _dot_tn(kw.astype(BF16), vb)
        ns_ref[hp] = decay * ns + jnp.sum(kw, axis=0, keepdims=True)
        ms_ref[hp] = jnp.broadcast_to(m_new, (1, LANE))
        den = w0 * jnp.sum(q * ns, axis=-1, keepdims=True) + jnp.sum(sqk, axis=-1, keepdims=True)
        yield
        hc = (w0 * inter + intra) / jnp.maximum(jnp.abs(den), jnp.exp(-m_t))
        y_ref[pl.ds(r0, C), vcols] = _sigmoid(og_ref[pl.ds(r0, C), vcols]) * hc

    def chunk(c, carry):
        r0 = pl.multiple_of(c * C, C)
        g = gt_ref[pl.ds(r0, C), :] + bg
        g = ML_GATE_CAP * jnp.tanh(g / ML_GATE_CAP)
        _round_robin([head_chunk(hp, r0, g) for hp in range(HP)])
        return carry

    lax.fori_loop(0, Tt // C, chunk, 0)

    @pl.when(t == nt - 1)
    def _():
        c_ref[0] = cs_ref[...]
        n_ref[0] = ns_ref[...]
        m_ref[0] = ms_ref[...]


def mlstm_scan(z, gates, b_gate, c0, n0, m0, B, T, t_valid=None):
    C = min(CHUNK, T)
    Tt = _pick_tile(T, (512, 256, 128, 64, 32, 16))
    nt = T // Tt
    H = ML_HEADS
    row = lambda b, t: b * nt + t
    bg = jnp.pad(b_gate, (0, LANE - 2 * H)).reshape(1, LANE)
    n0 = n0.reshape(B, H, 1, ML_DK)
    m0 = jnp.broadcast_to(m0[:, :, None, None], (B, H, 1, LANE))
    HP = ML_HEADS_PER_STEP
    ng = H // HP
    kw, vw = HP * ML_DK, HP * ML_DV
    vec = lambda w: pl.BlockSpec((1, HP, 1, w), lambda b, h, t: (b, h, 0, 0))
    cspec = pl.BlockSpec((1, HP, ML_DK, ML_DV), lambda b, h, t: (b, h, 0, 0))
    y, c, n, m = pl.pallas_call(
        functools.partial(_mlstm_body, C=C, HP=HP, t_valid=t_valid, nt=nt),
        grid=(B, ng, nt),
        in_specs=[pl.BlockSpec((Tt, kw), lambda b, h, t: (row(b, t), h)),
                  pl.BlockSpec((Tt, kw), lambda b, h, t: (row(b, t), ng + h)),
                  pl.BlockSpec((Tt, vw), lambda b, h, t: (row(b, t), 2 * ML_QK // vw + h)),
                  pl.BlockSpec((Tt, vw), lambda b, h, t: (row(b, t), (2 * ML_QK + ML_VW) // vw + h)),
                  pl.BlockSpec((Tt, LANE), lambda b, h, t: (row(b, t), 0)),
                  pl.BlockSpec((1, LANE), lambda b, h, t: (0, 0)),
                  cspec, vec(ML_DK), vec(LANE)],
        out_specs=[pl.BlockSpec((Tt, vw), lambda b, h, t: (row(b, t), h)), cspec, vec(ML_DK), vec(LANE)],
        out_shape=[jax.ShapeDtypeStruct((B * T, ML_VW), F32), jax.ShapeDtypeStruct((B, H, ML_DK, ML_DV), F32),
                   jax.ShapeDtypeStruct((B, H, 1, ML_DK), F32), jax.ShapeDtypeStruct((B, H, 1, LANE), F32)],
        scratch_shapes=[pltpu.VMEM((HP, ML_DK, ML_DV), F32), pltpu.VMEM((HP, 1, ML_DK), F32),
                        pltpu.VMEM((HP, 1, LANE), F32)],
        compiler_params=_cparams(("parallel", "parallel", "arbitrary")),
        name="mlstm_scan",
    )(z, z, z, z, gates, bg, c0, n0, m0)
    return y, c, n.reshape(B, H, ML_DK), m[:, :, 0, 0]


PAGES_PER_STEP = 16
SEG_PER_PAGE = PAGE_SIZE // CMP_STRIDE
TILE_ROWS = 2 * NSA_KVH


def _cmp_paged_body(pt_ref, *refs):
    P = PAGES_PER_STEP
    x_refs, w_ref, o_ref, lhs_ref = refs[:P], refs[P], refs[P + 1], refs[P + 2]
    rows = SEG_PER_PAGE * TILE_ROWS
    for p in range(P):
        for j in range(CMP_STRIDE):
            xj = x_refs[p][pl.ds(j, SEG_PER_PAGE, stride=CMP_STRIDE)]
            lhs_ref[p * rows:(p + 1) * rows, j * HD:(j + 1) * HD] = xj.reshape(rows, HD).astype(BF16)
    res = _dot(lhs_ref[...], w_ref[...])
    c = lax.broadcasted_iota(jnp.int32, (P * rows, 1), 0) % TILE_ROWS
    o_ref[0] = jnp.where(c < NSA_KVH, res[:, :2 * HD], res[:, 2 * HD:])


def nsa_compress_paged(cache4, page_table, w1, a):
    B, n_pages = page_table.shape
    P = PAGES_PER_STEP
    rows = SEG_PER_PAGE * TILE_ROWS
    w_cat = w1.reshape(2, CMP_RATIO, CMP_STRIDE, HD, HD).transpose(2, 3, 0, 1, 4).reshape(CMP_STRIDE * HD, 4 * HD)
    xspec = lambda i: pl.BlockSpec((PAGE_SIZE, None, TILE_ROWS, HD),
                                   lambda b, s, pt: (pt[b, s * P + i], 2 * a, 0, 0))
    grid_spec = pltpu.PrefetchScalarGridSpec(
        num_scalar_prefetch=1,
        grid=(B, n_pages // P),
        in_specs=[xspec(i) for i in range(P)] + [pl.BlockSpec((CMP_STRIDE * HD, 4 * HD), lambda b, s, pt: (0, 0))],
        out_specs=pl.BlockSpec((1, P * rows, 2 * HD), lambda b, s, pt: (b, s, 0)),
        scratch_shapes=[pltpu.VMEM((P * rows, CMP_STRIDE * HD), BF16)])
    return pl.pallas_call(
        _cmp_paged_body,
        grid_spec=grid_spec,
        out_shape=jax.ShapeDtypeStruct((B, n_pages * rows, 2 * HD), F32),
        compiler_params=_cparams(("parallel", "arbitrary")),
        name="nsa_compress_paged",
    )(page_table, *([cache4] * P), w_cat.astype(BF16))


def _nsa_dec_cmp_body(part_ref, w1_ref, pe_ref, w2_ref, q_ref, o_ref, idx_ref, *, n_seg, n_slc, pos):
    R = n_seg * TILE_ROWS
    nsp = -(-n_slc // LANE) * LANE
    scale = HD ** -0.5
    p = part_ref[0]
    c_row = lax.broadcasted_iota(jnp.int32, (R, 1), 0) % TILE_ROWS
    is_k = c_row < NSA_KVH
    pe_hid = [_pe_hidden(pe_ref.at[pl.ds(s, 1)], w1_ref.at[pl.ds(s, 1)]) for s in range(2)]
    hid = p[:, :HD] + pltpu.roll(p[:, HD:], R - TILE_ROWS, 0) + jnp.where(is_k, pe_hid[0], pe_hid[1])
    ge = _gelu(hid).astype(BF16)
    kv = jnp.where(is_k, _dot(ge, w2_ref[0].astype(BF16)), _dot(ge, w2_ref[1].astype(BF16))).astype(BF16)
    q = q_ref[0].astype(BF16)
    col = lax.broadcasted_iota(jnp.int32, (1, R), 1)
    col_c, col_m = col % TILE_ROWS, col // TILE_ROWS
    head_kv = lax.broadcasted_iota(jnp.int32, (NSA_HEADS, 1), 0) // NSA_G
    valid = ((col_c == head_kv) & (col_m < n_seg - CMP_RATIO + 1)
             & (col_m * CMP_STRIDE + CMP_BLOCK - 1 <= pos))
    s = jnp.where(valid, _dot_nt(q, kv) * scale, NEG)
    e = jnp.exp(s - jnp.max(s, axis=-1, keepdims=True))
    prob = jnp.where(valid, e / jnp.sum(e, axis=-1, keepdims=True), 0.0)
    o_ref[0] = _dot(pltpu.roll(prob, NSA_KVH, 1).astype(BF16), kv)
    group = (lax.broadcasted_iota(jnp.int32, (8, 1), 0)
             == lax.broadcasted_iota(jnp.int32, (1, NSA_HEADS), 1) // NSA_G).astype(F32)
    psum = _dot_hi(group, prob)
    m_r = lax.broadcasted_iota(jnp.int32, (R, 1), 0) // TILE_ROWS
    j = lax.broadcasted_iota(jnp.int32, (1, nsp), 1)
    ov = ((m_r * CMP_STRIDE <= j * SLC_BLOCK + SLC_BLOCK - 1) & (m_r * CMP_STRIDE + CMP_BLOCK - 1 >= j * SLC_BLOCK)
          & (m_r < n_seg - CMP_RATIO + 1) & (j < n_slc)).astype(F32)
    imp = _dot_hi(psum, ov)
    cur = pos // SLC_BLOCK
    forced = (j == 0) | (j == cur) | (j == cur - 1)
    score = jnp.where(j * SLC_BLOCK <= pos, imp + FORCE_BONUS * forced.astype(F32), -1.0)
    score = jnp.where(j < n_slc, score, -2.0)
    jf = j.astype(F32)
    slot = lax.broadcasted_iota(jnp.int32, (1, LANE), 1)
    picked = jnp.zeros((8, LANE), F32)
    for it in range(min(SLC_TOP_N, n_slc)):
        best = jnp.max(score, axis=-1, keepdims=True)
        ix = jnp.min(jnp.where(score == best, jf, float(nsp)), axis=-1, keepdims=True)
        picked = jnp.where(slot == it, ix, picked)
        score = jnp.where(jf == ix, -3.0, score)
    idx_ref[0] = picked.astype(jnp.int32)


def nsa_dec_cmp(part, w1, pe, w2, q_raw, n_seg, n_slc, pos):
    B = part.shape[0]
    R = n_seg * TILE_ROWS
    full = lambda shape: pl.BlockSpec(shape, lambda b: (0,) * len(shape))
    return pl.pallas_call(
        functools.partial(_nsa_dec_cmp_body, n_seg=n_seg, n_slc=n_slc, pos=pos),
        grid=(B,),
        in_specs=[pl.BlockSpec((1, R, 2 * HD), lambda b: (b, 0, 0)),
                  full((2, CMP_BLOCK, HD, HD)), full((2, CMP_BLOCK, HD)), full((2, HD, HD)),
                  pl.BlockSpec((1, NSA_HEADS, HD), lambda b: (b, 0, 0))],
        out_specs=[pl.BlockSpec((1, NSA_HEADS, HD), lambda b: (b, 0, 0)),
                   pl.BlockSpec((1, 8, LANE), lambda b: (b, 0, 0))],
        out_shape=[jax.ShapeDtypeStruct((B, NSA_HEADS, HD), F32), jax.ShapeDtypeStruct((B, 8, LANE), jnp.int32)],
        compiler_params=_cparams(("parallel",)),
        name="nsa_dec_cmp",
    )(part, w1, pe, w2, q_raw)


def _nsa_dec_attn_body(pt_ref, idx_ref, *refs, n_cache_blocks, nsel, win_skip):
    blk_refs = refs[:nsel]
    q_ref, new_ref, win_ref, ocmp_ref, gate_ref, o_ref = refs[nsel:]
    b = pl.program_id(0)
    kh = pl.program_id(1)
    scale = HD ** -0.5
    q = q_ref[0, 0].astype(BF16)
    new = new_ref[0, 0]
    s_new = _dot_nt(q, new.astype(BF16)) * scale

    def attend(x, msk, s_own, v_own):
        s = [jnp.where(m, _dot_nt(q, xi) * scale, NEG) for xi, m in zip(x, msk)]
        mx = s_own
        for si in s:
            mx = jnp.maximum(mx, jnp.max(si, axis=-1, keepdims=True))
        p_own = jnp.exp(s_own - mx)
        num = p_own * v_own
        den = p_own
        for si, xi in zip(s, x):
            p = jnp.exp(si - mx)
            num = num + _dot(pltpu.roll(p, NSA_KVH, 1).astype(BF16), xi)
            den = den + jnp.sum(p, axis=-1, keepdims=True)
        return num / den

    col_c = lax.broadcasted_iota(jnp.int32, (1, SLC_BLOCK * TILE_ROWS), 1) % TILE_ROWS
    xs = [r[...].reshape(SLC_BLOCK * TILE_ROWS, HD).astype(BF16) for r in blk_refs]
    ms = [(col_c == kh) & (idx_ref[b, kh, n] < n_cache_blocks) for n in range(nsel)]
    o_slc = attend(xs, ms, s_new[:, 0:1], new[1:2])

    n_win = win_ref.shape[1]
    xw = win_ref[0].reshape(n_win * TILE_ROWS, HD).astype(BF16)
    colw = lax.broadcasted_iota(jnp.int32, (1, n_win * TILE_ROWS), 1)
    mw = ((colw % TILE_ROWS) == kh) & ((colw // TILE_ROWS) >= win_skip)
    o_win = attend([xw], [mw], s_new[:, 2:3], new[3:4])

    gate = _sigmoid(gate_ref[0, 0])
    lane = lax.broadcasted_iota(jnp.int32, (1, LANE), 1)
    row = lax.broadcasted_iota(jnp.int32, (8, 1), 0)
    pick = lambda c: jnp.sum(jnp.where(lane == 3 * row + c, gate, 0.0), axis=-1, keepdims=True)
    o_ref[0, 0] = pick(0) * ocmp_ref[0, 0] + pick(1) * o_slc + pick(2) * o_win


def nsa_dec_attn(cache5, win4, page_table, idx, q_rot, new_rows, o_cmp, gates, a, n_cache_blocks, win_skip):
    B, _, nsel = idx.shape
    n_win = win4.shape[1]

    def blk_spec(n):
        def blk_map(b, k, pt, ix):
            j = jnp.minimum(ix[b, k, n], n_cache_blocks - 1)
            return (pt[b, j // 2] * 2 + j % 2, 0, 2 * a + 1, 0, 0)
        return pl.BlockSpec((None, SLC_BLOCK, None, TILE_ROWS, HD), blk_map)

    grp = pl.BlockSpec((1, 1, 8, HD), lambda b, k, pt, ix: (b, k, 0, 0))
    grid_spec = pltpu.PrefetchScalarGridSpec(
        num_scalar_prefetch=2,
        grid=(B, NSA_KVH),
        in_specs=[blk_spec(n) for n in range(nsel)] + [
            grp, grp, pl.BlockSpec((1, n_win, TILE_ROWS, HD), lambda b, k, pt, ix: (a * B + b, 0, 0, 0)), grp, grp],
        out_specs=grp)
    return pl.pallas_call(
        functools.partial(_nsa_dec_attn_body, n_cache_blocks=n_cache_blocks, nsel=nsel, win_skip=win_skip),
        grid_spec=grid_spec,
        out_shape=jax.ShapeDtypeStruct((B, NSA_KVH, 8, HD), F32),
        compiler_params=_cparams(("parallel", "arbitrary")),
        name="nsa_dec_attn",
    )(page_table, idx, *([cache5] * nsel), q_rot, new_rows, win4, o_cmp, gates)


def _group_rows(x, B):
    x = x.reshape(B, NSA_KVH, NSA_G, HD)
    return jnp.pad(x, ((0, 0), (0, 0), (0, 8 - NSA_G), (0, 0)))


def nsa_layer_decode(x, g, w_in, w1, pe, w2, w_out, cache, win_state, page_table, a, B):
    R = x.shape[0]
    n_pool = cache.shape[0]
    n_pages = page_table.shape[1]
    past = n_pages * PAGE_SIZE
    n_seg = (past + 1) // CMP_STRIDE
    n_slc = -(-(past + 1) // SLC_BLOCK)
    W = NSA_KVH * HD
    z = proj(x, g, jnp.swapaxes(w_in, 1, 2), n_out=NSA_MAIN, layer=a, w_rows_out=True)
    gates = proj(x, g, _nsa_gate_weight(w_in, a))
    rot = nsa_rope(z, rope_tables(jnp.full((R,), past, jnp.int32)), R)
    part = nsa_compress_paged(cache.reshape(n_pool * PAGE_SIZE, 4, TILE_ROWS, HD), page_table, w1, a)
    o_cmp, idx = nsa_dec_cmp(part, w1, pe, w2, z[:B, :NSA_QW].reshape(B, NSA_HEADS, HD), n_seg, n_slc, past)
    nsel = min(SLC_TOP_N, n_slc)
    kv = z[:B, NSA_QW:].reshape(B, 6, NSA_KVH, HD)
    ks_new = rot[:B, NSA_QW:NSA_QW + W].reshape(B, NSA_KVH, HD)
    kw_new = rot[:B, NSA_QW + W:].reshape(B, NSA_KVH, HD)
    new_rows = jnp.stack([ks_new, kv[:, 3], kw_new, kv[:, 5]], axis=2)
    new_rows = jnp.pad(new_rows, ((0, 0), (0, 0), (0, 4), (0, 0)))
    n_win = win_state.shape[2]
    mix = nsa_dec_attn(cache.reshape(n_pool * 2, SLC_BLOCK, 4, TILE_ROWS, HD),
                       win_state.reshape(-1, n_win, TILE_ROWS, HD), page_table, idx[:, :NSA_KVH, :nsel],
                       _group_rows(rot[:B, :NSA_QW], B), new_rows, _group_rows(o_cmp.reshape(B, NSA_QW), B),
                       jnp.broadcast_to(gates[:B].reshape(B, NSA_KVH, 1, LANE), (B, NSA_KVH, 8, LANE)),
                       a, past // SLC_BLOCK, max(n_win + 1 - WINDOW, 0))
    mix = jnp.pad(mix[:, :, :NSA_G].reshape(B, NSA_QW), ((0, R - B), (0, 0)))
    y = proj(mix, g, w_out, residual=x, norm=False, layer=a)
    rows = jnp.stack([kv[:, 0], kv[:, 1], ks_new, kv[:, 3]], axis=1)[:, None]
    win_new = jnp.stack([kw_new, kv[:, 5]], axis=1)[:, None]
    buf = jnp.concatenate([win_state[a], win_new], axis=1)
    return y, rows, buf[:, -min(WINDOW, n_win + 1):]


def kernel(x_prompt, x_sample, cache_nsa_kv, state_nsa_win, state_hgrn, state_mlstm_c, state_mlstm_n, state_mlstm_m, state_ffn_conv, page_table, norm_mix, norm_ffn, norm_out, nsa_w_in, nsa_cmp_w1, nsa_cmp_pe, nsa_cmp_w2, nsa_w_out, hgrn_w_in, hgrn_lb, hgrn_norm, hgrn_w_out, ml_w_in, ml_b_gate, ml_w_out, ffn_w_up, ffn_conv_w, ffn_conv_b, ffn_w_down):
    P = dict(norm_mix=norm_mix, norm_ffn=norm_ffn, norm_out=norm_out, nsa_w_in=nsa_w_in, nsa_cmp_w1=nsa_cmp_w1,
             nsa_cmp_pe=nsa_cmp_pe, nsa_cmp_w2=nsa_cmp_w2, nsa_w_out=nsa_w_out, hgrn_w_in=hgrn_w_in,
             hgrn_lb=hgrn_lb, hgrn_norm=hgrn_norm, hgrn_w_out=hgrn_w_out, ml_w_in=ml_w_in, ml_b_gate=ml_b_gate,
             ml_w_out=ml_w_out, ffn_w_up=ffn_w_up, ffn_conv_w=ffn_conv_w, ffn_conv_b=ffn_conv_b,
             ffn_w_down=ffn_w_down)
    past = dict(cache=cache_nsa_kv, win=state_nsa_win, hgrn=state_hgrn, c=state_mlstm_c, n=state_mlstm_n,
                m=state_mlstm_m, ffn=state_ffn_conv, page_table=page_table)
    yp, sp = _trunk(x_prompt, P, None)
    ys, ss = _trunk(x_sample, P, past)
    return (yp, ys, sp['nsa_kv'], ss['nsa_kv'], sp['nsa_win'], ss['nsa_win'], sp['hgrn'], ss['hgrn'],
            sp['c'], ss['c'], sp['n'], ss['n'], sp['m'], ss['m'], sp['ffn'], ss['ffn'])


DEC_ROWS = 16


def _spread(z, B, T):
    return jnp.zeros((B, T, z.shape[1]), F32).at[:, 0].set(z[:B]).reshape(B * T, z.shape[1])


def _gather_first(o, B, T, R):
    return jnp.pad(o.reshape(B, T, -1)[:, 0], ((0, R - B), (0, 0)))


def _trunk(x_in, P, past):
    B, T, D = x_in.shape
    decode = past is not None
    if decode:
        R = DEC_ROWS
        x = jnp.pad(x_in.reshape(B, D), ((0, R - B), (0, 0)))
        Ts = DEC_ROWS
    else:
        x = x_in.reshape(B * T, D)
        tables = rope_tables(jnp.arange(T, dtype=jnp.int32))
    rows_l, win_l, hg_l, c_l, n_l, m_l, ffn_l = [], [], [], [], [], [], []
    for i in range(DEPTH):
        kind, a = i % N_MIXERS, i // N_MIXERS
        g = P['norm_mix'][i]
        if kind == 0:
            args = (x, g, P['nsa_w_in'], P['nsa_cmp_w1'][a], P['nsa_cmp_pe'][a], P['nsa_cmp_w2'][a], P['nsa_w_out'])
            if decode:
                x, rows, win = nsa_layer_decode(*args, past['cache'], past['win'], past['page_table'], a, B)
            else:
                x, rows, win = nsa_layer_prefill(*args, a, tables, B, T)
            rows_l.append(rows)
            win_l.append(win)
        elif kind == 1:
            z = proj(x, g, P['hgrn_w_in'], layer=a)
            if decode:
                o, S = hgrn_scan(_spread(z, B, Ts), P['hgrn_lb'], i, P['hgrn_norm'][a], past['hgrn'][a], B, Ts, 1)
                o = _gather_first(o, B, Ts, R)
            else:
                o, S = hgrn_scan(z, P['hgrn_lb'], i, P['hgrn_norm'][a], jnp.zeros((B, HG_HEADS, HD, HD), F32), B, T)
            x = proj(o, g, P['hgrn_w_out'], residual=x, norm=False, layer=a)
            hg_l.append(S)
        else:
            w = P['ml_w_in']
            z = proj(x, g, jnp.swapaxes(w, 1, 2), n_out=ML_MAIN, layer=a, w_rows_out=True)
            gt = proj(x, g, jnp.pad(w[a, :, ML_MAIN:], ((0, 0), (0, LANE - 2 * ML_HEADS))))
            if decode:
                o, C, nv, m = mlstm_scan(_spread(z, B, Ts), _spread(gt, B, Ts), P['ml_b_gate'][a],
                                         past['c'][a], past['n'][a], past['m'][a], B, Ts, 1)
                o = _gather_first(o, B, Ts, R)
            else:
                o, C, nv, m = mlstm_scan(z, gt, P['ml_b_gate'][a], jnp.zeros((B, ML_HEADS, ML_DK, ML_DV), F32),
                                         jnp.zeros((B, ML_HEADS, ML_DK), F32), jnp.zeros((B, ML_HEADS), F32), B, T)
            x = proj(o, g, P['ml_w_out'], residual=x, norm=False, layer=a)
            c_l.append(C)
            n_l.append(nv)
            m_l.append(m)
        fargs = (P['norm_ffn'][i], P['ffn_w_up'], P['ffn_conv_w'], P['ffn_conv_b'], P['ffn_w_down'], i,
                 P['norm_out'])
        last = i == DEPTH - 1
        if decode:
            buf = past['ffn'][i]
            bufp = jnp.pad(buf.transpose(1, 0, 2), ((0, 0), (0, R - B), (0, 0)))
            y, sa = conv_ffn(x[None], *fargs, buf=bufp, final_norm=last)
            x = y[0]
            ffn_l.append(jnp.stack([buf[:, 1], sa[0, :B]], axis=1))
        else:
            y, sa = conv_ffn(x.reshape(B, T, D), *fargs, final_norm=last)
            x = y.reshape(B * T, D)
            ffn_l.append(sa[:, 6:8])
    out = x[:B].reshape(B, 1, D) if decode else x.reshape(B, T, D)
    states = dict(nsa_kv=jnp.stack(rows_l, axis=2), nsa_win=jnp.stack(win_l, axis=0), hgrn=jnp.stack(hg_l, axis=0),
                  c=jnp.stack(c_l, axis=0), n=jnp.stack(n_l, axis=0), m=jnp.stack(m_l, axis=0),
                  ffn=jnp.stack(ffn_l, axis=0))
    return out, states
```

```python
import functools
import math

import jax
import jax.numpy as jnp
from jax import lax
from jax.experimental import pallas as pl
from jax.experimental.pallas import tpu as pltpu

F32 = jnp.float32
BF16 = jnp.bfloat16

D_MODEL = 2048
DEPTH = 4
PAGE_SIZE = 128
N_MIXERS = 3

HD = 128
NSA_HEADS = D_MODEL // HD
NSA_KVH = 4
NSA_G = NSA_HEADS // NSA_KVH
NSA_QW = NSA_HEADS * HD
NSA_KVW = 6 * NSA_KVH * HD
NSA_MAIN = NSA_QW + NSA_KVW
CMP_STRIDE = 16
CMP_RATIO = 2
CMP_BLOCK = CMP_STRIDE * CMP_RATIO
SLC_BLOCK = 64
SLC_TOP_N = 16
FORCE_BONUS = 1000.0
WINDOW = 512
ROPE_THETA = 500000.0
ROPE_DIM = HD // 4
ROPE_HALF = ROPE_DIM // 2

HG_HEADS = D_MODEL // HD
HG_W = HG_HEADS * HD

ML_HEADS = 8
ML_DK = D_MODEL // (2 * ML_HEADS)
ML_DV = D_MODEL // ML_HEADS
ML_QK = ML_HEADS * ML_DK
ML_VW = ML_HEADS * ML_DV
ML_MAIN = 2 * ML_QK + 2 * ML_VW
ML_GATE_CAP = 15.0
CHUNK = 64
SUB = 16
HG_HEADS_PER_STEP = 4
ML_HEADS_PER_STEP = 8

D_FF = (11 * D_MODEL) // 4
EPS = 1e-6
NEG = -1e30

LANE = 128
VMEM_LIMIT = 56 * 1024 * 1024
FFN_VMEM_LIMIT = 62 * 1024 * 1024


def _cparams(sem, vmem_limit=VMEM_LIMIT):
    return pltpu.CompilerParams(dimension_semantics=sem, vmem_limit_bytes=vmem_limit)


def _dot(a, b):
    return jnp.dot(a, b, preferred_element_type=F32)


def _dot_nt(a, b):
    return lax.dot_general(a, b, (((1,), (1,)), ((), ())), preferred_element_type=F32)


def _dot_tn(a, b):
    return lax.dot_general(a, b, (((0,), (0,)), ((), ())), preferred_element_type=F32)


def _dot_hi(a, b):
    return jnp.dot(a, b, preferred_element_type=F32, precision=lax.Precision.HIGHEST)


def _rms(x, g):
    return x * lax.rsqrt(jnp.mean(x * x, axis=-1, keepdims=True) + EPS) * g


def _sigmoid(x):
    return 1.0 / (1.0 + jnp.exp(-x))


def _pick_tile(n, cands):
    for c in cands:
        if n % c == 0:
            return c
    return n


def _proj_body(*refs, norm, residual, w_rows_out):
    if residual:
        x_ref, g_ref, w_ref, r_ref, o_ref, xn_ref = refs
    else:
        x_ref, g_ref, w_ref, o_ref, xn_ref = refs

    @pl.when(pl.program_id(1) == 0)
    def _():
        x = x_ref[...]
        if norm:
            x = _rms(x, g_ref[...])
        xn_ref[...] = x.astype(BF16)

    w = w_ref[...].astype(BF16)
    acc = _dot_nt(xn_ref[...], w) if w_rows_out else _dot(xn_ref[...], w)
    if residual:
        acc = acc + r_ref[...]
    o_ref[...] = acc


def proj(x, g, w, residual=None, norm=True, n_out=None, layer=None, w_rows_out=False):
    M, K = x.shape
    N = w.shape[1 if w_rows_out else -1] if n_out is None else n_out
    tm = _pick_tile(M, (1024, 512, 256, 128, 64, 32, 16))
    tn = _pick_tile(N, (512, 384, 256, 128))
    if w_rows_out:
        wspec = pl.BlockSpec((None, tn, K), lambda i, j: (layer, j, 0))
    elif layer is None:
        wspec = pl.BlockSpec((K, tn), lambda i, j: (0, j))
    else:
        wspec = pl.BlockSpec((None, K, tn), lambda i, j: (layer, 0, j))
    in_specs = [pl.BlockSpec((tm, K), lambda i, j: (i, 0)),
                pl.BlockSpec((1, K), lambda i, j: (0, 0)),
                wspec]
    args = [x, g.reshape(1, K), w]
    if residual is not None:
        in_specs.append(pl.BlockSpec((tm, tn), lambda i, j: (i, j)))
        args.append(residual)
    return pl.pallas_call(
        functools.partial(_proj_body, norm=norm, residual=residual is not None, w_rows_out=w_rows_out),
        grid=(M // tm, N // tn),
        in_specs=in_specs,
        out_specs=pl.BlockSpec((tm, tn), lambda i, j: (i, j)),
        out_shape=jax.ShapeDtypeStruct((M, N), F32),
        scratch_shapes=[pltpu.VMEM((tm, K), BF16)],
        compiler_params=_cparams(("parallel", "arbitrary")),
        name="proj",
    )(*args)


def _ffn_body(x_ref, g_ref, wa_ref, wu_ref, cw_ref, cb_ref, wd_ref, go_ref, b0_ref, b1_ref,
              y_ref, sa_ref, xn_ref, carry_ref, *, rows_are_batch, final_norm, nf):
    t = pl.program_id(0)
    f = pl.program_id(1)
    S, tm = x_ref.shape[0], x_ref.shape[1]

    @pl.when(f == 0)
    def _():
        for s in range(S):
            xn_ref[s] = _rms(x_ref[s], g_ref[...]).astype(BF16)
            y_ref[s] = x_ref[s]

    wa = wa_ref[...].astype(BF16)
    wu = wu_ref[...].astype(BF16)
    wd = wd_ref[...].astype(BF16)
    cw = cw_ref[...]
    cb = cb_ref[...]
    row = lax.broadcasted_iota(jnp.int32, (tm, 1), 0)
    for s in range(S):
        xn = xn_ref[s]
        a = _dot(xn, wa)
        u = _dot(xn, wu)
        if rows_are_batch:
            am2 = b0_ref[...]
            am1 = b1_ref[...]
        else:
            prev = jnp.where(t == 0, 0.0, carry_ref[f, s])
            p1 = prev[7:8]
            p2 = prev[6:7]
            am1 = jnp.where(row >= 1, pltpu.roll(a, 1, 0), p1)
            am2 = jnp.where(row >= 2, pltpu.roll(a, 2, 0), jnp.where(row == 1, p1, p2))
            carry_ref[f, s] = a[tm - 8:tm]
        conv = cb + am2 * cw[0:1] + am1 * cw[1:2] + a * cw[2:3]
        gt = conv * _sigmoid(conv) * u
        y_ref[s] += _dot(gt.astype(BF16), wd)
        sa_ref[s, 0] = a[0:8] if rows_are_batch else a[tm - 8:tm]

    if final_norm:
        @pl.when(f == nf - 1)
        def _():
            for s in range(S):
                y_ref[s] = _rms(y_ref[s], go_ref[...])


def conv_ffn(x, g, w_up, conv_w, conv_b, w_down, layer, g_out, buf=None, final_norm=False):
    B, T, D = x.shape
    F = w_down.shape[1]
    rows_are_batch = buf is not None
    tm = _pick_tile(T, (512, 256, 128, 64, 32, 16, 8))
    tf = _pick_tile(F, (512, 256, 128))
    nf = F // tf
    if buf is None:
        b0 = b1 = jnp.zeros((8, F), F32)
        bspec = pl.BlockSpec((8, tf), lambda t, f: (0, f))
    else:
        b0, b1 = buf[0], buf[1]
        bspec = pl.BlockSpec((tm, tf), lambda t, f: (0, f))
    y, sa = pl.pallas_call(
        functools.partial(_ffn_body, rows_are_batch=rows_are_batch, final_norm=final_norm, nf=nf),
        grid=(T // tm, nf),
        in_specs=[pl.BlockSpec((B, tm, D), lambda t, f: (0, t, 0), pipeline_mode=pl.Buffered(1)),
                  pl.BlockSpec((1, D), lambda t, f: (0, 0)),
                  pl.BlockSpec((None, D, tf), lambda t, f: (layer, 0, f)),
                  pl.BlockSpec((None, D, tf), lambda t, f: (layer, 0, nf + f)),
                  pl.BlockSpec((None, 3, tf), lambda t, f: (layer, 0, f)),
                  pl.BlockSpec((None, 1, tf), lambda t, f: (layer, 0, f)),
                  pl.BlockSpec((None, tf, D), lambda t, f: (layer, f, 0)),
                  pl.BlockSpec((1, D), lambda t, f: (0, 0)),
                  bspec, bspec],
        out_specs=[pl.BlockSpec((B, tm, D), lambda t, f: (0, t, 0), pipeline_mode=pl.Buffered(1)),
                   pl.BlockSpec((B, 1, 8, tf), lambda t, f: (0, t, 0, f))],
        out_shape=[jax.ShapeDtypeStruct((B, T, D), F32), jax.ShapeDtypeStruct((B, T // tm, 8, F), F32)],
        scratch_shapes=[pltpu.VMEM((B, tm, D), BF16), pltpu.VMEM((nf, B, 8, tf), F32)],
        compiler_params=_cparams(("arbitrary", "arbitrary"), FFN_VMEM_LIMIT),
        name="conv_ffn",
    )(x, g.reshape(1, D), w_up, w_up, conv_w, conv_b[:, None, :], w_down, g_out.reshape(1, D), b0, b1)
    return y, sa[:, -1]


def rope_tables(pos):
    inv = ROPE_THETA ** (-jnp.arange(ROPE_HALF, dtype=F32) / ROPE_HALF)
    ang = pos.astype(F32)[:, None] * inv[None, :]
    c, s = jnp.cos(ang), jnp.sin(ang)
    T = pos.shape[0]
    one = jnp.ones((T, HD - ROPE_DIM), F32)
    zero = jnp.zeros((T, HD - ROPE_DIM), F32)
    zh = jnp.zeros((T, ROPE_HALF), F32)
    cos = jnp.concatenate([c, c, one], axis=1)
    sin_lo = jnp.concatenate([-s, zh, zero], axis=1)
    sin_hi = jnp.concatenate([zh, s, zero], axis=1)
    return cos, sin_lo, sin_hi


def _rope_body(x_ref, c_ref, sl_ref, sh_ref, o_ref):
    c, sl, sh = c_ref[...], sl_ref[...], sh_ref[...]
    for h in range(x_ref.shape[1] // HD):
        x = x_ref[:, h * HD:(h + 1) * HD]
        o_ref[:, h * HD:(h + 1) * HD] = (x * c + pltpu.roll(x, HD - ROPE_HALF, 1) * sl
                                         + pltpu.roll(x, ROPE_HALF, 1) * sh)


def nsa_rope(z, tables, T):
    M = z.shape[0]
    tm = _pick_tile(T, (512, 256, 128, 64, 32, 16, 8))
    nt = T // tm
    W = 4 * HD

    def col(j):
        return j + 2 * (j >= 4).astype(jnp.int32) + (j >= 5).astype(jnp.int32)

    tspec = pl.BlockSpec((tm, HD), lambda i, j: (i % nt, 0))
    return pl.pallas_call(
        _rope_body,
        grid=(M // tm, 6),
        in_specs=[pl.BlockSpec((tm, W), lambda i, j: (i, col(j))), tspec, tspec, tspec],
        out_specs=pl.BlockSpec((tm, W), lambda i, j: (i, j)),
        out_shape=jax.ShapeDtypeStruct((M, 6 * W), F32),
        compiler_params=_cparams(("parallel", "arbitrary")),
        name="nsa_rope",
    )(z, *tables)


def _gelu(x):
    return 0.5 * x * (1.0 + jnp.tanh(math.sqrt(2.0 / math.pi) * (x + 0.044715 * x * x * x)))


def _pe_hidden(pe_ref, w1_ref):
    acc = jnp.zeros((8, HD), F32)
    for l in range(CMP_BLOCK):
        row = jnp.broadcast_to(pe_ref[0, l:l + 1, :], (8, HD))
        acc = acc + _dot(row.astype(BF16), w1_ref[0, l].astype(BF16))
    return acc[0:1]


def _cmp_body(x_ref, w1_ref, pe_ref, w2_ref, o_ref, *, nseg):
    parts = []
    for r in range(CMP_RATIO):
        acc = jnp.zeros((nseg, HD), F32)
        for j in range(CMP_STRIDE):
            xj = x_ref[pl.ds(j, nseg, stride=CMP_STRIDE), :]
            acc = acc + _dot(xj.astype(BF16), w1_ref[0, CMP_STRIDE * r + j].astype(BF16))
        parts.append(acc)
    hid = _pe_hidden(pe_ref, w1_ref) + parts[0] + pltpu.roll(parts[1], nseg - 1, 0)
    o_ref[0, 0, 0] = _dot(_gelu(hid).astype(BF16), w2_ref[0].astype(BF16))


def nsa_compress(z, w1, pe, w2, B, T):
    nseg = T // CMP_STRIDE
    return pl.pallas_call(
        functools.partial(_cmp_body, nseg=nseg),
        grid=(2, B, NSA_KVH),
        in_specs=[pl.BlockSpec((T, HD), lambda s, b, k: (b, NSA_QW // HD + NSA_KVH * s + k)),
                  pl.BlockSpec((1, CMP_BLOCK, HD, HD), lambda s, b, k: (s, 0, 0, 0)),
                  pl.BlockSpec((1, CMP_BLOCK, HD), lambda s, b, k: (s, 0, 0)),
                  pl.BlockSpec((1, HD, HD), lambda s, b, k: (s, 0, 0))],
        out_specs=pl.BlockSpec((1, 1, 1, nseg, HD), lambda s, b, k: (s, b, k, 0, 0)),
        out_shape=jax.ShapeDtypeStruct((2, B, NSA_KVH, nseg, HD), F32),
        compiler_params=_cparams(("parallel", "parallel", "parallel")),
        name="nsa_compress",
    )(z, w1, pe, w2)


def _overlap_t(nsp, nseg, n_slc):
    j = lax.broadcasted_iota(jnp.int32, (nsp, 1), 0)
    m = lax.broadcasted_iota(jnp.int32, (1, nseg), 1)
    ov = ((m * CMP_STRIDE <= j * SLC_BLOCK + SLC_BLOCK - 1) & (m * CMP_STRIDE + CMP_BLOCK - 1 >= j * SLC_BLOCK)
          & (m < nseg - CMP_RATIO + 1) & (j < n_slc))
    return ov.astype(F32)


def _rank_rows(score, n):
    j = lax.broadcasted_iota(jnp.int32, (score.shape[0], 1), 0)
    rank = jnp.zeros(score.shape, F32)
    for i in range(n):
        si = score[i:i + 1, :]
        rank = rank + ((si > score) | ((si == score) & (i < j))).astype(F32)
    return rank


def _nsa_attn_body(qraw_ref, qrot_ref, kc_ref, vc_ref, ks_ref, vs_ref, kw_ref, vw_ref, gate_ref, o_ref,
                   ksb_ref, vsb_ref, kwb_ref, vwb_ref, *, T, tq, tk):
    qi = pl.program_id(2)
    t0 = qi * tq
    scale = HD ** -0.5
    nseg = T // CMP_STRIDE
    n_slc = T // SLC_BLOCK
    nsp = max(8, n_slc)
    pos = t0 + lax.broadcasted_iota(jnp.int32, (tq, 1), 0)
    pos_row = t0 + lax.broadcasted_iota(jnp.int32, (1, tq), 1)

    @pl.when(qi == 0)
    def _():
        ksb_ref[:, :HD] = ks_ref[...].astype(BF16)
        ksb_ref[:, HD:] = (lax.broadcasted_iota(jnp.int32, (T, 1), 0) // SLC_BLOCK
                           == lax.broadcasted_iota(jnp.int32, (1, LANE), 1)).astype(BF16)
        for dst, src in zip((vsb_ref, kwb_ref, vwb_ref), (vs_ref, kw_ref, vw_ref)):
            dst[...] = src[...].astype(BF16)

    def heads_on_rows(ref):
        return jnp.concatenate([ref[:, g * HD:(g + 1) * HD] for g in range(NSA_G)], axis=0)

    def per_head(x):
        return jnp.concatenate([x] * NSA_G, axis=0)

    q_raw = (heads_on_rows(qraw_ref) * scale).astype(BF16)
    q_rot = (heads_on_rows(qrot_ref) * scale).astype(BF16)

    kc = kc_ref[0, 0, 0].astype(BF16)
    vc = vc_ref[0, 0, 0].astype(BF16)
    m_idx = lax.broadcasted_iota(jnp.int32, (1, nseg), 1)
    valid = (m_idx * CMP_STRIDE + CMP_BLOCK - 1 <= pos) & (m_idx < nseg - CMP_RATIO + 1)
    s = _dot_nt(q_raw, kc) + per_head(jnp.where(valid, 0.0, NEG))
    e = jnp.exp(s - jnp.max(s, axis=-1, keepdims=True))
    p = e * (1.0 / jnp.sum(e, axis=-1, keepdims=True)) * per_head(valid.astype(F32))
    o_cmp = _dot(p.astype(BF16), vc)
    psum = p[0:tq]
    for g in range(1, NSA_G):
        psum = psum + p[g * tq:(g + 1) * tq]

    imp_t = lax.dot_general(_overlap_t(nsp, nseg, n_slc), psum, (((1,), (1,)), ((), ())),
                            preferred_element_type=F32, precision=lax.Precision.HIGHEST)
    j = lax.broadcasted_iota(jnp.int32, (nsp, 1), 0)
    cur = pos_row // SLC_BLOCK
    forced = (j == 0) | (j == cur) | (j == cur - 1)
    score = jnp.where(j * SLC_BLOCK <= pos_row, imp_t + FORCE_BONUS * forced.astype(F32), -1.0)
    score = jnp.where(j < n_slc, score, -2.0)
    sel_t = ((_rank_rows(score, n_slc) < min(SLC_TOP_N, n_slc)) & (j < n_slc)).astype(F32)
    if nsp < LANE:
        sel_t = jnp.concatenate([sel_t, jnp.zeros((LANE - nsp, tq), F32)], axis=0)
    sel = sel_t.T
    def update(s, v, carry):
        m_old, l_old, acc = carry
        m_new = jnp.maximum(m_old, jnp.max(s, axis=-1, keepdims=True))
        alpha = jnp.exp(m_old - m_new)
        p = jnp.exp(s - m_new)
        return (m_new, alpha * l_old + jnp.sum(p, axis=-1, keepdims=True), alpha * acc + _dot(p.astype(BF16), v))

    R = NSA_G * tq
    init = (jnp.full((R, 1), NEG, F32), jnp.zeros((R, 1), F32), jnp.zeros((R, HD), F32))

    q_aug = jnp.concatenate([q_rot, per_head(((sel - 1.0) * -NEG).astype(BF16))], axis=1)

    def slc_scores(kt):
        k0 = pl.multiple_of(kt * tk, tk)
        return _dot_nt(q_aug, ksb_ref[pl.ds(k0, tk), :]), vsb_ref[pl.ds(k0, tk), :]

    def slc_step(kt, carry):
        s, v = slc_scores(kt)
        return update(s, v, carry)

    kt_diag = t0 // tk
    slc = lax.fori_loop(0, kt_diag, slc_step, init)
    s, v = slc_scores(kt_diag)
    kpos = kt_diag * tk + lax.broadcasted_iota(jnp.int32, (1, tk), 1)
    slc = update(s + per_head(jnp.where(kpos <= pos, 0.0, NEG)), v, slc)

    nkw = min(WINDOW + tq, T)
    k0w = pl.multiple_of(jnp.clip(t0 - WINDOW, 0, T - nkw), tq)
    d = pos - (k0w + lax.broadcasted_iota(jnp.int32, (1, nkw), 1))
    win = update(_dot_nt(q_rot, kwb_ref[pl.ds(k0w, nkw), :]) + per_head(jnp.where((d >= 0) & (d < WINDOW), 0.0, NEG)),
                 vwb_ref[pl.ds(k0w, nkw), :], init)

    gate = _sigmoid(gate_ref[...])
    o_slc = slc[2] / slc[1]
    o_win = win[2] / win[1]
    for g in range(NSA_G):
        rows = slice(g * tq, (g + 1) * tq)
        o_ref[:, g * HD:(g + 1) * HD] = (gate[:, 3 * g:3 * g + 1] * o_cmp[rows]
                                         + gate[:, 3 * g + 1:3 * g + 2] * o_slc[rows]
                                         + gate[:, 3 * g + 2:3 * g + 3] * o_win[rows])


def nsa_attn_prefill(z, rot, cmp, gates, B, T):
    assert T % LANE == 0 and T // SLC_BLOCK <= LANE, "selection blocks must fit one lane group"
    tq = LANE
    tk = _pick_tile(T, (1024, 512, 256, 128))
    nq = T // tq
    nseg = T // CMP_STRIDE
    W = NSA_G * HD
    qspec = pl.BlockSpec((tq, W), lambda b, k, q: (b * nq + q, k))
    cspec = lambda s: pl.BlockSpec((1, 1, 1, nseg, HD), lambda b, k, q: (s, b, k, 0, 0))
    kvspec = lambda c0: pl.BlockSpec((T, HD), lambda b, k, q: (b, c0 + k))
    return pl.pallas_call(
        functools.partial(_nsa_attn_body, T=T, tq=tq, tk=tk),
        grid=(B, NSA_KVH, nq),
        in_specs=[qspec, qspec, cspec(0), cspec(1),
                  kvspec(NSA_QW // HD), kvspec((NSA_QW + 3 * W) // HD),
                  kvspec((NSA_QW + W) // HD), kvspec((NSA_QW + 5 * W) // HD),
                  pl.BlockSpec((tq, LANE), lambda b, k, q: (b * nq + q, k))],
        out_specs=qspec,
        out_shape=jax.ShapeDtypeStruct((B * T, NSA_QW), F32),
        scratch_shapes=[pltpu.VMEM((T, 2 * HD), BF16)] + [pltpu.VMEM((T, HD), BF16) for _ in range(3)],
        compiler_params=_cparams(("parallel", "parallel", "arbitrary")),
        name="nsa_attn",
    )(z, rot, cmp, cmp, rot, z, rot, z, gates)


def _nsa_gate_weight(w_in, a):
    K = w_in.shape[1]
    wg = w_in[a, :, NSA_MAIN:].reshape(K, NSA_KVH, 3 * NSA_G)
    return jnp.pad(wg, ((0, 0), (0, 0), (0, LANE - 3 * NSA_G))).reshape(K, NSA_KVH * LANE)


def _nsa_rows_body(*refs, n_layers):
    o_ref = refs[-1]
    per_layer = 4 * NSA_KVH
    for a in range(n_layers):
        @pl.when(pl.program_id(1) == a)
        def _(a=a):
            for h, src in enumerate(refs[a * per_layer:(a + 1) * per_layer]):
                o_ref[:, h, :] = src[...]


def nsa_rows(zs, rots, T):
    M = zs[0].shape[0]
    n_layers = len(zs)
    tm = _pick_tile(T, (512, 256, 128, 64, 32, 16, 8))
    nh = NSA_KVH
    q0 = NSA_QW // HD
    cols = []
    for z, rot in zip(zs, rots):
        cols += ([(z, q0 + h) for h in range(2 * nh)] + [(rot, q0 + h) for h in range(nh)]
                 + [(z, q0 + 3 * nh + h) for h in range(nh)])
    in_specs = [pl.BlockSpec((tm, HD), functools.partial(lambda i, a, c: (i, c), c=c)) for _, c in cols]
    return pl.pallas_call(
        functools.partial(_nsa_rows_body, n_layers=n_layers),
        grid=(M // tm, n_layers),
        in_specs=in_specs,
        out_specs=pl.BlockSpec((tm, None, 4 * NSA_KVH, HD), lambda i, a: (i, a, 0, 0)),
        out_shape=jax.ShapeDtypeStruct((M, n_layers, 4 * NSA_KVH, HD), F32),
        compiler_params=_cparams(("parallel", "arbitrary")),
        name="nsa_rows",
    )(*[arr for arr, _ in cols])


def nsa_layer_prefill(x, g, w_in, w1, pe, w2, w_out, a, tables, B, T):
    z = proj(x, g, jnp.swapaxes(w_in, 1, 2), n_out=NSA_MAIN, layer=a, w_rows_out=True)
    gates = proj(x, g, _nsa_gate_weight(w_in, a))
    rot = nsa_rope(z, tables, T)
    cmp = nsa_compress(z, w1, pe, w2, B, T)
    mix = nsa_attn_prefill(z, rot, cmp, gates, B, T)
    y = proj(mix, g, w_out, residual=x, norm=False, layer=a)
    W = NSA_KVH * HD
    Tw = min(WINDOW, T)
    win = jnp.concatenate([rot.reshape(B, T, -1)[:, T - Tw:, NSA_QW + W:], z.reshape(B, T, -1)[:, T - Tw:, NSA_QW + 5 * W:]],
                          axis=2).reshape(B, Tw, 2, NSA_KVH, HD)
    return y, (z, rot), win


def _round_robin(gens):
    while gens:
        alive = []
        for gen in gens:
            try:
                next(gen)
                alive.append(gen)
            except StopIteration:
                pass
        gens = alive


def _tri(c):
    return (lax.broadcasted_iota(jnp.int32, (c, 1), 0) >= lax.broadcasted_iota(jnp.int32, (1, c), 1)).astype(F32)


def _hgrn_body(q_ref, f_ref, i_ref, gz_ref, lb_ref, gn_ref, s0_ref, o_ref, s_ref, st_ref,
               *, C, SC, HP, t_valid, nt, layer):
    t = pl.program_id(2)
    Tt = q_ref.shape[0]

    @pl.when(t == 0)
    def _():
        for hp in range(HP):
            st_ref[hp] = s0_ref[0, hp].T

    lbp = lb_ref[...]
    e = jnp.exp(lbp - jnp.max(lbp, axis=0, keepdims=True))
    lb_all = jnp.sum(e[1:layer + 1], axis=0, keepdims=True) / jnp.sum(e, axis=0, keepdims=True)
    gn_all = gn_ref[...]
    tri = _tri(C)
    row8 = lax.broadcasted_iota(jnp.int32, (8, 1), 0)

    def head_chunk(hp, r0):
        cols = slice(hp * HD, (hp + 1) * HD)
        lb, gn = lb_all[:, cols], gn_all[:, cols]
        qz = q_ref[pl.ds(r0, C), cols]
        f = lb + (1.0 - lb) * _sigmoid(f_ref[pl.ds(r0, C), cols])
        v = i_ref[pl.ds(r0, C), cols]
        gz = gz_ref[pl.ds(r0, C), cols]
        k = 1.0 - f
        lf = jnp.log(f)
        if t_valid is not None:
            ok = (t * Tt + r0 + lax.broadcasted_iota(jnp.int32, (C, 1), 0)) < t_valid
            k = jnp.where(ok, k, 0.0)
            lf = jnp.where(ok, lf, 0.0)
        q = qz * _sigmoid(qz)
        b = _dot_hi(tri, lf)
        yield
        st = st_ref[hp]
        vb = v.astype(BF16)
        out = _dot_nt((q * jnp.exp(b)).astype(BF16), st.astype(BF16))
        bl = b[C - 1:C]
        st_ref[hp] = st * jnp.exp(bl) + _dot_tn(vb, (k * jnp.exp(bl - b)).astype(BF16))
        atts = []
        for I in range(1, C // SC):
            r = I * SC
            beta = b[r - 1:r]
            qt = q[r:r + SC] * jnp.exp(b[r:r + SC] - beta)
            kt = k[0:r] * jnp.exp(beta - b[0:r])
            atts.append(_dot_nt(qt.astype(BF16), kt.astype(BF16)))
        yield
        offdiag = [_dot(att.astype(BF16), vb[0:I * SC]) for I, att in enumerate(atts, 1)]
        pieces = []
        for I in range(C // SC):
            r = I * SC
            bI, qI, kI, vI = b[r:r + SC], q[r:r + SC], k[r:r + SC], v[r:r + SC]
            groups = [jnp.zeros((8, HD), F32) for _ in range(SC // 8)]
            for s in range(SC):
                for gi in range(s // 8, SC // 8):
                    rows = slice(gi * 8, gi * 8 + 8)
                    d = bI[rows] - bI[s:s + 1]
                    if gi == s // 8:
                        d = jnp.where(row8 >= s % 8, d, NEG)
                    att = jnp.sum(qI[rows] * jnp.exp(d) * kI[s:s + 1], axis=-1, keepdims=True)
                    groups[gi] = groups[gi] + att * vI[s:s + 1]
            pieces.extend(groups)
        yield
        intra = jnp.concatenate(pieces, axis=0)
        if offdiag:
            intra = intra + jnp.concatenate([jnp.zeros((SC, HD), F32)] + offdiag, axis=0)
        out = out + intra
        o_ref[pl.ds(r0, C), cols] = _rms(out, gn) * (gz * _sigmoid(gz))

    def chunk(c, carry):
        r0 = pl.multiple_of(c * C, C)
        _round_robin([head_chunk(hp, r0) for hp in range(HP)])
        return carry

    lax.fori_loop(0, Tt // C, chunk, 0)

    @pl.when(t == nt - 1)
    def _():
        for hp in range(HP):
            s_ref[0, hp] = st_ref[hp].T


def hgrn_scan(z, lb, layer, gn, s0, B, T, t_valid=None):
    depth = lb.shape[0]
    C = min(CHUNK, T)
    SC = min(SUB, C)
    Tt = _pick_tile(T, (512, 256, 128, 64, 32, 16))
    nt = T // Tt
    HP = HG_HEADS_PER_STEP
    ng = HG_HEADS // HP
    W = HP * HD
    zspec = lambda part: pl.BlockSpec((Tt, W), lambda b, h, t: (b * nt + t, part * ng + h))
    vspec = pl.BlockSpec((1, W), lambda b, h, t: (0, h))
    lspec = pl.BlockSpec((depth, W), lambda b, h, t: (0, h))
    sspec = pl.BlockSpec((1, HP, HD, HD), lambda b, h, t: (b, h, 0, 0))
    return pl.pallas_call(
        functools.partial(_hgrn_body, C=C, SC=SC, HP=HP, t_valid=t_valid, nt=nt, layer=layer),
        grid=(B, ng, nt),
        in_specs=[zspec(0), zspec(1), zspec(2), zspec(3), lspec, vspec, sspec],
        out_specs=[pl.BlockSpec((Tt, W), lambda b, h, t: (b * nt + t, h)), sspec],
        out_shape=[jax.ShapeDtypeStruct((B * T, HG_W), F32), jax.ShapeDtypeStruct((B, HG_HEADS, HD, HD), F32)],
        scratch_shapes=[pltpu.VMEM((HP, HD, HD), F32)],
        compiler_params=_cparams(("parallel", "parallel", "arbitrary")),
        name="hgrn_scan",
    )(z, z, z, z, lb, gn.reshape(1, HG_W), s0)


def _mlstm_body(q_ref, k_ref, v_ref, og_ref, gt_ref, bg_ref, c0_ref, n0_ref, m0_ref,
                y_ref, c_ref, n_ref, m_ref, cs_ref, ns_ref, ms_ref, *, C, HP, t_valid, nt):
    t = pl.program_id(2)
    Tt = q_ref.shape[0]

    @pl.when(t == 0)
    def _():
        cs_ref[...] = c0_ref[0]
        ns_ref[...] = n0_ref[0]
        ms_ref[...] = m0_ref[0]

    tri_b = lax.broadcasted_iota(jnp.int32, (C, 1), 0) >= lax.broadcasted_iota(jnp.int32, (1, C), 1)
    upper = (lax.broadcasted_iota(jnp.int32, (C, 1), 0) <= lax.broadcasted_iota(jnp.int32, (1, C), 1)).astype(F32)
    eye = (lax.broadcasted_iota(jnp.int32, (C, 1), 0) == lax.broadcasted_iota(jnp.int32, (1, C), 1)).astype(F32)
    tri = tri_b.astype(F32)
    ones = jnp.ones((C, C), F32)
    lane = lax.broadcasted_iota(jnp.int32, (1, LANE), 1)
    bg = bg_ref[...]

    def head_chunk(hp, r0, g):
        h = pl.program_id(1) * HP + hp
        kcols = slice(hp * ML_DK, (hp + 1) * ML_DK)
        vcols = slice(hp * ML_DV, (hp + 1) * ML_DV)
        ig = jnp.sum(jnp.where(lane == h, g, 0.0), axis=-1, keepdims=True)
        fg = jnp.sum(jnp.where(lane == ML_HEADS + h, g, 0.0), axis=-1, keepdims=True)
        lf = jnp.minimum(fg, 0.0) - jnp.log(1.0 + jnp.exp(-jnp.abs(fg)))
        if t_valid is not None:
            ok = (t * Tt + r0 + lax.broadcasted_iota(jnp.int32, (C, 1), 0)) < t_valid
            lf = jnp.where(ok, lf, 0.0)
            ig = jnp.where(ok, ig, NEG)
        lf_b = jnp.broadcast_to(lf, (C, C))
        ig_b = jnp.broadcast_to(ig, (C, C))
        b_col = _dot_hi(tri, lf_b)
        b_row_i = _dot_hi(ones, lf_b * upper - ig_b * eye)
        q = q_ref[pl.ds(r0, C), kcols]
        ks = k_ref[pl.ds(r0, C), kcols] * (ML_DK ** -0.5)
        v = v_ref[pl.ds(r0, C), vcols]
        qb, kb, vb = q.astype(BF16), ks.astype(BF16), v.astype(BF16)
        cs = cs_ref[hp]
        ns = ns_ref[hp]
        qk = _dot_nt(qb, kb)
        inter = _dot(qb, cs.astype(BF16))
        yield
        logw = jnp.where(tri_b, b_col - b_row_i, NEG)
        m_old = ms_ref[hp][:, 0:1]
        b1 = b_col[:, 0:1]
        m_prev = b1 + m_old
        m_t = jnp.maximum(m_prev, jnp.max(logw, axis=-1, keepdims=True))
        w0 = jnp.exp(m_prev - m_t)
        sqk = qk * jnp.exp(logw - m_t)
        intra = _dot(sqk.astype(BF16), vb)
        m_new = m_t[C - 1:C]
        b_last = b1[C - 1:C]
        wk = jnp.exp(b_last - b1 + ig - m_new)
        decay = jnp.exp(b_last + m_old - m_new)
        kw = ks * wk
        cs_ref[hp] = decay * cs + _dot_tn(kw.astype(BF16), vb)
        ns_ref[hp] = decay * ns + jnp.sum(kw, axis=0, keepdims=True)
        ms_ref[hp] = jnp.broadcast_to(m_new, (1, LANE))
        den = w0 * jnp.sum(q * ns, axis=-1, keepdims=True) + jnp.sum(sqk, axis=-1, keepdims=True)
        yield
        hc = (w0 * inter + intra) / jnp.maximum(jnp.abs(den), jnp.exp(-m_t))
        y_ref[pl.ds(r0, C), vcols] = _sigmoid(og_ref[pl.ds(r0, C), vcols]) * hc

    def chunk(c, carry):
        r0 = pl.multiple_of(c * C, C)
        g = gt_ref[pl.ds(r0, C), :] + bg
        g = ML_GATE_CAP * jnp.tanh(g / ML_GATE_CAP)
        _round_robin([head_chunk(hp, r0, g) for hp in range(HP)])
        return carry

    lax.fori_loop(0, Tt // C, chunk, 0)

    @pl.when(t == nt - 1)
    def _():
        c_ref[0] = cs_ref[...]
        n_ref[0] = ns_ref[...]
        m_ref[0] = ms_ref[...]


def mlstm_scan(z, gates, b_gate, c0, n0, m0, B, T, t_valid=None):
    C = min(CHUNK, T)
    Tt = _pick_tile(T, (512, 256, 128, 64, 32, 16))
    nt = T // Tt
    H = ML_HEADS
    row = lambda b, t: b * nt + t
    bg = jnp.pad(b_gate, (0, LANE - 2 * H)).reshape(1, LANE)
    n0 = n0.reshape(B, H, 1, ML_DK)
    m0 = jnp.broadcast_to(m0[:, :, None, None], (B, H, 1, LANE))
    HP = ML_HEADS_PER_STEP
    ng = H // HP
    kw, vw = HP * ML_DK, HP * ML_DV
    vec = lambda w: pl.BlockSpec((1, HP, 1, w), lambda b, h, t: (b, h, 0, 0))
    cspec = pl.BlockSpec((1, HP, ML_DK, ML_DV), lambda b, h, t: (b, h, 0, 0))
    y, c, n, m = pl.pallas_call(
        functools.partial(_mlstm_body, C=C, HP=HP, t_valid=t_valid, nt=nt),
        grid=(B, ng, nt),
        in_specs=[pl.BlockSpec((Tt, kw), lambda b, h, t: (row(b, t), h)),
                  pl.BlockSpec((Tt, kw), lambda b, h, t: (row(b, t), ng + h)),
                  pl.BlockSpec((Tt, vw), lambda b, h, t: (row(b, t), 2 * ML_QK // vw + h)),
                  pl.BlockSpec((Tt, vw), lambda b, h, t: (row(b, t), (2 * ML_QK + ML_VW) // vw + h)),
                  pl.BlockSpec((Tt, LANE), lambda b, h, t: (row(b, t), 0)),
                  pl.BlockSpec((1, LANE), lambda b, h, t: (0, 0)),
                  cspec, vec(ML_DK), vec(LANE)],
        out_specs=[pl.BlockSpec((Tt, vw), lambda b, h, t: (row(b, t), h)), cspec, vec(ML_DK), vec(LANE)],
        out_shape=[jax.ShapeDtypeStruct((B * T, ML_VW), F32), jax.ShapeDtypeStruct((B, H, ML_DK, ML_DV), F32),
                   jax.ShapeDtypeStruct((B, H, 1, ML_DK), F32), jax.ShapeDtypeStruct((B, H, 1, LANE), F32)],
        scratch_shapes=[pltpu.VMEM((HP, ML_DK, ML_DV), F32), pltpu.VMEM((HP, 1, ML_DK), F32),
                        pltpu.VMEM((HP, 1, LANE), F32)],
        compiler_params=_cparams(("parallel", "parallel", "arbitrary")),
        name="mlstm_scan",
    )(z, z, z, z, gates, bg, c0, n0, m0)
    return y, c, n.reshape(B, H, ML_DK), m[:, :, 0, 0]


PAGES_PER_STEP = 16
SEG_PER_PAGE = PAGE_SIZE // CMP_STRIDE
TILE_ROWS = 2 * NSA_KVH


def _cmp_paged_body(pt_ref, *refs):
    P = PAGES_PER_STEP
    x_refs, w_ref, o_ref, lhs_ref = refs[:P], refs[P], refs[P + 1], refs[P + 2]
    rows = SEG_PER_PAGE * TILE_ROWS
    for p in range(P):
        for j in range(CMP_STRIDE):
            xj = x_refs[p][pl.ds(j, SEG_PER_PAGE, stride=CMP_STRIDE)]
            lhs_ref[p * rows:(p + 1) * rows, j * HD:(j + 1) * HD] = xj.reshape(rows, HD).astype(BF16)
    res = _dot(lhs_ref[...], w_ref[...])
    c = lax.broadcasted_iota(jnp.int32, (P * rows, 1), 0) % TILE_ROWS
    sel = jnp.where(c < NSA_KVH, res[:, :2 * HD], res[:, 2 * HD:])
    o_ref[0, 0] = sel[:, :HD]
    o_ref[0, 1] = sel[:, HD:]


def nsa_compress_paged(cache4, page_table, w1, a):
    B, n_pages = page_table.shape
    P = PAGES_PER_STEP
    rows = SEG_PER_PAGE * TILE_ROWS
    w_cat = w1.reshape(2, CMP_RATIO, CMP_STRIDE, HD, HD).transpose(2, 3, 0, 1, 4).reshape(CMP_STRIDE * HD, 4 * HD)
    xspec = lambda i: pl.BlockSpec((PAGE_SIZE, None, TILE_ROWS, HD),
                                   lambda b, s, pt: (pt[b, s * P + i], 2 * a, 0, 0))
    grid_spec = pltpu.PrefetchScalarGridSpec(
        num_scalar_prefetch=1,
        grid=(B, n_pages // P),
        in_specs=[xspec(i) for i in range(P)] + [pl.BlockSpec((CMP_STRIDE * HD, 4 * HD), lambda b, s, pt: (0, 0))],
        out_specs=pl.BlockSpec((1, CMP_RATIO, P * rows, HD), lambda b, s, pt: (b, 0, s, 0)),
        scratch_shapes=[pltpu.VMEM((P * rows, CMP_STRIDE * HD), BF16)])
    return pl.pallas_call(
        _cmp_paged_body,
        grid_spec=grid_spec,
        out_shape=jax.ShapeDtypeStruct((B, CMP_RATIO, n_pages * rows, HD), F32),
        compiler_params=_cparams(("parallel", "arbitrary")),
        name="nsa_compress_paged",
    )(page_table, *([cache4] * P), w_cat.astype(BF16))


def _nsa_dec_cmp_body(part_ref, w1_ref, pe_ref, w2_ref, q_ref, o_ref, idx_ref, *, n_seg, n_slc, pos):
    nsp = -(-n_slc // LANE) * LANE
    scale = HD ** -0.5
    pe_hid = [_pe_hidden(pe_ref.at[pl.ds(s, 1)], w1_ref.at[pl.ds(s, 1)]) for s in range(2)]
    q = q_ref[0].astype(BF16)
    head_kv = lax.broadcasted_iota(jnp.int32, (NSA_HEADS, 1), 0) // NSA_G
    col_m = lax.broadcasted_iota(jnp.int32, (1, n_seg), 1)
    valid_m = (col_m < n_seg - CMP_RATIO + 1) & (col_m * CMP_STRIDE + CMP_BLOCK - 1 <= pos)

    def tokens(slot, kh):
        rows = pl.ds(slot * NSA_KVH + kh, n_seg, stride=TILE_ROWS)
        hid = part_ref[0, 0, rows, :] + pltpu.roll(part_ref[0, 1, rows, :], n_seg - 1, 0) + pe_hid[slot]
        return _dot(_gelu(hid).astype(BF16), w2_ref[slot].astype(BF16)).astype(BF16)

    o_cmp = jnp.zeros((NSA_HEADS, HD), F32)
    psum_rows = []
    for kh in range(NSA_KVH):
        valid = valid_m & (head_kv == kh)
        s = jnp.where(valid, _dot_nt(q, tokens(0, kh)) * scale, NEG)
        e = jnp.exp(s - jnp.max(s, axis=-1, keepdims=True))
        prob = jnp.where(valid, e / jnp.sum(e, axis=-1, keepdims=True), 0.0)
        o_cmp = o_cmp + _dot(prob.astype(BF16), tokens(1, kh))
        psum_rows.append(jnp.sum(prob, axis=0, keepdims=True))
    o_ref[0] = o_cmp
    psum = jnp.concatenate(psum_rows + [jnp.zeros((8 - NSA_KVH, n_seg), F32)], axis=0)
    m_r = lax.broadcasted_iota(jnp.int32, (n_seg, 1), 0)
    j = lax.broadcasted_iota(jnp.int32, (1, nsp), 1)
    ov = ((m_r * CMP_STRIDE <= j * SLC_BLOCK + SLC_BLOCK - 1) & (m_r * CMP_STRIDE + CMP_BLOCK - 1 >= j * SLC_BLOCK)
          & (m_r < n_seg - CMP_RATIO + 1) & (j < n_slc)).astype(F32)
    imp = _dot_hi(psum, ov)
    cur = pos // SLC_BLOCK
    forced = (j == 0) | (j == cur) | (j == cur - 1)
    score = jnp.where(j * SLC_BLOCK <= pos, imp + FORCE_BONUS * forced.astype(F32), -1.0)
    score = jnp.where(j < n_slc, score, -2.0)
    jf = j.astype(F32)
    slot = lax.broadcasted_iota(jnp.int32, (1, LANE), 1)
    picked = jnp.zeros((8, LANE), F32)
    for it in range(min(SLC_TOP_N, n_slc)):
        best = jnp.max(score, axis=-1, keepdims=True)
        ix = jnp.min(jnp.where(score == best, jf, float(nsp)), axis=-1, keepdims=True)
        picked = jnp.where(slot == it, ix, picked)
        score = jnp.where(jf == ix, -3.0, score)
    idx_ref[0] = picked.astype(jnp.int32)


def nsa_dec_cmp(part, w1, pe, w2, q_raw, n_seg, n_slc, pos):
    B = part.shape[0]
    R = n_seg * TILE_ROWS
    full = lambda shape: pl.BlockSpec(shape, lambda b: (0,) * len(shape))
    return pl.pallas_call(
        functools.partial(_nsa_dec_cmp_body, n_seg=n_seg, n_slc=n_slc, pos=pos),
        grid=(B,),
        in_specs=[pl.BlockSpec((1, CMP_RATIO, R, HD), lambda b: (b, 0, 0, 0)),
                  full((2, CMP_BLOCK, HD, HD)), full((2, CMP_BLOCK, HD)), full((2, HD, HD)),
                  pl.BlockSpec((1, NSA_HEADS, HD), lambda b: (b, 0, 0))],
        out_specs=[pl.BlockSpec((1, NSA_HEADS, HD), lambda b: (b, 0, 0)),
                   pl.BlockSpec((1, 8, LANE), lambda b: (b, 0, 0))],
        out_shape=[jax.ShapeDtypeStruct((B, NSA_HEADS, HD), F32), jax.ShapeDtypeStruct((B, 8, LANE), jnp.int32)],
        compiler_params=_cparams(("parallel",)),
        name="nsa_dec_cmp",
    )(part, w1, pe, w2, q_raw)


def _nsa_dec_attn_body(pt_ref, idx_ref, *refs, n_cache_blocks, nsel, win_skip):
    blk_refs = refs[:nsel]
    q_ref, new_ref, win_ref, ocmp_ref, gate_ref, o_ref = refs[nsel:]
    b = pl.program_id(0)
    kh = pl.program_id(1)
    scale = HD ** -0.5
    q = q_ref[0, 0].astype(BF16)
    new = new_ref[0, 0]
    s_new = _dot_nt(q, new.astype(BF16)) * scale

    def attend(x, msk, s_own, v_own):
        s = [jnp.where(m, _dot_nt(q, xi) * scale, NEG) for xi, m in zip(x, msk)]
        mx = s_own
        for si in s:
            mx = jnp.maximum(mx, jnp.max(si, axis=-1, keepdims=True))
        p_own = jnp.exp(s_own - mx)
        num = p_own * v_own
        den = p_own
        for si, xi in zip(s, x):
            p = jnp.exp(si - mx)
            num = num + _dot(pltpu.roll(p, NSA_KVH, 1).astype(BF16), xi)
            den = den + jnp.sum(p, axis=-1, keepdims=True)
        return num / den

    col_c = lax.broadcasted_iota(jnp.int32, (1, SLC_BLOCK * TILE_ROWS), 1) % TILE_ROWS
    xs = [r[...].reshape(SLC_BLOCK * TILE_ROWS, HD).astype(BF16) for r in blk_refs]
    ms = [(col_c == kh) & (idx_ref[b, kh, n] < n_cache_blocks) for n in range(nsel)]
    o_slc = attend(xs, ms, s_new[:, 0:1], new[1:2])

    n_win = win_ref.shape[1]
    xw = win_ref[0].reshape(n_win * TILE_ROWS, HD).astype(BF16)
    colw = lax.broadcasted_iota(jnp.int32, (1, n_win * TILE_ROWS), 1)
    mw = ((colw % TILE_ROWS) == kh) & ((colw // TILE_ROWS) >= win_skip)
    o_win = attend([xw], [mw], s_new[:, 2:3], new[3:4])

    gate = _sigmoid(gate_ref[0, 0])
    lane = lax.broadcasted_iota(jnp.int32, (1, LANE), 1)
    row = lax.broadcasted_iota(jnp.int32, (8, 1), 0)
    pick = lambda c: jnp.sum(jnp.where(lane == 3 * row + c, gate, 0.0), axis=-1, keepdims=True)
    o_ref[0, 0] = pick(0) * ocmp_ref[0, 0] + pick(1) * o_slc + pick(2) * o_win


def nsa_dec_attn(cache5, win4, page_table, idx, q_rot, new_rows, o_cmp, gates, a, n_cache_blocks, win_skip):
    B, _, nsel = idx.shape
    n_win = win4.shape[1]

    def blk_spec(n):
        def blk_map(b, k, pt, ix):
            j = jnp.minimum(ix[b, k, n], n_cache_blocks - 1)
            return (pt[b, j // 2] * 2 + j % 2, 0, 2 * a + 1, 0, 0)
        return pl.BlockSpec((None, SLC_BLOCK, None, TILE_ROWS, HD), blk_map)

    grp = pl.BlockSpec((1, 1, 8, HD), lambda b, k, pt, ix: (b, k, 0, 0))
    grid_spec = pltpu.PrefetchScalarGridSpec(
        num_scalar_prefetch=2,
        grid=(B, NSA_KVH),
        in_specs=[blk_spec(n) for n in range(nsel)] + [
            grp, grp, pl.BlockSpec((1, n_win, TILE_ROWS, HD), lambda b, k, pt, ix: (a * B + b, 0, 0, 0)), grp, grp],
        out_specs=grp)
    return pl.pallas_call(
        functools.partial(_nsa_dec_attn_body, n_cache_blocks=n_cache_blocks, nsel=nsel, win_skip=win_skip),
        grid_spec=grid_spec,
        out_shape=jax.ShapeDtypeStruct((B, NSA_KVH, 8, HD), F32),
        compiler_params=_cparams(("parallel", "arbitrary")),
        name="nsa_dec_attn",
    )(page_table, idx, *([cache5] * nsel), q_rot, new_rows, win4, o_cmp, gates)


def _group_rows(x, B):
    x = x.reshape(B, NSA_KVH, NSA_G, HD)
    return jnp.pad(x, ((0, 0), (0, 0), (0, 8 - NSA_G), (0, 0)))


def nsa_layer_decode(x, g, w_in, w1, pe, w2, w_out, cache, win_state, page_table, a, B):
    R = x.shape[0]
    n_pool = cache.shape[0]
    n_pages = page_table.shape[1]
    past = n_pages * PAGE_SIZE
    n_seg = (past + 1) // CMP_STRIDE
    n_slc = -(-(past + 1) // SLC_BLOCK)
    W = NSA_KVH * HD
    z = proj(x, g, jnp.swapaxes(w_in, 1, 2), n_out=NSA_MAIN, layer=a, w_rows_out=True)
    gates = proj(x, g, _nsa_gate_weight(w_in, a))
    rot = nsa_rope(z, rope_tables(jnp.full((R,), past, jnp.int32)), R)
    part = nsa_compress_paged(cache.reshape(n_pool * PAGE_SIZE, 4, TILE_ROWS, HD), page_table, w1, a)
    o_cmp, idx = nsa_dec_cmp(part, w1, pe, w2, z[:B, :NSA_QW].reshape(B, NSA_HEADS, HD), n_seg, n_slc, past)
    nsel = min(SLC_TOP_N, n_slc)
    kv = z[:B, NSA_QW:].reshape(B, 6, NSA_KVH, HD)
    ks_new = rot[:B, NSA_QW:NSA_QW + W].reshape(B, NSA_KVH, HD)
    kw_new = rot[:B, NSA_QW + W:].reshape(B, NSA_KVH, HD)
    new_rows = jnp.stack([ks_new, kv[:, 3], kw_new, kv[:, 5]], axis=2)
    new_rows = jnp.pad(new_rows, ((0, 0), (0, 0), (0, 4), (0, 0)))
    n_win = win_state.shape[2]
    mix = nsa_dec_attn(cache.reshape(n_pool * 2, SLC_BLOCK, 4, TILE_ROWS, HD),
                       win_state.reshape(-1, n_win, TILE_ROWS, HD), page_table, idx[:, :NSA_KVH, :nsel],
                       _group_rows(rot[:B, :NSA_QW], B), new_rows, _group_rows(o_cmp.reshape(B, NSA_QW), B),
                       jnp.broadcast_to(gates[:B].reshape(B, NSA_KVH, 1, LANE), (B, NSA_KVH, 8, LANE)),
                       a, past // SLC_BLOCK, max(n_win + 1 - WINDOW, 0))
    mix = jnp.pad(mix[:, :, :NSA_G].reshape(B, NSA_QW), ((0, R - B), (0, 0)))
    y = proj(mix, g, w_out, residual=x, norm=False, layer=a)
    rows = jnp.stack([kv[:, 0], kv[:, 1], ks_new, kv[:, 3]], axis=1)[:, None]
    win_new = jnp.stack([kw_new, kv[:, 5]], axis=1)[:, None]
    buf = jnp.concatenate([win_state[a], win_new], axis=1)
    return y, rows, buf[:, -min(WINDOW, n_win + 1):]


def kernel(x_prompt, x_sample, cache_nsa_kv, state_nsa_win, state_hgrn, state_mlstm_c, state_mlstm_n, state_mlstm_m, state_ffn_conv, page_table, norm_mix, norm_ffn, norm_out, nsa_w_in, nsa_cmp_w1, nsa_cmp_pe, nsa_cmp_w2, nsa_w_out, hgrn_w_in, hgrn_lb, hgrn_norm, hgrn_w_out, ml_w_in, ml_b_gate, ml_w_out, ffn_w_up, ffn_conv_w, ffn_conv_b, ffn_w_down):
    P = dict(norm_mix=norm_mix, norm_ffn=norm_ffn, norm_out=norm_out, nsa_w_in=nsa_w_in, nsa_cmp_w1=nsa_cmp_w1,
             nsa_cmp_pe=nsa_cmp_pe, nsa_cmp_w2=nsa_cmp_w2, nsa_w_out=nsa_w_out, hgrn_w_in=hgrn_w_in,
             hgrn_lb=hgrn_lb, hgrn_norm=hgrn_norm, hgrn_w_out=hgrn_w_out, ml_w_in=ml_w_in, ml_b_gate=ml_b_gate,
             ml_w_out=ml_w_out, ffn_w_up=ffn_w_up, ffn_conv_w=ffn_conv_w, ffn_conv_b=ffn_conv_b,
             ffn_w_down=ffn_w_down)
    past = dict(cache=cache_nsa_kv, win=state_nsa_win, hgrn=state_hgrn, c=state_mlstm_c, n=state_mlstm_n,
                m=state_mlstm_m, ffn=state_ffn_conv, page_table=page_table)
    yp, sp = _trunk(x_prompt, P, None)
    ys, ss = _trunk(x_sample, P, past)
    return (yp, ys, sp['nsa_kv'], ss['nsa_kv'], sp['nsa_win'], ss['nsa_win'], sp['hgrn'], ss['hgrn'],
            sp['c'], ss['c'], sp['n'], ss['n'], sp['m'], ss['m'], sp['ffn'], ss['ffn'])


DEC_ROWS = 16


def _spread(z, B, T):
    return jnp.zeros((B, T, z.shape[1]), F32).at[:, 0].set(z[:B]).reshape(B * T, z.shape[1])


def _gather_first(o, B, T, R):
    return jnp.pad(o.reshape(B, T, -1)[:, 0], ((0, R - B), (0, 0)))


def _trunk(x_in, P, past):
    B, T, D = x_in.shape
    decode = past is not None
    if decode:
        R = DEC_ROWS
        x = jnp.pad(x_in.reshape(B, D), ((0, R - B), (0, 0)))
        Ts = DEC_ROWS
    else:
        x = x_in.reshape(B * T, D)
        tables = rope_tables(jnp.arange(T, dtype=jnp.int32))
    rows_l, win_l, hg_l, c_l, n_l, m_l, ffn_l = [], [], [], [], [], [], []
    for i in range(DEPTH):
        kind, a = i % N_MIXERS, i // N_MIXERS
        g = P['norm_mix'][i]
        if kind == 0:
            args = (x, g, P['nsa_w_in'], P['nsa_cmp_w1'][a], P['nsa_cmp_pe'][a], P['nsa_cmp_w2'][a], P['nsa_w_out'])
            if decode:
                x, rows, win = nsa_layer_decode(*args, past['cache'], past['win'], past['page_table'], a, B)
                rows_l.append(rows)
            else:
                x, rows, win = nsa_layer_prefill(*args, a, tables, B, T)
                rows_l.append(rows)
            win_l.append(win)
        elif kind == 1:
            z = proj(x, g, P['hgrn_w_in'], layer=a)
            if decode:
                o, S = hgrn_scan(_spread(z, B, Ts), P['hgrn_lb'], i, P['hgrn_norm'][a], past['hgrn'][a], B, Ts, 1)
                o = _gather_first(o, B, Ts, R)
            else:
                o, S = hgrn_scan(z, P['hgrn_lb'], i, P['hgrn_norm'][a], jnp.zeros((B, HG_HEADS, HD, HD), F32), B, T)
            x = proj(o, g, P['hgrn_w_out'], residual=x, norm=False, layer=a)
            hg_l.append(S)
        else:
            w = P['ml_w_in']
            z = proj(x, g, jnp.swapaxes(w, 1, 2), n_out=ML_MAIN, layer=a, w_rows_out=True)
            gt = proj(x, g, jnp.pad(w[a, :, ML_MAIN:], ((0, 0), (0, LANE - 2 * ML_HEADS))))
            if decode:
                o, C, nv, m = mlstm_scan(_spread(z, B, Ts), _spread(gt, B, Ts), P['ml_b_gate'][a],
                                         past['c'][a], past['n'][a], past['m'][a], B, Ts, 1)
                o = _gather_first(o, B, Ts, R)
            else:
                o, C, nv, m = mlstm_scan(z, gt, P['ml_b_gate'][a], jnp.zeros((B, ML_HEADS, ML_DK, ML_DV), F32),
                                         jnp.zeros((B, ML_HEADS, ML_DK), F32), jnp.zeros((B, ML_HEADS), F32), B, T)
            x = proj(o, g, P['ml_w_out'], residual=x, norm=False, layer=a)
            c_l.append(C)
            n_l.append(nv)
            m_l.append(m)
        fargs = (P['norm_ffn'][i], P['ffn_w_up'], P['ffn_conv_w'], P['ffn_conv_b'], P['ffn_w_down'], i,
                 P['norm_out'])
        last = i == DEPTH - 1
        if decode:
            buf = past['ffn'][i]
            bufp = jnp.pad(buf.transpose(1, 0, 2), ((0, 0), (0, R - B), (0, 0)))
            y, sa = conv_ffn(x[None], *fargs, buf=bufp, final_norm=last)
            x = y[0]
            ffn_l.append(jnp.stack([buf[:, 1], sa[0, :B]], axis=1))
        else:
            y, sa = conv_ffn(x.reshape(B, T, D), *fargs, final_norm=last)
            x = y.reshape(B * T, D)
            ffn_l.append(sa[:, 6:8])
    out = x[:B].reshape(B, 1, D) if decode else x.reshape(B, T, D)
    if decode:
        nsa_kv = jnp.stack(rows_l, axis=2)
    else:
        nsa_kv = nsa_rows([z for z, _ in rows_l], [rot for _, rot in rows_l], T)
        nsa_kv = nsa_kv.reshape(B, T, len(rows_l), 4, NSA_KVH, HD)
    states = dict(nsa_kv=nsa_kv, nsa_win=jnp.stack(win_l, axis=0), hgrn=jnp.stack(hg_l, axis=0),
                  c=jnp.stack(c_l, axis=0), n=jnp.stack(n_l, axis=0), m=jnp.stack(m_l, axis=0),
                  ffn=jnp.stack(ffn_l, axis=0))
    return out, states
```

```python
import functools
import math

import jax
import jax.numpy as jnp
from jax import lax
from jax.experimental import pallas as pl
from jax.experimental.pallas import tpu as pltpu

F32 = jnp.float32
BF16 = jnp.bfloat16

D_MODEL = 2048
DEPTH = 4
PAGE_SIZE = 128
N_MIXERS = 3

HD = 128
NSA_HEADS = D_MODEL // HD
NSA_KVH = 4
NSA_G = NSA_HEADS // NSA_KVH
NSA_QW = NSA_HEADS * HD
NSA_KVW = 6 * NSA_KVH * HD
NSA_MAIN = NSA_QW + NSA_KVW
CMP_STRIDE = 16
CMP_RATIO = 2
CMP_BLOCK = CMP_STRIDE * CMP_RATIO
SLC_BLOCK = 64
SLC_TOP_N = 16
FORCE_BONUS = 1000.0
WINDOW = 512
ROPE_THETA = 500000.0
ROPE_DIM = HD // 4
ROPE_HALF = ROPE_DIM // 2

HG_HEADS = D_MODEL // HD
HG_W = HG_HEADS * HD

ML_HEADS = 8
ML_DK = D_MODEL // (2 * ML_HEADS)
ML_DV = D_MODEL // ML_HEADS
ML_QK = ML_HEADS * ML_DK
ML_VW = ML_HEADS * ML_DV
ML_MAIN = 2 * ML_QK + 2 * ML_VW
ML_GATE_CAP = 15.0
CHUNK = 64
SUB = 16
HG_HEADS_PER_STEP = 4
ML_HEADS_PER_STEP = 8

D_FF = (11 * D_MODEL) // 4
EPS = 1e-6
NEG = -1e30

LANE = 128
VMEM_LIMIT = 56 * 1024 * 1024
FFN_VMEM_LIMIT = 62 * 1024 * 1024


def _cparams(sem, vmem_limit=VMEM_LIMIT):
    return pltpu.CompilerParams(dimension_semantics=sem, vmem_limit_bytes=vmem_limit)


def _dot(a, b):
    return jnp.dot(a, b, preferred_element_type=F32)


def _dot_nt(a, b):
    return lax.dot_general(a, b, (((1,), (1,)), ((), ())), preferred_element_type=F32)


def _dot_tn(a, b):
    return lax.dot_general(a, b, (((0,), (0,)), ((), ())), preferred_element_type=F32)


def _dot_hi(a, b):
    return jnp.dot(a, b, preferred_element_type=F32, precision=lax.Precision.HIGHEST)


def _rms(x, g):
    return x * lax.rsqrt(jnp.mean(x * x, axis=-1, keepdims=True) + EPS) * g


def _sigmoid(x):
    return 1.0 / (1.0 + jnp.exp(-x))


def _pick_tile(n, cands):
    for c in cands:
        if n % c == 0:
            return c
    return n


def _proj_body(*refs, norm, residual, w_rows_out):
    if residual:
        x_ref, g_ref, w_ref, r_ref, o_ref, xn_ref = refs
    else:
        x_ref, g_ref, w_ref, o_ref, xn_ref = refs

    @pl.when(pl.program_id(1) == 0)
    def _():
        x = x_ref[...]
        if norm:
            x = _rms(x, g_ref[...])
        xn_ref[...] = x.astype(BF16)

    w = w_ref[...].astype(BF16)
    acc = _dot_nt(xn_ref[...], w) if w_rows_out else _dot(xn_ref[...], w)
    if residual:
        acc = acc + r_ref[...]
    o_ref[...] = acc


def proj(x, g, w, residual=None, norm=True, n_out=None, layer=None, w_rows_out=False):
    M, K = x.shape
    N = w.shape[1 if w_rows_out else -1] if n_out is None else n_out
    tm = _pick_tile(M, (1024, 512, 256, 128, 64, 32, 16))
    tn = _pick_tile(N, (512, 384, 256, 128))
    if w_rows_out:
        wspec = pl.BlockSpec((None, tn, K), lambda i, j: (layer, j, 0))
    elif layer is None:
        wspec = pl.BlockSpec((K, tn), lambda i, j: (0, j))
    else:
        wspec = pl.BlockSpec((None, K, tn), lambda i, j: (layer, 0, j))
    in_specs = [pl.BlockSpec((tm, K), lambda i, j: (i, 0)),
                pl.BlockSpec((1, K), lambda i, j: (0, 0)),
                wspec]
    args = [x, g.reshape(1, K), w]
    if residual is not None:
        in_specs.append(pl.BlockSpec((tm, tn), lambda i, j: (i, j)))
        args.append(residual)
    return pl.pallas_call(
        functools.partial(_proj_body, norm=norm, residual=residual is not None, w_rows_out=w_rows_out),
        grid=(M // tm, N // tn),
        in_specs=in_specs,
        out_specs=pl.BlockSpec((tm, tn), lambda i, j: (i, j)),
        out_shape=jax.ShapeDtypeStruct((M, N), F32),
        scratch_shapes=[pltpu.VMEM((tm, K), BF16)],
        compiler_params=_cparams(("parallel", "arbitrary")),
        name="proj",
    )(*args)


def _ffn_body(x_ref, g_ref, wa_ref, wu_ref, cw_ref, cb_ref, wd_ref, go_ref, b0_ref, b1_ref,
              y_ref, sa_ref, xn_ref, carry_ref, *, rows_are_batch, final_norm, nf):
    t = pl.program_id(0)
    f = pl.program_id(1)
    S, tm = x_ref.shape[0], x_ref.shape[1]

    @pl.when(f == 0)
    def _():
        for s in range(S):
            xn_ref[s] = _rms(x_ref[s], g_ref[...]).astype(BF16)
            y_ref[s] = x_ref[s]

    wa = wa_ref[...].astype(BF16)
    wu = wu_ref[...].astype(BF16)
    wd = wd_ref[...].astype(BF16)
    cw = cw_ref[...]
    cb = cb_ref[...]
    row = lax.broadcasted_iota(jnp.int32, (tm, 1), 0)
    for s in range(S):
        xn = xn_ref[s]
        a = _dot(xn, wa)
        u = _dot(xn, wu)
        if rows_are_batch:
            am2 = b0_ref[...]
            am1 = b1_ref[...]
        else:
            prev = jnp.where(t == 0, 0.0, carry_ref[f, s])
            p1 = prev[7:8]
            p2 = prev[6:7]
            am1 = jnp.where(row >= 1, pltpu.roll(a, 1, 0), p1)
            am2 = jnp.where(row >= 2, pltpu.roll(a, 2, 0), jnp.where(row == 1, p1, p2))
            carry_ref[f, s] = a[tm - 8:tm]
        conv = cb + am2 * cw[0:1] + am1 * cw[1:2] + a * cw[2:3]
        gt = conv * _sigmoid(conv) * u
        y_ref[s] += _dot(gt.astype(BF16), wd)
        sa_ref[s, 0] = a[0:8] if rows_are_batch else a[tm - 8:tm]

    if final_norm:
        @pl.when(f == nf - 1)
        def _():
            for s in range(S):
                y_ref[s] = _rms(y_ref[s], go_ref[...])


def conv_ffn(x, g, w_up, conv_w, conv_b, w_down, layer, g_out, buf=None, final_norm=False):
    B, T, D = x.shape
    F = w_down.shape[1]
    rows_are_batch = buf is not None
    tm = _pick_tile(T, (512, 256, 128, 64, 32, 16, 8))
    tf = _pick_tile(F, (512, 256, 128))
    nf = F // tf
    if buf is None:
        b0 = b1 = jnp.zeros((8, F), F32)
        bspec = pl.BlockSpec((8, tf), lambda t, f: (0, f))
    else:
        b0, b1 = buf[0], buf[1]
        bspec = pl.BlockSpec((tm, tf), lambda t, f: (0, f))
    y, sa = pl.pallas_call(
        functools.partial(_ffn_body, rows_are_batch=rows_are_batch, final_norm=final_norm, nf=nf),
        grid=(T // tm, nf),
        in_specs=[pl.BlockSpec((B, tm, D), lambda t, f: (0, t, 0), pipeline_mode=pl.Buffered(1)),
                  pl.BlockSpec((1, D), lambda t, f: (0, 0)),
                  pl.BlockSpec((None, D, tf), lambda t, f: (layer, 0, f)),
                  pl.BlockSpec((None, D, tf), lambda t, f: (layer, 0, nf + f)),
                  pl.BlockSpec((None, 3, tf), lambda t, f: (layer, 0, f)),
                  pl.BlockSpec((None, 1, tf), lambda t, f: (layer, 0, f)),
                  pl.BlockSpec((None, tf, D), lambda t, f: (layer, f, 0)),
                  pl.BlockSpec((1, D), lambda t, f: (0, 0)),
                  bspec, bspec],
        out_specs=[pl.BlockSpec((B, tm, D), lambda t, f: (0, t, 0), pipeline_mode=pl.Buffered(1)),
                   pl.BlockSpec((B, 1, 8, tf), lambda t, f: (0, t, 0, f))],
        out_shape=[jax.ShapeDtypeStruct((B, T, D), F32), jax.ShapeDtypeStruct((B, T // tm, 8, F), F32)],
        scratch_shapes=[pltpu.VMEM((B, tm, D), BF16), pltpu.VMEM((nf, B, 8, tf), F32)],
        compiler_params=_cparams(("arbitrary", "arbitrary"), FFN_VMEM_LIMIT),
        name="conv_ffn",
    )(x, g.reshape(1, D), w_up, w_up, conv_w, conv_b[:, None, :], w_down, g_out.reshape(1, D), b0, b1)
    return y, sa[:, -1]


def rope_tables(pos):
    inv = ROPE_THETA ** (-jnp.arange(ROPE_HALF, dtype=F32) / ROPE_HALF)
    ang = pos.astype(F32)[:, None] * inv[None, :]
    c, s = jnp.cos(ang), jnp.sin(ang)
    T = pos.shape[0]
    one = jnp.ones((T, HD - ROPE_DIM), F32)
    zero = jnp.zeros((T, HD - ROPE_DIM), F32)
    zh = jnp.zeros((T, ROPE_HALF), F32)
    cos = jnp.concatenate([c, c, one], axis=1)
    sin_lo = jnp.concatenate([-s, zh, zero], axis=1)
    sin_hi = jnp.concatenate([zh, s, zero], axis=1)
    return cos, sin_lo, sin_hi


def _rope_body(q_ref, ks_ref, kw_ref, c_ref, sl_ref, sh_ref, o_ref):
    c, sl, sh = c_ref[...], sl_ref[...], sh_ref[...]
    h_out = 0
    for x_ref in (q_ref, ks_ref, kw_ref):
        for h in range(x_ref.shape[1] // HD):
            x = x_ref[:, h * HD:(h + 1) * HD]
            o_ref[:, h_out * HD:(h_out + 1) * HD] = (x * c + pltpu.roll(x, HD - ROPE_HALF, 1) * sl
                                                     + pltpu.roll(x, ROPE_HALF, 1) * sh)
            h_out += 1


def nsa_rope(z, tables, T):
    M = z.shape[0]
    tm = _pick_tile(T, (512, 256, 128, 64, 32, 16, 8))
    nt = T // tm
    W = NSA_KVH * HD
    tspec = pl.BlockSpec((tm, HD), lambda i: (i % nt, 0))
    return pl.pallas_call(
        _rope_body,
        grid=(M // tm,),
        in_specs=[pl.BlockSpec((tm, NSA_QW), lambda i: (i, 0)),
                  pl.BlockSpec((tm, W), lambda i: (i, (NSA_QW + 2 * W) // W)),
                  pl.BlockSpec((tm, W), lambda i: (i, (NSA_QW + 4 * W) // W)),
                  tspec, tspec, tspec],
        out_specs=pl.BlockSpec((tm, NSA_QW + 2 * W), lambda i: (i, 0)),
        out_shape=jax.ShapeDtypeStruct((M, NSA_QW + 2 * W), F32),
        compiler_params=_cparams(("parallel",)),
        name="nsa_rope",
    )(z, z, z, *tables)


def _gelu(x):
    return 0.5 * x * (1.0 + jnp.tanh(math.sqrt(2.0 / math.pi) * (x + 0.044715 * x * x * x)))


def _pe_hidden(pe_ref, w1_ref):
    acc = jnp.zeros((8, HD), F32)
    for l in range(CMP_BLOCK):
        row = jnp.broadcast_to(pe_ref[0, l:l + 1, :], (8, HD))
        acc = acc + _dot(row.astype(BF16), w1_ref[0, l].astype(BF16))
    return acc[0:1]


def _cmp_body(x_ref, w1_ref, pe_ref, w2_ref, o_ref, *, nseg):
    parts = []
    for r in range(CMP_RATIO):
        acc = jnp.zeros((nseg, HD), F32)
        for j in range(CMP_STRIDE):
            xj = x_ref[pl.ds(j, nseg, stride=CMP_STRIDE), :]
            acc = acc + _dot(xj.astype(BF16), w1_ref[0, CMP_STRIDE * r + j].astype(BF16))
        parts.append(acc)
    hid = _pe_hidden(pe_ref, w1_ref) + parts[0] + pltpu.roll(parts[1], nseg - 1, 0)
    o_ref[0, 0, 0] = _dot(_gelu(hid).astype(BF16), w2_ref[0].astype(BF16))


def nsa_compress(z, w1, pe, w2, B, T):
    nseg = T // CMP_STRIDE
    return pl.pallas_call(
        functools.partial(_cmp_body, nseg=nseg),
        grid=(2, B, NSA_KVH),
        in_specs=[pl.BlockSpec((T, HD), lambda s, b, k: (b, NSA_QW // HD + NSA_KVH * s + k)),
                  pl.BlockSpec((1, CMP_BLOCK, HD, HD), lambda s, b, k: (s, 0, 0, 0)),
                  pl.BlockSpec((1, CMP_BLOCK, HD), lambda s, b, k: (s, 0, 0)),
                  pl.BlockSpec((1, HD, HD), lambda s, b, k: (s, 0, 0))],
        out_specs=pl.BlockSpec((1, 1, 1, nseg, HD), lambda s, b, k: (s, b, k, 0, 0)),
        out_shape=jax.ShapeDtypeStruct((2, B, NSA_KVH, nseg, HD), F32),
        compiler_params=_cparams(("parallel", "parallel", "parallel")),
        name="nsa_compress",
    )(z, w1, pe, w2)


def _overlap_t(nsp, nseg, n_slc):
    j = lax.broadcasted_iota(jnp.int32, (nsp, 1), 0)
    m = lax.broadcasted_iota(jnp.int32, (1, nseg), 1)
    ov = ((m * CMP_STRIDE <= j * SLC_BLOCK + SLC_BLOCK - 1) & (m * CMP_STRIDE + CMP_BLOCK - 1 >= j * SLC_BLOCK)
          & (m < nseg - CMP_RATIO + 1) & (j < n_slc))
    return ov.astype(F32)


def _rank_rows(score, n):
    j = lax.broadcasted_iota(jnp.int32, (score.shape[0], 1), 0)
    rank = jnp.zeros(score.shape, F32)
    for i in range(n):
        si = score[i:i + 1, :]
        rank = rank + ((si > score) | ((si == score) & (i < j))).astype(F32)
    return rank


def _nsa_attn_body(qraw_ref, qrot_ref, kc_ref, vc_ref, ks_ref, vs_ref, kw_ref, vw_ref, gate_ref, o_ref,
                   ksb_ref, vsb_ref, kwb_ref, vwb_ref, *, T, tq, tk):
    qi = pl.program_id(2)
    t0 = qi * tq
    scale = HD ** -0.5
    nseg = T // CMP_STRIDE
    n_slc = T // SLC_BLOCK
    nsp = max(8, n_slc)
    pos = t0 + lax.broadcasted_iota(jnp.int32, (tq, 1), 0)
    pos_row = t0 + lax.broadcasted_iota(jnp.int32, (1, tq), 1)

    @pl.when(qi == 0)
    def _():
        ksb_ref[:, :HD] = ks_ref[...].astype(BF16)
        ksb_ref[:, HD:] = (lax.broadcasted_iota(jnp.int32, (T, 1), 0) // SLC_BLOCK
                           == lax.broadcasted_iota(jnp.int32, (1, LANE), 1)).astype(BF16)
        for dst, src in zip((vsb_ref, kwb_ref, vwb_ref), (vs_ref, kw_ref, vw_ref)):
            dst[...] = src[...].astype(BF16)

    def heads_on_rows(ref):
        return jnp.concatenate([ref[:, g * HD:(g + 1) * HD] for g in range(NSA_G)], axis=0)

    def per_head(x):
        return jnp.concatenate([x] * NSA_G, axis=0)

    q_raw = (heads_on_rows(qraw_ref) * scale).astype(BF16)
    q_rot = (heads_on_rows(qrot_ref) * scale).astype(BF16)

    kc = kc_ref[0, 0, 0].astype(BF16)
    vc = vc_ref[0, 0, 0].astype(BF16)
    m_idx = lax.broadcasted_iota(jnp.int32, (1, nseg), 1)
    valid = (m_idx * CMP_STRIDE + CMP_BLOCK - 1 <= pos) & (m_idx < nseg - CMP_RATIO + 1)
    s = _dot_nt(q_raw, kc) + per_head(jnp.where(valid, 0.0, NEG))
    e = jnp.exp(s - jnp.max(s, axis=-1, keepdims=True))
    p = e * (1.0 / jnp.sum(e, axis=-1, keepdims=True)) * per_head(valid.astype(F32))
    o_cmp = _dot(p.astype(BF16), vc)
    psum = p[0:tq]
    for g in range(1, NSA_G):
        psum = psum + p[g * tq:(g + 1) * tq]

    imp_t = lax.dot_general(_overlap_t(nsp, nseg, n_slc), psum, (((1,), (1,)), ((), ())),
                            preferred_element_type=F32, precision=lax.Precision.HIGHEST)
    j = lax.broadcasted_iota(jnp.int32, (nsp, 1), 0)
    cur = pos_row // SLC_BLOCK
    forced = (j == 0) | (j == cur) | (j == cur - 1)
    score = jnp.where(j * SLC_BLOCK <= pos_row, imp_t + FORCE_BONUS * forced.astype(F32), -1.0)
    score = jnp.where(j < n_slc, score, -2.0)
    sel_t = ((_rank_rows(score, n_slc) < min(SLC_TOP_N, n_slc)) & (j < n_slc)).astype(F32)
    if nsp < LANE:
        sel_t = jnp.concatenate([sel_t, jnp.zeros((LANE - nsp, tq), F32)], axis=0)
    sel = sel_t.T
    def update(s, v, carry):
        m_old, l_old, acc = carry
        m_new = jnp.maximum(m_old, jnp.max(s, axis=-1, keepdims=True))
        alpha = jnp.exp(m_old - m_new)
        p = jnp.exp(s - m_new)
        return (m_new, alpha * l_old + jnp.sum(p, axis=-1, keepdims=True), alpha * acc + _dot(p.astype(BF16), v))

    R = NSA_G * tq
    init = (jnp.full((R, 1), NEG, F32), jnp.zeros((R, 1), F32), jnp.zeros((R, HD), F32))

    q_aug = jnp.concatenate([q_rot, per_head(((sel - 1.0) * -NEG).astype(BF16))], axis=1)

    def slc_scores(kt):
        k0 = pl.multiple_of(kt * tk, tk)
        return _dot_nt(q_aug, ksb_ref[pl.ds(k0, tk), :]), vsb_ref[pl.ds(k0, tk), :]

    def slc_step(kt, carry):
        s, v = slc_scores(kt)
        return update(s, v, carry)

    kt_diag = t0 // tk
    slc = lax.fori_loop(0, kt_diag, slc_step, init)
    s, v = slc_scores(kt_diag)
    kpos = kt_diag * tk + lax.broadcasted_iota(jnp.int32, (1, tk), 1)
    slc = update(s + per_head(jnp.where(kpos <= pos, 0.0, NEG)), v, slc)

    nkw = min(WINDOW + tq, T)
    k0w = pl.multiple_of(jnp.clip(t0 - WINDOW, 0, T - nkw), tq)
    d = pos - (k0w + lax.broadcasted_iota(jnp.int32, (1, nkw), 1))
    win = update(_dot_nt(q_rot, kwb_ref[pl.ds(k0w, nkw), :]) + per_head(jnp.where((d >= 0) & (d < WINDOW), 0.0, NEG)),
                 vwb_ref[pl.ds(k0w, nkw), :], init)

    gate = _sigmoid(gate_ref[...])
    o_slc = slc[2] / slc[1]
    o_win = win[2] / win[1]
    for g in range(NSA_G):
        rows = slice(g * tq, (g + 1) * tq)
        o_ref[:, g * HD:(g + 1) * HD] = (gate[:, 3 * g:3 * g + 1] * o_cmp[rows]
                                         + gate[:, 3 * g + 1:3 * g + 2] * o_slc[rows]
                                         + gate[:, 3 * g + 2:3 * g + 3] * o_win[rows])


def nsa_attn_prefill(z, rot, cmp, gates, B, T):
    assert T % LANE == 0 and T // SLC_BLOCK <= LANE, "selection blocks must fit one lane group"
    tq = LANE
    tk = _pick_tile(T, (1024, 512, 256, 128))
    nq = T // tq
    nseg = T // CMP_STRIDE
    W = NSA_G * HD
    qspec = pl.BlockSpec((tq, W), lambda b, k, q: (b * nq + q, k))
    cspec = lambda s: pl.BlockSpec((1, 1, 1, nseg, HD), lambda b, k, q: (s, b, k, 0, 0))
    kvspec = lambda c0: pl.BlockSpec((T, HD), lambda b, k, q: (b, c0 + k))
    return pl.pallas_call(
        functools.partial(_nsa_attn_body, T=T, tq=tq, tk=tk),
        grid=(B, NSA_KVH, nq),
        in_specs=[qspec, qspec, cspec(0), cspec(1),
                  kvspec(NSA_QW // HD), kvspec((NSA_QW + 3 * W) // HD),
                  kvspec((NSA_QW + W) // HD), kvspec((NSA_QW + 5 * W) // HD),
                  pl.BlockSpec((tq, LANE), lambda b, k, q: (b * nq + q, k))],
        out_specs=qspec,
        out_shape=jax.ShapeDtypeStruct((B * T, NSA_QW), F32),
        scratch_shapes=[pltpu.VMEM((T, 2 * HD), BF16)] + [pltpu.VMEM((T, HD), BF16) for _ in range(3)],
        compiler_params=_cparams(("parallel", "parallel", "arbitrary")),
        name="nsa_attn",
    )(z, rot, cmp, cmp, rot, z, rot, z, gates)


def _nsa_gate_weight(w_in, a):
    K = w_in.shape[1]
    wg = w_in[a, :, NSA_MAIN:].reshape(K, NSA_KVH, 3 * NSA_G)
    return jnp.pad(wg, ((0, 0), (0, 0), (0, LANE - 3 * NSA_G))).reshape(K, NSA_KVH * LANE)


def _nsa_rows_body(*refs, n_layers):
    o_ref = refs[-1]
    per_layer = 4 * NSA_KVH
    for a in range(n_layers):
        @pl.when(pl.program_id(1) == a)
        def _(a=a):
            for h, src in enumerate(refs[a * per_layer:(a + 1) * per_layer]):
                o_ref[:, h, :] = src[...]


def nsa_rows(zs, rots, T):
    M = zs[0].shape[0]
    n_layers = len(zs)
    tm = _pick_tile(T, (512, 256, 128, 64, 32, 16, 8))
    nh = NSA_KVH
    q0 = NSA_QW // HD
    cols = []
    for z, rot in zip(zs, rots):
        cols += ([(z, q0 + h) for h in range(2 * nh)] + [(rot, q0 + h) for h in range(nh)]
                 + [(z, q0 + 3 * nh + h) for h in range(nh)])
    in_specs = [pl.BlockSpec((tm, HD), functools.partial(lambda i, a, c: (i, c), c=c)) for _, c in cols]
    return pl.pallas_call(
        functools.partial(_nsa_rows_body, n_layers=n_layers),
        grid=(M // tm, n_layers),
        in_specs=in_specs,
        out_specs=pl.BlockSpec((tm, None, 4 * NSA_KVH, HD), lambda i, a: (i, a, 0, 0)),
        out_shape=jax.ShapeDtypeStruct((M, n_layers, 4 * NSA_KVH, HD), F32),
        compiler_params=_cparams(("parallel", "arbitrary")),
        name="nsa_rows",
    )(*[arr for arr, _ in cols])


def nsa_layer_prefill(x, g, w_in, w1, pe, w2, w_out, a, tables, B, T):
    z = proj(x, g, jnp.swapaxes(w_in, 1, 2), n_out=NSA_MAIN, layer=a, w_rows_out=True)
    gates = proj(x, g, _nsa_gate_weight(w_in, a))
    rot = nsa_rope(z, tables, T)
    cmp = nsa_compress(z, w1, pe, w2, B, T)
    mix = nsa_attn_prefill(z, rot, cmp, gates, B, T)
    y = proj(mix, g, w_out, residual=x, norm=False, layer=a)
    W = NSA_KVH * HD
    Tw = min(WINDOW, T)
    win = jnp.concatenate([rot.reshape(B, T, -1)[:, T - Tw:, NSA_QW + W:], z.reshape(B, T, -1)[:, T - Tw:, NSA_QW + 5 * W:]],
                          axis=2).reshape(B, Tw, 2, NSA_KVH, HD)
    return y, (z, rot), win


def _round_robin(gens):
    while gens:
        alive = []
        for gen in gens:
            try:
                next(gen)
                alive.append(gen)
            except StopIteration:
                pass
        gens = alive


def _tri(c):
    return (lax.broadcasted_iota(jnp.int32, (c, 1), 0) >= lax.broadcasted_iota(jnp.int32, (1, c), 1)).astype(F32)


def _hgrn_body(q_ref, f_ref, i_ref, gz_ref, lb_ref, gn_ref, s0_ref, o_ref, s_ref, st_ref,
               *, C, SC, HP, t_valid, nt, layer):
    t = pl.program_id(2)
    Tt = q_ref.shape[0]

    @pl.when(t == 0)
    def _():
        for hp in range(HP):
            st_ref[hp] = s0_ref[0, hp].T

    lbp = lb_ref[...]
    e = jnp.exp(lbp - jnp.max(lbp, axis=0, keepdims=True))
    lb_all = jnp.sum(e[1:layer + 1], axis=0, keepdims=True) / jnp.sum(e, axis=0, keepdims=True)
    gn_all = gn_ref[...]
    tri = _tri(C)
    row8 = lax.broadcasted_iota(jnp.int32, (8, 1), 0)

    def head_chunk(hp, r0):
        cols = slice(hp * HD, (hp + 1) * HD)
        lb, gn = lb_all[:, cols], gn_all[:, cols]
        qz = q_ref[pl.ds(r0, C), cols]
        f = lb + (1.0 - lb) * _sigmoid(f_ref[pl.ds(r0, C), cols])
        v = i_ref[pl.ds(r0, C), cols]
        gz = gz_ref[pl.ds(r0, C), cols]
        k = 1.0 - f
        lf = jnp.log(f)
        if t_valid is not None:
            ok = (t * Tt + r0 + lax.broadcasted_iota(jnp.int32, (C, 1), 0)) < t_valid
            k = jnp.where(ok, k, 0.0)
            lf = jnp.where(ok, lf, 0.0)
        q = qz * _sigmoid(qz)
        b = _dot_hi(tri, lf)
        yield
        st = st_ref[hp]
        vb = v.astype(BF16)
        out = _dot_nt((q * jnp.exp(b)).astype(BF16), st.astype(BF16))
        bl = b[C - 1:C]
        st_ref[hp] = st * jnp.exp(bl) + _dot_tn(vb, (k * jnp.exp(bl - b)).astype(BF16))
        atts = []
        for I in range(1, C // SC):
            r = I * SC
            beta = b[r - 1:r]
            qt = q[r:r + SC] * jnp.exp(b[r:r + SC] - beta)
            kt = k[0:r] * jnp.exp(beta - b[0:r])
            atts.append(_dot_nt(qt.astype(BF16), kt.astype(BF16)))
        yield
        offdiag = [_dot(att.astype(BF16), vb[0:I * SC]) for I, att in enumerate(atts, 1)]
        pieces = []
        for I in range(C // SC):
            r = I * SC
            bI, qI, kI, vI = b[r:r + SC], q[r:r + SC], k[r:r + SC], v[r:r + SC]
            groups = [jnp.zeros((8, HD), F32) for _ in range(SC // 8)]
            for s in range(SC):
                for gi in range(s // 8, SC // 8):
                    rows = slice(gi * 8, gi * 8 + 8)
                    d = bI[rows] - bI[s:s + 1]
                    if gi == s // 8:
                        d = jnp.where(row8 >= s % 8, d, NEG)
                    att = jnp.sum(qI[rows] * jnp.exp(d) * kI[s:s + 1], axis=-1, keepdims=True)
                    groups[gi] = groups[gi] + att * vI[s:s + 1]
            pieces.extend(groups)
        yield
        intra = jnp.concatenate(pieces, axis=0)
        if offdiag:
            intra = intra + jnp.concatenate([jnp.zeros((SC, HD), F32)] + offdiag, axis=0)
        out = out + intra
        o_ref[pl.ds(r0, C), cols] = _rms(out, gn) * (gz * _sigmoid(gz))

    def chunk(c, carry):
        r0 = pl.multiple_of(c * C, C)
        _round_robin([head_chunk(hp, r0) for hp in range(HP)])
        return carry

    lax.fori_loop(0, Tt // C, chunk, 0)

    @pl.when(t == nt - 1)
    def _():
        for hp in range(HP):
            s_ref[0, hp] = st_ref[hp].T


def hgrn_scan(z, lb, layer, gn, s0, B, T, t_valid=None):
    depth = lb.shape[0]
    C = min(CHUNK, T)
    SC = min(SUB, C)
    Tt = _pick_tile(T, (512, 256, 128, 64, 32, 16))
    nt = T // Tt
    HP = HG_HEADS_PER_STEP
    ng = HG_HEADS // HP
    W = HP * HD
    zspec = lambda part: pl.BlockSpec((Tt, W), lambda b, h, t: (b * nt + t, part * ng + h))
    vspec = pl.BlockSpec((1, W), lambda b, h, t: (0, h))
    lspec = pl.BlockSpec((depth, W), lambda b, h, t: (0, h))
    sspec = pl.BlockSpec((1, HP, HD, HD), lambda b, h, t: (b, h, 0, 0))
    return pl.pallas_call(
        functools.partial(_hgrn_body, C=C, SC=SC, HP=HP, t_valid=t_valid, nt=nt, layer=layer),
        grid=(B, ng, nt),
        in_specs=[zspec(0), zspec(1), zspec(2), zspec(3), lspec, vspec, sspec],
        out_specs=[pl.BlockSpec((Tt, W), lambda b, h, t: (b * nt + t, h)), sspec],
        out_shape=[jax.ShapeDtypeStruct((B * T, HG_W), F32), jax.ShapeDtypeStruct((B, HG_HEADS, HD, HD), F32)],
        scratch_shapes=[pltpu.VMEM((HP, HD, HD), F32)],
        compiler_params=_cparams(("parallel", "parallel", "arbitrary")),
        name="hgrn_scan",
    )(z, z, z, z, lb, gn.reshape(1, HG_W), s0)


def _mlstm_body(q_ref, k_ref, v_ref, og_ref, gt_ref, bg_ref, c0_ref, n0_ref, m0_ref,
                y_ref, c_ref, n_ref, m_ref, cs_ref, ns_ref, ms_ref, *, C, HP, t_valid, nt):
    t = pl.program_id(2)
    Tt = q_ref.shape[0]

    @pl.when(t == 0)
    def _():
        cs_ref[...] = c0_ref[0]
        ns_ref[...] = n0_ref[0]
        ms_ref[...] = m0_ref[0]

    tri_b = lax.broadcasted_iota(jnp.int32, (C, 1), 0) >= lax.broadcasted_iota(jnp.int32, (1, C), 1)
    upper = (lax.broadcasted_iota(jnp.int32, (C, 1), 0) <= lax.broadcasted_iota(jnp.int32, (1, C), 1)).astype(F32)
    eye = (lax.broadcasted_iota(jnp.int32, (C, 1), 0) == lax.broadcasted_iota(jnp.int32, (1, C), 1)).astype(F32)
    tri = tri_b.astype(F32)
    ones = jnp.ones((C, C), F32)
    lane = lax.broadcasted_iota(jnp.int32, (1, LANE), 1)
    bg = bg_ref[...]

    def head_chunk(hp, r0, g):
        h = pl.program_id(1) * HP + hp
        kcols = slice(hp * ML_DK, (hp + 1) * ML_DK)
        vcols = slice(hp * ML_DV, (hp + 1) * ML_DV)
        ig = jnp.sum(jnp.where(lane == h, g, 0.0), axis=-1, keepdims=True)
        fg = jnp.sum(jnp.where(lane == ML_HEADS + h, g, 0.0), axis=-1, keepdims=True)
        lf = jnp.minimum(fg, 0.0) - jnp.log(1.0 + jnp.exp(-jnp.abs(fg)))
        if t_valid is not None:
            ok = (t * Tt + r0 + lax.broadcasted_iota(jnp.int32, (C, 1), 0)) < t_valid
            lf = jnp.where(ok, lf, 0.0)
            ig = jnp.where(ok, ig, NEG)
        lf_b = jnp.broadcast_to(lf, (C, C))
        ig_b = jnp.broadcast_to(ig, (C, C))
        b_col = _dot_hi(tri, lf_b)
        b_row_i = _dot_hi(ones, lf_b * upper - ig_b * eye)
        q = q_ref[pl.ds(r0, C), kcols]
        ks = k_ref[pl.ds(r0, C), kcols] * (ML_DK ** -0.5)
        v = v_ref[pl.ds(r0, C), vcols]
        qb, kb, vb = q.astype(BF16), ks.astype(BF16), v.astype(BF16)
        cs = cs_ref[hp]
        ns = ns_ref[hp]
        qk = _dot_nt(qb, kb)
        inter = _dot(qb, cs.astype(BF16))
        yield
        logw = jnp.where(tri_b, b_col - b_row_i, NEG)
        m_old = ms_ref[hp][:, 0:1]
        b1 = b_col[:, 0:1]
        m_prev = b1 + m_old
        m_t = jnp.maximum(m_prev, jnp.max(logw, axis=-1, keepdims=True))
        w0 = jnp.exp(m_prev - m_t)
        sqk = qk * jnp.exp(logw - m_t)
        intra = _dot(sqk.astype(BF16), vb)
        m_new = m_t[C - 1:C]
        b_last = b1[C - 1:C]
        wk = jnp.exp(b_last - b1 + ig - m_new)
        decay = jnp.exp(b_last + m_old - m_new)
        kw = ks * wk
        cs_ref[hp] = decay * cs + _dot_tn(kw.astype(BF16), vb)
        ns_ref[hp] = decay * ns + jnp.sum(kw, axis=0, keepdims=True)
        ms_ref[hp] = jnp.broadcast_to(m_new, (1, LANE))
        den = w0 * jnp.sum(q * ns, axis=-1, keepdims=True) + jnp.sum(sqk, axis=-1, keepdims=True)
        yield
        hc = (w0 * inter + intra) / jnp.maximum(jnp.abs(den), jnp.exp(-m_t))
        y_ref[pl.ds(r0, C), vcols] = _sigmoid(og_ref[pl.ds(r0, C), vcols]) * hc

    def chunk(c, carry):
        r0 = pl.multiple_of(c * C, C)
        g = gt_ref[pl.ds(r0, C), :] + bg
        g = ML_GATE_CAP * jnp.tanh(g / ML_GATE_CAP)
        _round_robin([head_chunk(hp, r0, g) for hp in range(HP)])
        return carry

    lax.fori_loop(0, Tt // C, chunk, 0)

    @pl.when(t == nt - 1)
    def _():
        c_ref[0] = cs_ref[...]
        n_ref[0] = ns_ref[...]
        m_ref[0] = ms_ref[...]


def mlstm_scan(z, gates, b_gate, c0, n0, m0, B, T, t_valid=None):
    C = min(CHUNK, T)
    Tt = _pick_tile(T, (512, 256, 128, 64, 32, 16))
    nt = T // Tt
    H = ML_HEADS
    row = lambda b, t: b * nt + t
    bg = jnp.pad(b_gate, (0, LANE - 2 * H)).reshape(1, LANE)
    n0 = n0.reshape(B, H, 1, ML_DK)
    m0 = jnp.broadcast_to(m0[:, :, None, None], (B, H, 1, LANE))
    HP = ML_HEADS_PER_STEP
    ng = H // HP
    kw, vw = HP * ML_DK, HP * ML_DV
    vec = lambda w: pl.BlockSpec((1, HP, 1, w), lambda b, h, t: (b, h, 0, 0))
    cspec = pl.BlockSpec((1, HP, ML_DK, ML_DV), lambda b, h, t: (b, h, 0, 0))
    y, c, n, m = pl.pallas_call(
        functools.partial(_mlstm_body, C=C, HP=HP, t_valid=t_valid, nt=nt),
        grid=(B, ng, nt),
        in_specs=[pl.BlockSpec((Tt, kw), lambda b, h, t: (row(b, t), h)),
                  pl.BlockSpec((Tt, kw), lambda b, h, t: (row(b, t), ng + h)),
                  pl.BlockSpec((Tt, vw), lambda b, h, t: (row(b, t), 2 * ML_QK // vw + h)),
                  pl.BlockSpec((Tt, vw), lambda b, h, t: (row(b, t), (2 * ML_QK + ML_VW) // vw + h)),
                  pl.BlockSpec((Tt, LANE), lambda b, h, t: (row(b, t), 0)),
                  pl.BlockSpec((1, LANE), lambda b, h, t: (0, 0)),
                  cspec, vec(ML_DK), vec(LANE)],
        out_specs=[pl.BlockSpec((Tt, vw), lambda b, h, t: (row(b, t), h)), cspec, vec(ML_DK), vec(LANE)],
        out_shape=[jax.ShapeDtypeStruct((B * T, ML_VW), F32), jax.ShapeDtypeStruct((B, H, ML_DK, ML_DV), F32),
                   jax.ShapeDtypeStruct((B, H, 1, ML_DK), F32), jax.ShapeDtypeStruct((B, H, 1, LANE), F32)],
        scratch_shapes=[pltpu.VMEM((HP, ML_DK, ML_DV), F32), pltpu.VMEM((HP, 1, ML_DK), F32),
                        pltpu.VMEM((HP, 1, LANE), F32)],
        compiler_params=_cparams(("parallel", "parallel", "arbitrary")),
        name="mlstm_scan",
    )(z, z, z, z, gates, bg, c0, n0, m0)
    return y, c, n.reshape(B, H, ML_DK), m[:, :, 0, 0]


PAGES_PER_STEP = 16
SEG_PER_PAGE = PAGE_SIZE // CMP_STRIDE
TILE_ROWS = 2 * NSA_KVH


def _cmp_paged_body(pt_ref, *refs):
    P = PAGES_PER_STEP
    x_refs, w_ref, o_ref, lhs_ref = refs[:P], refs[P], refs[P + 1], refs[P + 2]
    rows = SEG_PER_PAGE * TILE_ROWS
    for p in range(P):
        for j in range(CMP_STRIDE):
            xj = x_refs[p][pl.ds(j, SEG_PER_PAGE, stride=CMP_STRIDE)]
            lhs_ref[p * rows:(p + 1) * rows, j * HD:(j + 1) * HD] = xj.reshape(rows, HD).astype(BF16)
    res = _dot(lhs_ref[...], w_ref[...])
    c = lax.broadcasted_iota(jnp.int32, (P * rows, 1), 0) % TILE_ROWS
    sel = jnp.where(c < NSA_KVH, res[:, :2 * HD], res[:, 2 * HD:])
    o_ref[0, 0] = sel[:, :HD]
    o_ref[0, 1] = sel[:, HD:]


def nsa_compress_paged(cache4, page_table, w1, a):
    B, n_pages = page_table.shape
    P = PAGES_PER_STEP
    rows = SEG_PER_PAGE * TILE_ROWS
    w_cat = w1.reshape(2, CMP_RATIO, CMP_STRIDE, HD, HD).transpose(2, 3, 0, 1, 4).reshape(CMP_STRIDE * HD, 4 * HD)
    xspec = lambda i: pl.BlockSpec((PAGE_SIZE, None, TILE_ROWS, HD),
                                   lambda b, s, pt: (pt[b, s * P + i], 2 * a, 0, 0))
    grid_spec = pltpu.PrefetchScalarGridSpec(
        num_scalar_prefetch=1,
        grid=(B, n_pages // P),
        in_specs=[xspec(i) for i in range(P)] + [pl.BlockSpec((CMP_STRIDE * HD, 4 * HD), lambda b, s, pt: (0, 0))],
        out_specs=pl.BlockSpec((1, CMP_RATIO, P * rows, HD), lambda b, s, pt: (b, 0, s, 0)),
        scratch_shapes=[pltpu.VMEM((P * rows, CMP_STRIDE * HD), BF16)])
    return pl.pallas_call(
        _cmp_paged_body,
        grid_spec=grid_spec,
        out_shape=jax.ShapeDtypeStruct((B, CMP_RATIO, n_pages * rows, HD), F32),
        compiler_params=_cparams(("parallel", "arbitrary")),
        name="nsa_compress_paged",
    )(page_table, *([cache4] * P), w_cat.astype(BF16))


def _nsa_dec_cmp_body(part_ref, w1_ref, pe_ref, w2_ref, q_ref, o_ref, idx_ref, *, n_seg, n_slc, pos):
    nsp = -(-n_slc // LANE) * LANE
    scale = HD ** -0.5
    pe_hid = [_pe_hidden(pe_ref.at[pl.ds(s, 1)], w1_ref.at[pl.ds(s, 1)]) for s in range(2)]
    q = q_ref[0].astype(BF16)
    head_kv = lax.broadcasted_iota(jnp.int32, (NSA_HEADS, 1), 0) // NSA_G
    col_m = lax.broadcasted_iota(jnp.int32, (1, n_seg), 1)
    valid_m = (col_m < n_seg - CMP_RATIO + 1) & (col_m * CMP_STRIDE + CMP_BLOCK - 1 <= pos)

    def tokens(slot, kh):
        rows = pl.ds(slot * NSA_KVH + kh, n_seg, stride=TILE_ROWS)
        hid = part_ref[0, 0, rows, :] + pltpu.roll(part_ref[0, 1, rows, :], n_seg - 1, 0) + pe_hid[slot]
        return _dot(_gelu(hid).astype(BF16), w2_ref[slot].astype(BF16)).astype(BF16)

    o_cmp = jnp.zeros((NSA_HEADS, HD), F32)
    psum_rows = []
    for kh in range(NSA_KVH):
        valid = valid_m & (head_kv == kh)
        s = jnp.where(valid, _dot_nt(q, tokens(0, kh)) * scale, NEG)
        e = jnp.exp(s - jnp.max(s, axis=-1, keepdims=True))
        prob = jnp.where(valid, e / jnp.sum(e, axis=-1, keepdims=True), 0.0)
        o_cmp = o_cmp + _dot(prob.astype(BF16), tokens(1, kh))
        psum_rows.append(jnp.sum(prob, axis=0, keepdims=True))
    o_ref[0] = o_cmp
    psum = jnp.concatenate(psum_rows + [jnp.zeros((8 - NSA_KVH, n_seg), F32)], axis=0)
    m_r = lax.broadcasted_iota(jnp.int32, (n_seg, 1), 0)
    j = lax.broadcasted_iota(jnp.int32, (1, nsp), 1)
    ov = ((m_r * CMP_STRIDE <= j * SLC_BLOCK + SLC_BLOCK - 1) & (m_r * CMP_STRIDE + CMP_BLOCK - 1 >= j * SLC_BLOCK)
          & (m_r < n_seg - CMP_RATIO + 1) & (j < n_slc)).astype(F32)
    imp = _dot_hi(psum, ov)
    cur = pos // SLC_BLOCK
    forced = (j == 0) | (j == cur) | (j == cur - 1)
    score = jnp.where(j * SLC_BLOCK <= pos, imp + FORCE_BONUS * forced.astype(F32), -1.0)
    score = jnp.where(j < n_slc, score, -2.0)
    jf = j.astype(F32)
    slot = lax.broadcasted_iota(jnp.int32, (1, LANE), 1)
    picked = jnp.zeros((8, LANE), F32)
    for it in range(min(SLC_TOP_N, n_slc)):
        best = jnp.max(score, axis=-1, keepdims=True)
        ix = jnp.min(jnp.where(score == best, jf, float(nsp)), axis=-1, keepdims=True)
        picked = jnp.where(slot == it, ix, picked)
        score = jnp.where(jf == ix, -3.0, score)
    idx_ref[0] = picked.astype(jnp.int32)


def nsa_dec_cmp(part, w1, pe, w2, q_raw, n_seg, n_slc, pos):
    B = part.shape[0]
    R = n_seg * TILE_ROWS
    full = lambda shape: pl.BlockSpec(shape, lambda b: (0,) * len(shape))
    return pl.pallas_call(
        functools.partial(_nsa_dec_cmp_body, n_seg=n_seg, n_slc=n_slc, pos=pos),
        grid=(B,),
        in_specs=[pl.BlockSpec((1, CMP_RATIO, R, HD), lambda b: (b, 0, 0, 0)),
                  full((2, CMP_BLOCK, HD, HD)), full((2, CMP_BLOCK, HD)), full((2, HD, HD)),
                  pl.BlockSpec((1, NSA_HEADS, HD), lambda b: (b, 0, 0))],
        out_specs=[pl.BlockSpec((1, NSA_HEADS, HD), lambda b: (b, 0, 0)),
                   pl.BlockSpec((1, 8, LANE), lambda b: (b, 0, 0))],
        out_shape=[jax.ShapeDtypeStruct((B, NSA_HEADS, HD), F32), jax.ShapeDtypeStruct((B, 8, LANE), jnp.int32)],
        compiler_params=_cparams(("parallel",)),
        name="nsa_dec_cmp",
    )(part, w1, pe, w2, q_raw)


def _nsa_dec_attn_body(pt_ref, idx_ref, *refs, n_cache_blocks, nsel, win_skip):
    blk_refs = refs[:nsel]
    q_ref, new_ref, win_ref, ocmp_ref, gate_ref, o_ref = refs[nsel:]
    b = pl.program_id(0)
    kh = pl.program_id(1)
    scale = HD ** -0.5
    q = q_ref[0, 0].astype(BF16)
    new = new_ref[0, 0]
    s_new = _dot_nt(q, new.astype(BF16)) * scale

    def attend(x, msk, s_own, v_own):
        s = [jnp.where(m, _dot_nt(q, xi) * scale, NEG) for xi, m in zip(x, msk)]
        mx = s_own
        for si in s:
            mx = jnp.maximum(mx, jnp.max(si, axis=-1, keepdims=True))
        p_own = jnp.exp(s_own - mx)
        num = p_own * v_own
        den = p_own
        for si, xi in zip(s, x):
            p = jnp.exp(si - mx)
            num = num + _dot(pltpu.roll(p, NSA_KVH, 1).astype(BF16), xi)
            den = den + jnp.sum(p, axis=-1, keepdims=True)
        return num / den

    col_c = lax.broadcasted_iota(jnp.int32, (1, SLC_BLOCK * TILE_ROWS), 1) % TILE_ROWS
    xs = [r[...].reshape(SLC_BLOCK * TILE_ROWS, HD).astype(BF16) for r in blk_refs]
    ms = [(col_c == kh) & (idx_ref[b, kh, n] < n_cache_blocks) for n in range(nsel)]
    o_slc = attend(xs, ms, s_new[:, 0:1], new[1:2])

    n_win = win_ref.shape[1]
    xw = win_ref[0].reshape(n_win * TILE_ROWS, HD).astype(BF16)
    colw = lax.broadcasted_iota(jnp.int32, (1, n_win * TILE_ROWS), 1)
    mw = ((colw % TILE_ROWS) == kh) & ((colw // TILE_ROWS) >= win_skip)
    o_win = attend([xw], [mw], s_new[:, 2:3], new[3:4])

    gate = _sigmoid(gate_ref[0, 0])
    lane = lax.broadcasted_iota(jnp.int32, (1, LANE), 1)
    row = lax.broadcasted_iota(jnp.int32, (8, 1), 0)
    pick = lambda c: jnp.sum(jnp.where(lane == 3 * row + c, gate, 0.0), axis=-1, keepdims=True)
    o_ref[0, 0] = pick(0) * ocmp_ref[0, 0] + pick(1) * o_slc + pick(2) * o_win


def nsa_dec_attn(cache5, win4, page_table, idx, q_rot, new_rows, o_cmp, gates, a, n_cache_blocks, win_skip):
    B, _, nsel = idx.shape
    n_win = win4.shape[1]

    def blk_spec(n):
        def blk_map(b, k, pt, ix):
            j = jnp.minimum(ix[b, k, n], n_cache_blocks - 1)
            return (pt[b, j // 2] * 2 + j % 2, 0, 2 * a + 1, 0, 0)
        return pl.BlockSpec((None, SLC_BLOCK, None, TILE_ROWS, HD), blk_map)

    grp = pl.BlockSpec((1, 1, 8, HD), lambda b, k, pt, ix: (b, k, 0, 0))
    grid_spec = pltpu.PrefetchScalarGridSpec(
        num_scalar_prefetch=2,
        grid=(B, NSA_KVH),
        in_specs=[blk_spec(n) for n in range(nsel)] + [
            grp, grp, pl.BlockSpec((1, n_win, TILE_ROWS, HD), lambda b, k, pt, ix: (a * B + b, 0, 0, 0)), grp, grp],
        out_specs=grp)
    return pl.pallas_call(
        functools.partial(_nsa_dec_attn_body, n_cache_blocks=n_cache_blocks, nsel=nsel, win_skip=win_skip),
        grid_spec=grid_spec,
        out_shape=jax.ShapeDtypeStruct((B, NSA_KVH, 8, HD), F32),
        compiler_params=_cparams(("parallel", "arbitrary")),
        name="nsa_dec_attn",
    )(page_table, idx, *([cache5] * nsel), q_rot, new_rows, win4, o_cmp, gates)


def _group_rows(x, B):
    x = x.reshape(B, NSA_KVH, NSA_G, HD)
    return jnp.pad(x, ((0, 0), (0, 0), (0, 8 - NSA_G), (0, 0)))


def nsa_layer_decode(x, g, w_in, w1, pe, w2, w_out, cache, win_state, page_table, a, B):
    R = x.shape[0]
    n_pool = cache.shape[0]
    n_pages = page_table.shape[1]
    past = n_pages * PAGE_SIZE
    n_seg = (past + 1) // CMP_STRIDE
    n_slc = -(-(past + 1) // SLC_BLOCK)
    W = NSA_KVH * HD
    z = proj(x, g, jnp.swapaxes(w_in, 1, 2), n_out=NSA_MAIN, layer=a, w_rows_out=True)
    gates = proj(x, g, _nsa_gate_weight(w_in, a))
    rot = nsa_rope(z, rope_tables(jnp.full((R,), past, jnp.int32)), R)
    part = nsa_compress_paged(cache.reshape(n_pool * PAGE_SIZE, 4, TILE_ROWS, HD), page_table, w1, a)
    o_cmp, idx = nsa_dec_cmp(part, w1, pe, w2, z[:B, :NSA_QW].reshape(B, NSA_HEADS, HD), n_seg, n_slc, past)
    nsel = min(SLC_TOP_N, n_slc)
    kv = z[:B, NSA_QW:].reshape(B, 6, NSA_KVH, HD)
    ks_new = rot[:B, NSA_QW:NSA_QW + W].reshape(B, NSA_KVH, HD)
    kw_new = rot[:B, NSA_QW + W:].reshape(B, NSA_KVH, HD)
    new_rows = jnp.stack([ks_new, kv[:, 3], kw_new, kv[:, 5]], axis=2)
    new_rows = jnp.pad(new_rows, ((0, 0), (0, 0), (0, 4), (0, 0)))
    n_win = win_state.shape[2]
    mix = nsa_dec_attn(cache.reshape(n_pool * 2, SLC_BLOCK, 4, TILE_ROWS, HD),
                       win_state.reshape(-1, n_win, TILE_ROWS, HD), page_table, idx[:, :NSA_KVH, :nsel],
                       _group_rows(rot[:B, :NSA_QW], B), new_rows, _group_rows(o_cmp.reshape(B, NSA_QW), B),
                       jnp.broadcast_to(gates[:B].reshape(B, NSA_KVH, 1, LANE), (B, NSA_KVH, 8, LANE)),
                       a, past // SLC_BLOCK, max(n_win + 1 - WINDOW, 0))
    mix = jnp.pad(mix[:, :, :NSA_G].reshape(B, NSA_QW), ((0, R - B), (0, 0)))
    y = proj(mix, g, w_out, residual=x, norm=False, layer=a)
    rows = jnp.stack([kv[:, 0], kv[:, 1], ks_new, kv[:, 3]], axis=1)[:, None]
    win_new = jnp.stack([kw_new, kv[:, 5]], axis=1)[:, None]
    buf = jnp.concatenate([win_state[a], win_new], axis=1)
    return y, rows, buf[:, -min(WINDOW, n_win + 1):]


def kernel(x_prompt, x_sample, cache_nsa_kv, state_nsa_win, state_hgrn, state_mlstm_c, state_mlstm_n, state_mlstm_m, state_ffn_conv, page_table, norm_mix, norm_ffn, norm_out, nsa_w_in, nsa_cmp_w1, nsa_cmp_pe, nsa_cmp_w2, nsa_w_out, hgrn_w_in, hgrn_lb, hgrn_norm, hgrn_w_out, ml_w_in, ml_b_gate, ml_w_out, ffn_w_up, ffn_conv_w, ffn_conv_b, ffn_w_down):
    P = dict(norm_mix=norm_mix, norm_ffn=norm_ffn, norm_out=norm_out, nsa_w_in=nsa_w_in, nsa_cmp_w1=nsa_cmp_w1,
             nsa_cmp_pe=nsa_cmp_pe, nsa_cmp_w2=nsa_cmp_w2, nsa_w_out=nsa_w_out, hgrn_w_in=hgrn_w_in,
             hgrn_lb=hgrn_lb, hgrn_norm=hgrn_norm, hgrn_w_out=hgrn_w_out, ml_w_in=ml_w_in, ml_b_gate=ml_b_gate,
             ml_w_out=ml_w_out, ffn_w_up=ffn_w_up, ffn_conv_w=ffn_conv_w, ffn_conv_b=ffn_conv_b,
             ffn_w_down=ffn_w_down)
    past = dict(cache=cache_nsa_kv, win=state_nsa_win, hgrn=state_hgrn, c=state_mlstm_c, n=state_mlstm_n,
                m=state_mlstm_m, ffn=state_ffn_conv, page_table=page_table)
    yp, sp = _trunk(x_prompt, P, None)
    ys, ss = _trunk(x_sample, P, past)
    return (yp, ys, sp['nsa_kv'], ss['nsa_kv'], sp['nsa_win'], ss['nsa_win'], sp['hgrn'], ss['hgrn'],
            sp['c'], ss['c'], sp['n'], ss['n'], sp['m'], ss['m'], sp['ffn'], ss['ffn'])


DEC_ROWS = 16


def _spread(z, B, T):
    return jnp.zeros((B, T, z.shape[1]), F32).at[:, 0].set(z[:B]).reshape(B * T, z.shape[1])


def _gather_first(o, B, T, R):
    return jnp.pad(o.reshape(B, T, -1)[:, 0], ((0, R - B), (0, 0)))


def _trunk(x_in, P, past):
    B, T, D = x_in.shape
    decode = past is not None
    if decode:
        R = DEC_ROWS
        x = jnp.pad(x_in.reshape(B, D), ((0, R - B), (0, 0)))
        Ts = DEC_ROWS
    else:
        x = x_in.reshape(B * T, D)
        tables = rope_tables(jnp.arange(T, dtype=jnp.int32))
    rows_l, win_l, hg_l, c_l, n_l, m_l, ffn_l = [], [], [], [], [], [], []
    for i in range(DEPTH):
        kind, a = i % N_MIXERS, i // N_MIXERS
        g = P['norm_mix'][i]
        if kind == 0:
            args = (x, g, P['nsa_w_in'], P['nsa_cmp_w1'][a], P['nsa_cmp_pe'][a], P['nsa_cmp_w2'][a], P['nsa_w_out'])
            if decode:
                x, rows, win = nsa_layer_decode(*args, past['cache'], past['win'], past['page_table'], a, B)
                rows_l.append(rows)
            else:
                x, rows, win = nsa_layer_prefill(*args, a, tables, B, T)
                rows_l.append(rows)
            win_l.append(win)
        elif kind == 1:
            z = proj(x, g, P['hgrn_w_in'], layer=a)
            if decode:
                o, S = hgrn_scan(_spread(z, B, Ts), P['hgrn_lb'], i, P['hgrn_norm'][a], past['hgrn'][a], B, Ts, 1)
                o = _gather_first(o, B, Ts, R)
            else:
                o, S = hgrn_scan(z, P['hgrn_lb'], i, P['hgrn_norm'][a], jnp.zeros((B, HG_HEADS, HD, HD), F32), B, T)
            x = proj(o, g, P['hgrn_w_out'], residual=x, norm=False, layer=a)
            hg_l.append(S)
        else:
            w = P['ml_w_in']
            z = proj(x, g, jnp.swapaxes(w, 1, 2), n_out=ML_MAIN, layer=a, w_rows_out=True)
            gt = proj(x, g, jnp.pad(w[a, :, ML_MAIN:], ((0, 0), (0, LANE - 2 * ML_HEADS))))
            if decode:
                o, C, nv, m = mlstm_scan(_spread(z, B, Ts), _spread(gt, B, Ts), P['ml_b_gate'][a],
                                         past['c'][a], past['n'][a], past['m'][a], B, Ts, 1)
                o = _gather_first(o, B, Ts, R)
            else:
                o, C, nv, m = mlstm_scan(z, gt, P['ml_b_gate'][a], jnp.zeros((B, ML_HEADS, ML_DK, ML_DV), F32),
                                         jnp.zeros((B, ML_HEADS, ML_DK), F32), jnp.zeros((B, ML_HEADS), F32), B, T)
            x = proj(o, g, P['ml_w_out'], residual=x, norm=False, layer=a)
            c_l.append(C)
            n_l.append(nv)
            m_l.append(m)
        fargs = (P['norm_ffn'][i], P['ffn_w_up'], P['ffn_conv_w'], P['ffn_conv_b'], P['ffn_w_down'], i,
                 P['norm_out'])
        last = i == DEPTH - 1
        if decode:
            buf = past['ffn'][i]
            bufp = jnp.pad(buf.transpose(1, 0, 2), ((0, 0), (0, R - B), (0, 0)))
            y, sa = conv_ffn(x[None], *fargs, buf=bufp, final_norm=last)
            x = y[0]
            ffn_l.append(jnp.stack([buf[:, 1], sa[0, :B]], axis=1))
        else:
            y, sa = conv_ffn(x.reshape(B, T, D), *fargs, final_norm=last)
            x = y.reshape(B * T, D)
            ffn_l.append(sa[:, 6:8])
    out = x[:B].reshape(B, 1, D) if decode else x.reshape(B, T, D)
    if decode:
        nsa_kv = jnp.stack(rows_l, axis=2)
    else:
        nsa_kv = nsa_rows([z for z, _ in rows_l], [rot for _, rot in rows_l], T)
        nsa_kv = nsa_kv.reshape(B, T, len(rows_l), 4, NSA_KVH, HD)
    states = dict(nsa_kv=nsa_kv, nsa_win=jnp.stack(win_l, axis=0), hgrn=jnp.stack(hg_l, axis=0),
                  c=jnp.stack(c_l, axis=0), n=jnp.stack(n_l, axis=0), m=jnp.stack(m_l, axis=0),
                  ffn=jnp.stack(ffn_l, axis=0))
    return out, states
```

```python
import functools
import math

import jax
import jax.numpy as jnp
from jax import lax
from jax.experimental import pallas as pl
from jax.experimental.pallas import tpu as pltpu

F32 = jnp.float32
BF16 = jnp.bfloat16

D_MODEL = 2048
DEPTH = 4
PAGE_SIZE = 128
N_MIXERS = 3

HD = 128
NSA_HEADS = D_MODEL // HD
NSA_KVH = 4
NSA_G = NSA_HEADS // NSA_KVH
NSA_QW = NSA_HEADS * HD
NSA_KVW = 6 * NSA_KVH * HD
NSA_MAIN = NSA_QW + NSA_KVW
CMP_STRIDE = 16
CMP_RATIO = 2
CMP_BLOCK = CMP_STRIDE * CMP_RATIO
SLC_BLOCK = 64
SLC_TOP_N = 16
FORCE_BONUS = 1000.0
WINDOW = 512
ROPE_THETA = 500000.0
ROPE_DIM = HD // 4
ROPE_HALF = ROPE_DIM // 2

HG_HEADS = D_MODEL // HD
HG_W = HG_HEADS * HD

ML_HEADS = 8
ML_DK = D_MODEL // (2 * ML_HEADS)
ML_DV = D_MODEL // ML_HEADS
ML_QK = ML_HEADS * ML_DK
ML_VW = ML_HEADS * ML_DV
ML_MAIN = 2 * ML_QK + 2 * ML_VW
ML_GATE_CAP = 15.0
CHUNK = 64
SUB = 16
HG_HEADS_PER_STEP = 4
ML_HEADS_PER_STEP = 8

D_FF = (11 * D_MODEL) // 4
EPS = 1e-6
NEG = -1e30

LANE = 128
VMEM_LIMIT = 56 * 1024 * 1024
FFN_VMEM_LIMIT = 62 * 1024 * 1024


def _cparams(sem, vmem_limit=VMEM_LIMIT):
    return pltpu.CompilerParams(dimension_semantics=sem, vmem_limit_bytes=vmem_limit)


def _dot(a, b):
    return jnp.dot(a, b, preferred_element_type=F32)


def _dot_nt(a, b):
    return lax.dot_general(a, b, (((1,), (1,)), ((), ())), preferred_element_type=F32)


def _dot_tn(a, b):
    return lax.dot_general(a, b, (((0,), (0,)), ((), ())), preferred_element_type=F32)


def _dot_hi(a, b):
    return jnp.dot(a, b, preferred_element_type=F32, precision=lax.Precision.HIGHEST)


def _rms(x, g):
    return x * lax.rsqrt(jnp.mean(x * x, axis=-1, keepdims=True) + EPS) * g


def _sigmoid(x):
    return 1.0 / (1.0 + jnp.exp(-x))


def _pick_tile(n, cands):
    for c in cands:
        if n % c == 0:
            return c
    return n


def _proj_body(*refs, norm, residual, w_rows_out):
    if residual:
        x_ref, g_ref, w_ref, r_ref, o_ref, xn_ref = refs
    else:
        x_ref, g_ref, w_ref, o_ref, xn_ref = refs

    @pl.when(pl.program_id(1) == 0)
    def _():
        x = x_ref[...]
        if norm:
            x = _rms(x, g_ref[...])
        xn_ref[...] = x.astype(BF16)

    w = w_ref[...].astype(BF16)
    acc = _dot_nt(xn_ref[...], w) if w_rows_out else _dot(xn_ref[...], w)
    if residual:
        acc = acc + r_ref[...]
    o_ref[...] = acc


def proj(x, g, w, residual=None, norm=True, n_out=None, layer=None, w_rows_out=False):
    M, K = x.shape
    N = w.shape[1 if w_rows_out else -1] if n_out is None else n_out
    tm = _pick_tile(M, (1024, 512, 256, 128, 64, 32, 16))
    tn = _pick_tile(N, (512, 384, 256, 128) if residual is not None else (1024, 512, 384, 256, 128))
    if w_rows_out:
        wspec = pl.BlockSpec((None, tn, K), lambda i, j: (layer, j, 0))
    elif layer is None:
        wspec = pl.BlockSpec((K, tn), lambda i, j: (0, j))
    else:
        wspec = pl.BlockSpec((None, K, tn), lambda i, j: (layer, 0, j))
    in_specs = [pl.BlockSpec((tm, K), lambda i, j: (i, 0)),
                pl.BlockSpec((1, K), lambda i, j: (0, 0)),
                wspec]
    args = [x, g.reshape(1, K), w]
    if residual is not None:
        in_specs.append(pl.BlockSpec((tm, tn), lambda i, j: (i, j)))
        args.append(residual)
    return pl.pallas_call(
        functools.partial(_proj_body, norm=norm, residual=residual is not None, w_rows_out=w_rows_out),
        grid=(M // tm, N // tn),
        in_specs=in_specs,
        out_specs=pl.BlockSpec((tm, tn), lambda i, j: (i, j)),
        out_shape=jax.ShapeDtypeStruct((M, N), F32),
        scratch_shapes=[pltpu.VMEM((tm, K), BF16)],
        compiler_params=_cparams(("parallel", "arbitrary")),
        name="proj",
    )(*args)


def _ffn_body(x_ref, g_ref, wa_ref, wu_ref, cw_ref, cb_ref, wd_ref, go_ref, b0_ref, b1_ref,
              y_ref, sa_ref, xn_ref, carry_ref, *, rows_are_batch, final_norm, nf):
    t = pl.program_id(0)
    f = pl.program_id(1)
    S, tm = x_ref.shape[0], x_ref.shape[1]

    @pl.when(f == 0)
    def _():
        for s in range(S):
            xn_ref[s] = _rms(x_ref[s], g_ref[...]).astype(BF16)
            y_ref[s] = x_ref[s]

    wa = wa_ref[...].astype(BF16)
    wu = wu_ref[...].astype(BF16)
    wd = wd_ref[...].astype(BF16)
    cw = cw_ref[...]
    cb = cb_ref[...]
    row = lax.broadcasted_iota(jnp.int32, (tm, 1), 0)
    for s in range(S):
        xn = xn_ref[s]
        a = _dot(xn, wa)
        u = _dot(xn, wu)
        if rows_are_batch:
            am2 = b0_ref[...]
            am1 = b1_ref[...]
        else:
            prev = jnp.where(t == 0, 0.0, carry_ref[f, s])
            p1 = prev[7:8]
            p2 = prev[6:7]
            am1 = jnp.where(row >= 1, pltpu.roll(a, 1, 0), p1)
            am2 = jnp.where(row >= 2, pltpu.roll(a, 2, 0), jnp.where(row == 1, p1, p2))
            carry_ref[f, s] = a[tm - 8:tm]
        conv = cb + am2 * cw[0:1] + am1 * cw[1:2] + a * cw[2:3]
        gt = conv * _sigmoid(conv) * u
        y_ref[s] += _dot(gt.astype(BF16), wd)
        sa_ref[s, 0] = a[0:8] if rows_are_batch else a[tm - 8:tm]

    if final_norm:
        @pl.when(f == nf - 1)
        def _():
            for s in range(S):
                y_ref[s] = _rms(y_ref[s], go_ref[...])


def conv_ffn(x, g, w_up, conv_w, conv_b, w_down, layer, g_out, buf=None, final_norm=False):
    B, T, D = x.shape
    F = w_down.shape[1]
    rows_are_batch = buf is not None
    tm = _pick_tile(T, (512, 256, 128, 64, 32, 16, 8))
    tf = _pick_tile(F, (512, 256, 128))
    nf = F // tf
    if buf is None:
        b0 = b1 = jnp.zeros((8, F), F32)
        bspec = pl.BlockSpec((8, tf), lambda t, f: (0, f))
    else:
        b0, b1 = buf[0], buf[1]
        bspec = pl.BlockSpec((tm, tf), lambda t, f: (0, f))
    y, sa = pl.pallas_call(
        functools.partial(_ffn_body, rows_are_batch=rows_are_batch, final_norm=final_norm, nf=nf),
        grid=(T // tm, nf),
        in_specs=[pl.BlockSpec((B, tm, D), lambda t, f: (0, t, 0), pipeline_mode=pl.Buffered(1)),
                  pl.BlockSpec((1, D), lambda t, f: (0, 0)),
                  pl.BlockSpec((None, D, tf), lambda t, f: (layer, 0, f)),
                  pl.BlockSpec((None, D, tf), lambda t, f: (layer, 0, nf + f)),
                  pl.BlockSpec((None, 3, tf), lambda t, f: (layer, 0, f)),
                  pl.BlockSpec((None, 1, tf), lambda t, f: (layer, 0, f)),
                  pl.BlockSpec((None, tf, D), lambda t, f: (layer, f, 0)),
                  pl.BlockSpec((1, D), lambda t, f: (0, 0)),
                  bspec, bspec],
        out_specs=[pl.BlockSpec((B, tm, D), lambda t, f: (0, t, 0), pipeline_mode=pl.Buffered(1)),
                   pl.BlockSpec((B, 1, 8, tf), lambda t, f: (0, t, 0, f))],
        out_shape=[jax.ShapeDtypeStruct((B, T, D), F32), jax.ShapeDtypeStruct((B, T // tm, 8, F), F32)],
        scratch_shapes=[pltpu.VMEM((B, tm, D), BF16), pltpu.VMEM((nf, B, 8, tf), F32)],
        compiler_params=_cparams(("arbitrary", "arbitrary"), FFN_VMEM_LIMIT),
        name="conv_ffn",
    )(x, g.reshape(1, D), w_up, w_up, conv_w, conv_b[:, None, :], w_down, g_out.reshape(1, D), b0, b1)
    return y, sa[:, -1]


def rope_tables(pos):
    inv = ROPE_THETA ** (-jnp.arange(ROPE_HALF, dtype=F32) / ROPE_HALF)
    ang = pos.astype(F32)[:, None] * inv[None, :]
    c, s = jnp.cos(ang), jnp.sin(ang)
    T = pos.shape[0]
    one = jnp.ones((T, HD - ROPE_DIM), F32)
    zero = jnp.zeros((T, HD - ROPE_DIM), F32)
    zh = jnp.zeros((T, ROPE_HALF), F32)
    cos = jnp.concatenate([c, c, one], axis=1)
    sin_lo = jnp.concatenate([-s, zh, zero], axis=1)
    sin_hi = jnp.concatenate([zh, s, zero], axis=1)
    return cos, sin_lo, sin_hi


def _rope_body(q_ref, ks_ref, kw_ref, c_ref, sl_ref, sh_ref, o_ref):
    c, sl, sh = c_ref[...], sl_ref[...], sh_ref[...]
    h_out = 0
    for x_ref in (q_ref, ks_ref, kw_ref):
        for h in range(x_ref.shape[1] // HD):
            x = x_ref[:, h * HD:(h + 1) * HD]
            o_ref[:, h_out * HD:(h_out + 1) * HD] = (x * c + pltpu.roll(x, HD - ROPE_HALF, 1) * sl
                                                     + pltpu.roll(x, ROPE_HALF, 1) * sh)
            h_out += 1


def nsa_rope(z, tables, T):
    M = z.shape[0]
    tm = _pick_tile(T, (512, 256, 128, 64, 32, 16, 8))
    nt = T // tm
    W = NSA_KVH * HD
    tspec = pl.BlockSpec((tm, HD), lambda i: (i % nt, 0))
    return pl.pallas_call(
        _rope_body,
        grid=(M // tm,),
        in_specs=[pl.BlockSpec((tm, NSA_QW), lambda i: (i, 0)),
                  pl.BlockSpec((tm, W), lambda i: (i, (NSA_QW + 2 * W) // W)),
                  pl.BlockSpec((tm, W), lambda i: (i, (NSA_QW + 4 * W) // W)),
                  tspec, tspec, tspec],
        out_specs=pl.BlockSpec((tm, NSA_QW + 2 * W), lambda i: (i, 0)),
        out_shape=jax.ShapeDtypeStruct((M, NSA_QW + 2 * W), F32),
        compiler_params=_cparams(("parallel",)),
        name="nsa_rope",
    )(z, z, z, *tables)


def _gelu(x):
    return 0.5 * x * (1.0 + jnp.tanh(math.sqrt(2.0 / math.pi) * (x + 0.044715 * x * x * x)))


def _pe_hidden(pe_ref, w1_ref):
    acc = jnp.zeros((8, HD), F32)
    for l in range(CMP_BLOCK):
        row = jnp.broadcast_to(pe_ref[0, l:l + 1, :], (8, HD))
        acc = acc + _dot(row.astype(BF16), w1_ref[0, l].astype(BF16))
    return acc[0:1]


def _cmp_body(x_ref, w1_ref, pe_ref, w2_ref, o_ref, *, nseg):
    parts = []
    for r in range(CMP_RATIO):
        acc = jnp.zeros((nseg, HD), F32)
        for j in range(CMP_STRIDE):
            xj = x_ref[pl.ds(j, nseg, stride=CMP_STRIDE), :]
            acc = acc + _dot(xj.astype(BF16), w1_ref[0, CMP_STRIDE * r + j].astype(BF16))
        parts.append(acc)
    hid = _pe_hidden(pe_ref, w1_ref) + parts[0] + pltpu.roll(parts[1], nseg - 1, 0)
    o_ref[0, 0, 0] = _dot(_gelu(hid).astype(BF16), w2_ref[0].astype(BF16))


def nsa_compress(z, w1, pe, w2, B, T):
    nseg = T // CMP_STRIDE
    return pl.pallas_call(
        functools.partial(_cmp_body, nseg=nseg),
        grid=(2, B, NSA_KVH),
        in_specs=[pl.BlockSpec((T, HD), lambda s, b, k: (b, NSA_QW // HD + NSA_KVH * s + k)),
                  pl.BlockSpec((1, CMP_BLOCK, HD, HD), lambda s, b, k: (s, 0, 0, 0)),
                  pl.BlockSpec((1, CMP_BLOCK, HD), lambda s, b, k: (s, 0, 0)),
                  pl.BlockSpec((1, HD, HD), lambda s, b, k: (s, 0, 0))],
        out_specs=pl.BlockSpec((1, 1, 1, nseg, HD), lambda s, b, k: (s, b, k, 0, 0)),
        out_shape=jax.ShapeDtypeStruct((2, B, NSA_KVH, nseg, HD), F32),
        compiler_params=_cparams(("parallel", "parallel", "parallel")),
        name="nsa_compress",
    )(z, w1, pe, w2)


def _overlap_t(nsp, nseg, n_slc):
    j = lax.broadcasted_iota(jnp.int32, (nsp, 1), 0)
    m = lax.broadcasted_iota(jnp.int32, (1, nseg), 1)
    ov = ((m * CMP_STRIDE <= j * SLC_BLOCK + SLC_BLOCK - 1) & (m * CMP_STRIDE + CMP_BLOCK - 1 >= j * SLC_BLOCK)
          & (m < nseg - CMP_RATIO + 1) & (j < n_slc))
    return ov.astype(F32)


def _rank_rows(score, n):
    j = lax.broadcasted_iota(jnp.int32, (score.shape[0], 1), 0)
    rank = jnp.zeros(score.shape, F32)
    for i in range(n):
        si = score[i:i + 1, :]
        rank = rank + ((si > score) | ((si == score) & (i < j))).astype(F32)
    return rank


def _nsa_attn_body(qraw_ref, qrot_ref, kc_ref, vc_ref, ks_ref, vs_ref, kw_ref, vw_ref, gate_ref, o_ref,
                   ksb_ref, vsb_ref, kwb_ref, vwb_ref, *, T, tq, tk):
    qi = pl.program_id(2)
    t0 = qi * tq
    scale = HD ** -0.5
    nseg = T // CMP_STRIDE
    n_slc = T // SLC_BLOCK
    nsp = max(8, n_slc)
    pos = t0 + lax.broadcasted_iota(jnp.int32, (tq, 1), 0)
    pos_row = t0 + lax.broadcasted_iota(jnp.int32, (1, tq), 1)

    @pl.when(qi == 0)
    def _():
        ksb_ref[:, :HD] = ks_ref[...].astype(BF16)
        ksb_ref[:, HD:] = (lax.broadcasted_iota(jnp.int32, (T, 1), 0) // SLC_BLOCK
                           == lax.broadcasted_iota(jnp.int32, (1, LANE), 1)).astype(BF16)
        for dst, src in zip((vsb_ref, kwb_ref, vwb_ref), (vs_ref, kw_ref, vw_ref)):
            dst[...] = src[...].astype(BF16)

    def heads_on_rows(ref):
        return jnp.concatenate([ref[:, g * HD:(g + 1) * HD] for g in range(NSA_G)], axis=0)

    def per_head(x):
        return jnp.concatenate([x] * NSA_G, axis=0)

    q_raw = (heads_on_rows(qraw_ref) * scale).astype(BF16)
    q_rot = (heads_on_rows(qrot_ref) * scale).astype(BF16)

    kc = kc_ref[0, 0, 0].astype(BF16)
    vc = vc_ref[0, 0, 0].astype(BF16)
    m_idx = lax.broadcasted_iota(jnp.int32, (1, nseg), 1)
    valid = (m_idx * CMP_STRIDE + CMP_BLOCK - 1 <= pos) & (m_idx < nseg - CMP_RATIO + 1)
    s = _dot_nt(q_raw, kc) + per_head(jnp.where(valid, 0.0, NEG))
    e = jnp.exp(s - jnp.max(s, axis=-1, keepdims=True))
    p = e * (1.0 / jnp.sum(e, axis=-1, keepdims=True)) * per_head(valid.astype(F32))
    o_cmp = _dot(p.astype(BF16), vc)
    psum = p[0:tq]
    for g in range(1, NSA_G):
        psum = psum + p[g * tq:(g + 1) * tq]

    imp_t = lax.dot_general(_overlap_t(nsp, nseg, n_slc), psum, (((1,), (1,)), ((), ())),
                            preferred_element_type=F32, precision=lax.Precision.HIGHEST)
    j = lax.broadcasted_iota(jnp.int32, (nsp, 1), 0)
    cur = pos_row // SLC_BLOCK
    forced = (j == 0) | (j == cur) | (j == cur - 1)
    score = jnp.where(j * SLC_BLOCK <= pos_row, imp_t + FORCE_BONUS * forced.astype(F32), -1.0)
    score = jnp.where(j < n_slc, score, -2.0)
    sel_t = ((_rank_rows(score, n_slc) < min(SLC_TOP_N, n_slc)) & (j < n_slc)).astype(F32)
    if nsp < LANE:
        sel_t = jnp.concatenate([sel_t, jnp.zeros((LANE - nsp, tq), F32)], axis=0)
    sel = sel_t.T
    def update(s, v, carry):
        m_old, l_old, acc = carry
        m_new = jnp.maximum(m_old, jnp.max(s, axis=-1, keepdims=True))
        alpha = jnp.exp(m_old - m_new)
        p = jnp.exp(s - m_new)
        return (m_new, alpha * l_old + jnp.sum(p, axis=-1, keepdims=True), alpha * acc + _dot(p.astype(BF16), v))

    R = NSA_G * tq
    init = (jnp.full((R, 1), NEG, F32), jnp.zeros((R, 1), F32), jnp.zeros((R, HD), F32))

    q_aug = jnp.concatenate([q_rot, per_head(((sel - 1.0) * -NEG).astype(BF16))], axis=1)

    def slc_scores(kt):
        k0 = pl.multiple_of(kt * tk, tk)
        return _dot_nt(q_aug, ksb_ref[pl.ds(k0, tk), :]), vsb_ref[pl.ds(k0, tk), :]

    def slc_step(kt, carry):
        s, v = slc_scores(kt)
        return update(s, v, carry)

    kt_diag = t0 // tk
    slc = lax.fori_loop(0, kt_diag, slc_step, init)
    s, v = slc_scores(kt_diag)
    kpos = kt_diag * tk + lax.broadcasted_iota(jnp.int32, (1, tk), 1)
    slc = update(s + per_head(jnp.where(kpos <= pos, 0.0, NEG)), v, slc)

    nkw = min(WINDOW + tq, T)
    k0w = pl.multiple_of(jnp.clip(t0 - WINDOW, 0, T - nkw), tq)
    d = pos - (k0w + lax.broadcasted_iota(jnp.int32, (1, nkw), 1))
    win = update(_dot_nt(q_rot, kwb_ref[pl.ds(k0w, nkw), :]) + per_head(jnp.where((d >= 0) & (d < WINDOW), 0.0, NEG)),
                 vwb_ref[pl.ds(k0w, nkw), :], init)

    gate = _sigmoid(gate_ref[...])
    o_slc = slc[2] / slc[1]
    o_win = win[2] / win[1]
    for g in range(NSA_G):
        rows = slice(g * tq, (g + 1) * tq)
        o_ref[:, g * HD:(g + 1) * HD] = (gate[:, 3 * g:3 * g + 1] * o_cmp[rows]
                                         + gate[:, 3 * g + 1:3 * g + 2] * o_slc[rows]
                                         + gate[:, 3 * g + 2:3 * g + 3] * o_win[rows])


def nsa_attn_prefill(z, rot, cmp, gates, B, T):
    assert T % LANE == 0 and T // SLC_BLOCK <= LANE, "selection blocks must fit one lane group"
    tq = LANE
    tk = _pick_tile(T, (1024, 512, 256, 128))
    nq = T // tq
    nseg = T // CMP_STRIDE
    W = NSA_G * HD
    qspec = pl.BlockSpec((tq, W), lambda b, k, q: (b * nq + q, k))
    cspec = lambda s: pl.BlockSpec((1, 1, 1, nseg, HD), lambda b, k, q: (s, b, k, 0, 0))
    kvspec = lambda c0: pl.BlockSpec((T, HD), lambda b, k, q: (b, c0 + k))
    return pl.pallas_call(
        functools.partial(_nsa_attn_body, T=T, tq=tq, tk=tk),
        grid=(B, NSA_KVH, nq),
        in_specs=[qspec, qspec, cspec(0), cspec(1),
                  kvspec(NSA_QW // HD), kvspec((NSA_QW + 3 * W) // HD),
                  kvspec((NSA_QW + W) // HD), kvspec((NSA_QW + 5 * W) // HD),
                  pl.BlockSpec((tq, LANE), lambda b, k, q: (b * nq + q, k))],
        out_specs=qspec,
        out_shape=jax.ShapeDtypeStruct((B * T, NSA_QW), F32),
        scratch_shapes=[pltpu.VMEM((T, 2 * HD), BF16)] + [pltpu.VMEM((T, HD), BF16) for _ in range(3)],
        compiler_params=_cparams(("parallel", "parallel", "arbitrary")),
        name="nsa_attn",
    )(z, rot, cmp, cmp, rot, z, rot, z, gates)


def _nsa_gate_weight(w_in, a):
    K = w_in.shape[1]
    wg = w_in[a, :, NSA_MAIN:].reshape(K, NSA_KVH, 3 * NSA_G)
    return jnp.pad(wg, ((0, 0), (0, 0), (0, LANE - 3 * NSA_G))).reshape(K, NSA_KVH * LANE)


def _nsa_rows_body(*refs, n_layers):
    o_ref = refs[-1]
    per_layer = 4 * NSA_KVH
    for a in range(n_layers):
        @pl.when(pl.program_id(1) == a)
        def _(a=a):
            for h, src in enumerate(refs[a * per_layer:(a + 1) * per_layer]):
                o_ref[:, h, :] = src[...]


def nsa_rows(zs, rots, T):
    M = zs[0].shape[0]
    n_layers = len(zs)
    tm = _pick_tile(T, (512, 256, 128, 64, 32, 16, 8))
    nh = NSA_KVH
    q0 = NSA_QW // HD
    cols = []
    for z, rot in zip(zs, rots):
        cols += ([(z, q0 + h) for h in range(2 * nh)] + [(rot, q0 + h) for h in range(nh)]
                 + [(z, q0 + 3 * nh + h) for h in range(nh)])
    in_specs = [pl.BlockSpec((tm, HD), functools.partial(lambda i, a, c: (i, c), c=c)) for _, c in cols]
    return pl.pallas_call(
        functools.partial(_nsa_rows_body, n_layers=n_layers),
        grid=(M // tm, n_layers),
        in_specs=in_specs,
        out_specs=pl.BlockSpec((tm, None, 4 * NSA_KVH, HD), lambda i, a: (i, a, 0, 0)),
        out_shape=jax.ShapeDtypeStruct((M, n_layers, 4 * NSA_KVH, HD), F32),
        compiler_params=_cparams(("parallel", "arbitrary")),
        name="nsa_rows",
    )(*[arr for arr, _ in cols])


def nsa_layer_prefill(x, g, w_in, w1, pe, w2, w_out, a, tables, B, T):
    z = proj(x, g, jnp.swapaxes(w_in, 1, 2), n_out=NSA_MAIN, layer=a, w_rows_out=True)
    gates = proj(x, g, _nsa_gate_weight(w_in, a))
    rot = nsa_rope(z, tables, T)
    cmp = nsa_compress(z, w1, pe, w2, B, T)
    mix = nsa_attn_prefill(z, rot, cmp, gates, B, T)
    y = proj(mix, g, w_out, residual=x, norm=False, layer=a)
    W = NSA_KVH * HD
    Tw = min(WINDOW, T)
    win = jnp.concatenate([rot.reshape(B, T, -1)[:, T - Tw:, NSA_QW + W:], z.reshape(B, T, -1)[:, T - Tw:, NSA_QW + 5 * W:]],
                          axis=2).reshape(B, Tw, 2, NSA_KVH, HD)
    return y, (z, rot), win


def _round_robin(gens):
    while gens:
        alive = []
        for gen in gens:
            try:
                next(gen)
                alive.append(gen)
            except StopIteration:
                pass
        gens = alive


def _tri(c):
    return (lax.broadcasted_iota(jnp.int32, (c, 1), 0) >= lax.broadcasted_iota(jnp.int32, (1, c), 1)).astype(F32)


def _hgrn_body(q_ref, f_ref, i_ref, gz_ref, lb_ref, gn_ref, s0_ref, o_ref, s_ref, st_ref,
               *, C, SC, HP, t_valid, nt, layer):
    t = pl.program_id(2)
    Tt = q_ref.shape[0]

    @pl.when(t == 0)
    def _():
        for hp in range(HP):
            st_ref[hp] = s0_ref[0, hp].T

    lbp = lb_ref[...]
    e = jnp.exp(lbp - jnp.max(lbp, axis=0, keepdims=True))
    lb_all = jnp.sum(e[1:layer + 1], axis=0, keepdims=True) / jnp.sum(e, axis=0, keepdims=True)
    gn_all = gn_ref[...]
    tri = _tri(C)
    row8 = lax.broadcasted_iota(jnp.int32, (8, 1), 0)

    def head_chunk(hp, r0):
        cols = slice(hp * HD, (hp + 1) * HD)
        lb, gn = lb_all[:, cols], gn_all[:, cols]
        qz = q_ref[pl.ds(r0, C), cols]
        f = lb + (1.0 - lb) * _sigmoid(f_ref[pl.ds(r0, C), cols])
        v = i_ref[pl.ds(r0, C), cols]
        gz = gz_ref[pl.ds(r0, C), cols]
        k = 1.0 - f
        lf = jnp.log(f)
        if t_valid is not None:
            ok = (t * Tt + r0 + lax.broadcasted_iota(jnp.int32, (C, 1), 0)) < t_valid
            k = jnp.where(ok, k, 0.0)
            lf = jnp.where(ok, lf, 0.0)
        q = qz * _sigmoid(qz)
        b = _dot_hi(tri, lf)
        yield
        st = st_ref[hp]
        vb = v.astype(BF16)
        out = _dot_nt((q * jnp.exp(b)).astype(BF16), st.astype(BF16))
        bl = b[C - 1:C]
        st_ref[hp] = st * jnp.exp(bl) + _dot_tn(vb, (k * jnp.exp(bl - b)).astype(BF16))
        atts = []
        for I in range(1, C // SC):
            r = I * SC
            beta = b[r - 1:r]
            qt = q[r:r + SC] * jnp.exp(b[r:r + SC] - beta)
            kt = k[0:r] * jnp.exp(beta - b[0:r])
            atts.append(_dot_nt(qt.astype(BF16), kt.astype(BF16)))
        yield
        offdiag = [_dot(att.astype(BF16), vb[0:I * SC]) for I, att in enumerate(atts, 1)]
        pieces = []
        for I in range(C // SC):
            r = I * SC
            bI, qI, kI, vI = b[r:r + SC], q[r:r + SC], k[r:r + SC], v[r:r + SC]
            groups = [jnp.zeros((8, HD), F32) for _ in range(SC // 8)]
            for s in range(SC):
                for gi in range(s // 8, SC // 8):
                    rows = slice(gi * 8, gi * 8 + 8)
                    d = bI[rows] - bI[s:s + 1]
                    if gi == s // 8:
                        d = jnp.where(row8 >= s % 8, d, NEG)
                    att = jnp.sum(qI[rows] * jnp.exp(d) * kI[s:s + 1], axis=-1, keepdims=True)
                    groups[gi] = groups[gi] + att * vI[s:s + 1]
            pieces.extend(groups)
        yield
        intra = jnp.concatenate(pieces, axis=0)
        if offdiag:
            intra = intra + jnp.concatenate([jnp.zeros((SC, HD), F32)] + offdiag, axis=0)
        out = out + intra
        o_ref[pl.ds(r0, C), cols] = _rms(out, gn) * (gz * _sigmoid(gz))

    def chunk(c, carry):
        r0 = pl.multiple_of(c * C, C)
        _round_robin([head_chunk(hp, r0) for hp in range(HP)])
        return carry

    lax.fori_loop(0, Tt // C, chunk, 0)

    @pl.when(t == nt - 1)
    def _():
        for hp in range(HP):
            s_ref[0, hp] = st_ref[hp].T


def hgrn_scan(z, lb, layer, gn, s0, B, T, t_valid=None):
    depth = lb.shape[0]
    C = min(CHUNK, T)
    SC = min(SUB, C)
    Tt = _pick_tile(T, (512, 256, 128, 64, 32, 16))
    nt = T // Tt
    HP = HG_HEADS_PER_STEP
    ng = HG_HEADS // HP
    W = HP * HD
    zspec = lambda part: pl.BlockSpec((Tt, W), lambda b, h, t: (b * nt + t, part * ng + h))
    vspec = pl.BlockSpec((1, W), lambda b, h, t: (0, h))
    lspec = pl.BlockSpec((depth, W), lambda b, h, t: (0, h))
    sspec = pl.BlockSpec((1, HP, HD, HD), lambda b, h, t: (b, h, 0, 0))
    return pl.pallas_call(
        functools.partial(_hgrn_body, C=C, SC=SC, HP=HP, t_valid=t_valid, nt=nt, layer=layer),
        grid=(B, ng, nt),
        in_specs=[zspec(0), zspec(1), zspec(2), zspec(3), lspec, vspec, sspec],
        out_specs=[pl.BlockSpec((Tt, W), lambda b, h, t: (b * nt + t, h)), sspec],
        out_shape=[jax.ShapeDtypeStruct((B * T, HG_W), F32), jax.ShapeDtypeStruct((B, HG_HEADS, HD, HD), F32)],
        scratch_shapes=[pltpu.VMEM((HP, HD, HD), F32)],
        compiler_params=_cparams(("parallel", "parallel", "arbitrary")),
        name="hgrn_scan",
    )(z, z, z, z, lb, gn.reshape(1, HG_W), s0)


def _mlstm_body(q_ref, k_ref, v_ref, og_ref, gt_ref, bg_ref, c0_ref, n0_ref, m0_ref,
                y_ref, c_ref, n_ref, m_ref, cs_ref, ns_ref, ms_ref, *, C, HP, t_valid, nt):
    t = pl.program_id(2)
    Tt = q_ref.shape[0]

    @pl.when(t == 0)
    def _():
        cs_ref[...] = c0_ref[0]
        ns_ref[...] = n0_ref[0]
        ms_ref[...] = m0_ref[0]

    tri_b = lax.broadcasted_iota(jnp.int32, (C, 1), 0) >= lax.broadcasted_iota(jnp.int32, (1, C), 1)
    upper = (lax.broadcasted_iota(jnp.int32, (C, 1), 0) <= lax.broadcasted_iota(jnp.int32, (1, C), 1)).astype(F32)
    eye = (lax.broadcasted_iota(jnp.int32, (C, 1), 0) == lax.broadcasted_iota(jnp.int32, (1, C), 1)).astype(F32)
    tri = tri_b.astype(F32)
    ones = jnp.ones((C, C), F32)
    lane = lax.broadcasted_iota(jnp.int32, (1, LANE), 1)
    bg = bg_ref[...]

    def head_chunk(hp, r0, g):
        h = pl.program_id(1) * HP + hp
        kcols = slice(hp * ML_DK, (hp + 1) * ML_DK)
        vcols = slice(hp * ML_DV, (hp + 1) * ML_DV)
        ig = jnp.sum(jnp.where(lane == h, g, 0.0), axis=-1, keepdims=True)
        fg = jnp.sum(jnp.where(lane == ML_HEADS + h, g, 0.0), axis=-1, keepdims=True)
        lf = jnp.minimum(fg, 0.0) - jnp.log(1.0 + jnp.exp(-jnp.abs(fg)))
        if t_valid is not None:
            ok = (t * Tt + r0 + lax.broadcasted_iota(jnp.int32, (C, 1), 0)) < t_valid
            lf = jnp.where(ok, lf, 0.0)
            ig = jnp.where(ok, ig, NEG)
        lf_b = jnp.broadcast_to(lf, (C, C))
        ig_b = jnp.broadcast_to(ig, (C, C))
        b_col = _dot_hi(tri, lf_b)
        b_row_i = _dot_hi(ones, lf_b * upper - ig_b * eye)
        q = q_ref[pl.ds(r0, C), kcols]
        ks = k_ref[pl.ds(r0, C), kcols] * (ML_DK ** -0.5)
        v = v_ref[pl.ds(r0, C), vcols]
        qb, kb, vb = q.astype(BF16), ks.astype(BF16), v.astype(BF16)
        cs = cs_ref[hp]
        ns = ns_ref[hp]
        qk = _dot_nt(qb, kb)
        inter = _dot(qb, cs.astype(BF16))
        yield
        logw = jnp.where(tri_b, b_col - b_row_i, NEG)
        m_old = ms_ref[hp][:, 0:1]
        b1 = b_col[:, 0:1]
        m_prev = b1 + m_old
        m_t = jnp.maximum(m_prev, jnp.max(logw, axis=-1, keepdims=True))
        w0 = jnp.exp(m_prev - m_t)
        sqk = qk * jnp.exp(logw - m_t)
        intra = _dot(sqk.astype(BF16), vb)
        m_new = m_t[C - 1:C]
        b_last = b1[C - 1:C]
        wk = jnp.exp(b_last - b1 + ig - m_new)
        decay = jnp.exp(b_last + m_old - m_new)
        kw = ks * wk
        cs_ref[hp] = decay * cs + _dot_tn(kw.astype(BF16), vb)
        ns_ref[hp] = decay * ns + jnp.sum(kw, axis=0, keepdims=True)
        ms_ref[hp] = jnp.broadcast_to(m_new, (1, LANE))
        den = w0 * jnp.sum(q * ns, axis=-1, keepdims=True) + jnp.sum(sqk, axis=-1, keepdims=True)
        yield
        hc = (w0 * inter + intra) / jnp.maximum(jnp.abs(den), jnp.exp(-m_t))
        y_ref[pl.ds(r0, C), vcols] = _sigmoid(og_ref[pl.ds(r0, C), vcols]) * hc

    def chunk(c, carry):
        r0 = pl.multiple_of(c * C, C)
        g = gt_ref[pl.ds(r0, C), :] + bg
        g = ML_GATE_CAP * jnp.tanh(g / ML_GATE_CAP)
        _round_robin([head_chunk(hp, r0, g) for hp in range(HP)])
        return carry

    lax.fori_loop(0, Tt // C, chunk, 0)

    @pl.when(t == nt - 1)
    def _():
        c_ref[0] = cs_ref[...]
        n_ref[0] = ns_ref[...]
        m_ref[0] = ms_ref[...]


def mlstm_scan(z, gates, b_gate, c0, n0, m0, B, T, t_valid=None):
    C = min(CHUNK, T)
    Tt = _pick_tile(T, (512, 256, 128, 64, 32, 16))
    nt = T // Tt
    H = ML_HEADS
    row = lambda b, t: b * nt + t
    bg = jnp.pad(b_gate, (0, LANE - 2 * H)).reshape(1, LANE)
    n0 = n0.reshape(B, H, 1, ML_DK)
    m0 = jnp.broadcast_to(m0[:, :, None, None], (B, H, 1, LANE))
    HP = ML_HEADS_PER_STEP
    ng = H // HP
    kw, vw = HP * ML_DK, HP * ML_DV
    vec = lambda w: pl.BlockSpec((1, HP, 1, w), lambda b, h, t: (b, h, 0, 0))
    cspec = pl.BlockSpec((1, HP, ML_DK, ML_DV), lambda b, h, t: (b, h, 0, 0))
    y, c, n, m = pl.pallas_call(
        functools.partial(_mlstm_body, C=C, HP=HP, t_valid=t_valid, nt=nt),
        grid=(B, ng, nt),
        in_specs=[pl.BlockSpec((Tt, kw), lambda b, h, t: (row(b, t), h)),
                  pl.BlockSpec((Tt, kw), lambda b, h, t: (row(b, t), ng + h)),
                  pl.BlockSpec((Tt, vw), lambda b, h, t: (row(b, t), 2 * ML_QK // vw + h)),
                  pl.BlockSpec((Tt, vw), lambda b, h, t: (row(b, t), (2 * ML_QK + ML_VW) // vw + h)),
                  pl.BlockSpec((Tt, LANE), lambda b, h, t: (row(b, t), 0)),
                  pl.BlockSpec((1, LANE), lambda b, h, t: (0, 0)),
                  cspec, vec(ML_DK), vec(LANE)],
        out_specs=[pl.BlockSpec((Tt, vw), lambda b, h, t: (row(b, t), h)), cspec, vec(ML_DK), vec(LANE)],
        out_shape=[jax.ShapeDtypeStruct((B * T, ML_VW), F32), jax.ShapeDtypeStruct((B, H, ML_DK, ML_DV), F32),
                   jax.ShapeDtypeStruct((B, H, 1, ML_DK), F32), jax.ShapeDtypeStruct((B, H, 1, LANE), F32)],
        scratch_shapes=[pltpu.VMEM((HP, ML_DK, ML_DV), F32), pltpu.VMEM((HP, 1, ML_DK), F32),
                        pltpu.VMEM((HP, 1, LANE), F32)],
        compiler_params=_cparams(("parallel", "parallel", "arbitrary")),
        name="mlstm_scan",
    )(z, z, z, z, gates, bg, c0, n0, m0)
    return y, c, n.reshape(B, H, ML_DK), m[:, :, 0, 0]


PAGES_PER_STEP = 16
SEG_PER_PAGE = PAGE_SIZE // CMP_STRIDE
TILE_ROWS = 2 * NSA_KVH


def _cmp_paged_body(pt_ref, *refs):
    P = PAGES_PER_STEP
    x_refs, w_ref, o_ref, lhs_ref = refs[:P], refs[P], refs[P + 1], refs[P + 2]
    rows = SEG_PER_PAGE * TILE_ROWS
    for p in range(P):
        for j in range(CMP_STRIDE):
            xj = x_refs[p][pl.ds(j, SEG_PER_PAGE, stride=CMP_STRIDE)]
            lhs_ref[p * rows:(p + 1) * rows, j * HD:(j + 1) * HD] = xj.reshape(rows, HD).astype(BF16)
    res = _dot(lhs_ref[...], w_ref[...])
    c = lax.broadcasted_iota(jnp.int32, (P * rows, 1), 0) % TILE_ROWS
    sel = jnp.where(c < NSA_KVH, res[:, :2 * HD], res[:, 2 * HD:])
    o_ref[0, 0] = sel[:, :HD]
    o_ref[0, 1] = sel[:, HD:]


def nsa_compress_paged(cache4, page_table, w1, a):
    B, n_pages = page_table.shape
    P = PAGES_PER_STEP
    rows = SEG_PER_PAGE * TILE_ROWS
    w_cat = w1.reshape(2, CMP_RATIO, CMP_STRIDE, HD, HD).transpose(2, 3, 0, 1, 4).reshape(CMP_STRIDE * HD, 4 * HD)
    xspec = lambda i: pl.BlockSpec((PAGE_SIZE, None, TILE_ROWS, HD),
                                   lambda b, s, pt: (pt[b, s * P + i], 2 * a, 0, 0))
    grid_spec = pltpu.PrefetchScalarGridSpec(
        num_scalar_prefetch=1,
        grid=(B, n_pages // P),
        in_specs=[xspec(i) for i in range(P)] + [pl.BlockSpec((CMP_STRIDE * HD, 4 * HD), lambda b, s, pt: (0, 0))],
        out_specs=pl.BlockSpec((1, CMP_RATIO, P * rows, HD), lambda b, s, pt: (b, 0, s, 0)),
        scratch_shapes=[pltpu.VMEM((P * rows, CMP_STRIDE * HD), BF16)])
    return pl.pallas_call(
        _cmp_paged_body,
        grid_spec=grid_spec,
        out_shape=jax.ShapeDtypeStruct((B, CMP_RATIO, n_pages * rows, HD), F32),
        compiler_params=_cparams(("parallel", "arbitrary")),
        name="nsa_compress_paged",
    )(page_table, *([cache4] * P), w_cat.astype(BF16))


def _nsa_dec_cmp_body(part_ref, w1_ref, pe_ref, w2_ref, q_ref, o_ref, idx_ref, *, n_seg, n_slc, pos):
    nsp = -(-n_slc // LANE) * LANE
    scale = HD ** -0.5
    pe_hid = [_pe_hidden(pe_ref.at[pl.ds(s, 1)], w1_ref.at[pl.ds(s, 1)]) for s in range(2)]
    q = q_ref[0].astype(BF16)
    head_kv = lax.broadcasted_iota(jnp.int32, (NSA_HEADS, 1), 0) // NSA_G
    col_m = lax.broadcasted_iota(jnp.int32, (1, n_seg), 1)
    valid_m = (col_m < n_seg - CMP_RATIO + 1) & (col_m * CMP_STRIDE + CMP_BLOCK - 1 <= pos)

    def tokens(slot, kh):
        rows = pl.ds(slot * NSA_KVH + kh, n_seg, stride=TILE_ROWS)
        hid = part_ref[0, 0, rows, :] + pltpu.roll(part_ref[0, 1, rows, :], n_seg - 1, 0) + pe_hid[slot]
        return _dot(_gelu(hid).astype(BF16), w2_ref[slot].astype(BF16)).astype(BF16)

    o_cmp = jnp.zeros((NSA_HEADS, HD), F32)
    psum_rows = []
    for kh in range(NSA_KVH):
        valid = valid_m & (head_kv == kh)
        s = jnp.where(valid, _dot_nt(q, tokens(0, kh)) * scale, NEG)
        e = jnp.exp(s - jnp.max(s, axis=-1, keepdims=True))
        prob = jnp.where(valid, e / jnp.sum(e, axis=-1, keepdims=True), 0.0)
        o_cmp = o_cmp + _dot(prob.astype(BF16), tokens(1, kh))
        psum_rows.append(jnp.sum(prob, axis=0, keepdims=True))
    o_ref[0] = o_cmp
    psum = jnp.concatenate(psum_rows + [jnp.zeros((8 - NSA_KVH, n_seg), F32)], axis=0)
    m_r = lax.broadcasted_iota(jnp.int32, (n_seg, 1), 0)
    j = lax.broadcasted_iota(jnp.int32, (1, nsp), 1)
    ov = ((m_r * CMP_STRIDE <= j * SLC_BLOCK + SLC_BLOCK - 1) & (m_r * CMP_STRIDE + CMP_BLOCK - 1 >= j * SLC_BLOCK)
          & (m_r < n_seg - CMP_RATIO + 1) & (j < n_slc)).astype(F32)
    imp = _dot_hi(psum, ov)
    cur = pos // SLC_BLOCK
    forced = (j == 0) | (j == cur) | (j == cur - 1)
    score = jnp.where(j * SLC_BLOCK <= pos, imp + FORCE_BONUS * forced.astype(F32), -1.0)
    score = jnp.where(j < n_slc, score, -2.0)
    jf = j.astype(F32)
    slot = lax.broadcasted_iota(jnp.int32, (1, LANE), 1)
    picked = jnp.zeros((8, LANE), F32)
    for it in range(min(SLC_TOP_N, n_slc)):
        best = jnp.max(score, axis=-1, keepdims=True)
        ix = jnp.min(jnp.where(score == best, jf, float(nsp)), axis=-1, keepdims=True)
        picked = jnp.where(slot == it, ix, picked)
        score = jnp.where(jf == ix, -3.0, score)
    idx_ref[0] = picked.astype(jnp.int32)


def nsa_dec_cmp(part, w1, pe, w2, q_raw, n_seg, n_slc, pos):
    B = part.shape[0]
    R = n_seg * TILE_ROWS
    full = lambda shape: pl.BlockSpec(shape, lambda b: (0,) * len(shape))
    return pl.pallas_call(
        functools.partial(_nsa_dec_cmp_body, n_seg=n_seg, n_slc=n_slc, pos=pos),
        grid=(B,),
        in_specs=[pl.BlockSpec((1, CMP_RATIO, R, HD), lambda b: (b, 0, 0, 0)),
                  full((2, CMP_BLOCK, HD, HD)), full((2, CMP_BLOCK, HD)), full((2, HD, HD)),
                  pl.BlockSpec((1, NSA_HEADS, HD), lambda b: (b, 0, 0))],
        out_specs=[pl.BlockSpec((1, NSA_HEADS, HD), lambda b: (b, 0, 0)),
                   pl.BlockSpec((1, 8, LANE), lambda b: (b, 0, 0))],
        out_shape=[jax.ShapeDtypeStruct((B, NSA_HEADS, HD), F32), jax.ShapeDtypeStruct((B, 8, LANE), jnp.int32)],
        compiler_params=_cparams(("parallel",)),
        name="nsa_dec_cmp",
    )(part, w1, pe, w2, q_raw)


def _nsa_dec_attn_body(pt_ref, idx_ref, *refs, n_cache_blocks, nsel, win_skip):
    blk_refs = refs[:nsel]
    q_ref, new_ref, win_ref, ocmp_ref, gate_ref, o_ref = refs[nsel:]
    b = pl.program_id(0)
    kh = pl.program_id(1)
    scale = HD ** -0.5
    q = q_ref[0, 0].astype(BF16)
    new = new_ref[0, 0]
    s_new = _dot_nt(q, new.astype(BF16)) * scale

    def attend(x, msk, s_own, v_own):
        s = [jnp.where(m, _dot_nt(q, xi) * scale, NEG) for xi, m in zip(x, msk)]
        mx = s_own
        for si in s:
            mx = jnp.maximum(mx, jnp.max(si, axis=-1, keepdims=True))
        p_own = jnp.exp(s_own - mx)
        num = p_own * v_own
        den = p_own
        for si, xi in zip(s, x):
            p = jnp.exp(si - mx)
            num = num + _dot(pltpu.roll(p, NSA_KVH, 1).astype(BF16), xi)
            den = den + jnp.sum(p, axis=-1, keepdims=True)
        return num / den

    col_c = lax.broadcasted_iota(jnp.int32, (1, SLC_BLOCK * TILE_ROWS), 1) % TILE_ROWS
    xs = [r[...].reshape(SLC_BLOCK * TILE_ROWS, HD).astype(BF16) for r in blk_refs]
    ms = [(col_c == kh) & (idx_ref[b, kh, n] < n_cache_blocks) for n in range(nsel)]
    o_slc = attend(xs, ms, s_new[:, 0:1], new[1:2])

    n_win = win_ref.shape[1]
    xw = win_ref[0].reshape(n_win * TILE_ROWS, HD).astype(BF16)
    colw = lax.broadcasted_iota(jnp.int32, (1, n_win * TILE_ROWS), 1)
    mw = ((colw % TILE_ROWS) == kh) & ((colw // TILE_ROWS) >= win_skip)
    o_win = attend([xw], [mw], s_new[:, 2:3], new[3:4])

    gate = _sigmoid(gate_ref[0, 0])
    lane = lax.broadcasted_iota(jnp.int32, (1, LANE), 1)
    row = lax.broadcasted_iota(jnp.int32, (8, 1), 0)
    pick = lambda c: jnp.sum(jnp.where(lane == 3 * row + c, gate, 0.0), axis=-1, keepdims=True)
    o_ref[0, 0] = pick(0) * ocmp_ref[0, 0] + pick(1) * o_slc + pick(2) * o_win


def nsa_dec_attn(cache5, win4, page_table, idx, q_rot, new_rows, o_cmp, gates, a, n_cache_blocks, win_skip):
    B, _, nsel = idx.shape
    n_win = win4.shape[1]

    def blk_spec(n):
        def blk_map(b, k, pt, ix):
            j = jnp.minimum(ix[b, k, n], n_cache_blocks - 1)
            return (pt[b, j // 2] * 2 + j % 2, 0, 2 * a + 1, 0, 0)
        return pl.BlockSpec((None, SLC_BLOCK, None, TILE_ROWS, HD), blk_map)

    grp = pl.BlockSpec((1, 1, 8, HD), lambda b, k, pt, ix: (b, k, 0, 0))
    grid_spec = pltpu.PrefetchScalarGridSpec(
        num_scalar_prefetch=2,
        grid=(B, NSA_KVH),
        in_specs=[blk_spec(n) for n in range(nsel)] + [
            grp, grp, pl.BlockSpec((1, n_win, TILE_ROWS, HD), lambda b, k, pt, ix: (a * B + b, 0, 0, 0)), grp, grp],
        out_specs=grp)
    return pl.pallas_call(
        functools.partial(_nsa_dec_attn_body, n_cache_blocks=n_cache_blocks, nsel=nsel, win_skip=win_skip),
        grid_spec=grid_spec,
        out_shape=jax.ShapeDtypeStruct((B, NSA_KVH, 8, HD), F32),
        compiler_params=_cparams(("parallel", "arbitrary")),
        name="nsa_dec_attn",
    )(page_table, idx, *([cache5] * nsel), q_rot, new_rows, win4, o_cmp, gates)


def _group_rows(x, B):
    x = x.reshape(B, NSA_KVH, NSA_G, HD)
    return jnp.pad(x, ((0, 0), (0, 0), (0, 8 - NSA_G), (0, 0)))


def nsa_layer_decode(x, g, w_in, w1, pe, w2, w_out, cache, win_state, page_table, a, B):
    R = x.shape[0]
    n_pool = cache.shape[0]
    n_pages = page_table.shape[1]
    past = n_pages * PAGE_SIZE
    n_seg = (past + 1) // CMP_STRIDE
    n_slc = -(-(past + 1) // SLC_BLOCK)
    W = NSA_KVH * HD
    z = proj(x, g, jnp.swapaxes(w_in, 1, 2), n_out=NSA_MAIN, layer=a, w_rows_out=True)
    gates = proj(x, g, _nsa_gate_weight(w_in, a))
    rot = nsa_rope(z, rope_tables(jnp.full((R,), past, jnp.int32)), R)
    part = nsa_compress_paged(cache.reshape(n_pool * PAGE_SIZE, 4, TILE_ROWS, HD), page_table, w1, a)
    o_cmp, idx = nsa_dec_cmp(part, w1, pe, w2, z[:B, :NSA_QW].reshape(B, NSA_HEADS, HD), n_seg, n_slc, past)
    nsel = min(SLC_TOP_N, n_slc)
    kv = z[:B, NSA_QW:].reshape(B, 6, NSA_KVH, HD)
    ks_new = rot[:B, NSA_QW:NSA_QW + W].reshape(B, NSA_KVH, HD)
    kw_new = rot[:B, NSA_QW + W:].reshape(B, NSA_KVH, HD)
    new_rows = jnp.stack([ks_new, kv[:, 3], kw_new, kv[:, 5]], axis=2)
    new_rows = jnp.pad(new_rows, ((0, 0), (0, 0), (0, 4), (0, 0)))
    n_win = win_state.shape[2]
    mix = nsa_dec_attn(cache.reshape(n_pool * 2, SLC_BLOCK, 4, TILE_ROWS, HD),
                       win_state.reshape(-1, n_win, TILE_ROWS, HD), page_table, idx[:, :NSA_KVH, :nsel],
                       _group_rows(rot[:B, :NSA_QW], B), new_rows, _group_rows(o_cmp.reshape(B, NSA_QW), B),
                       jnp.broadcast_to(gates[:B].reshape(B, NSA_KVH, 1, LANE), (B, NSA_KVH, 8, LANE)),
                       a, past // SLC_BLOCK, max(n_win + 1 - WINDOW, 0))
    mix = jnp.pad(mix[:, :, :NSA_G].reshape(B, NSA_QW), ((0, R - B), (0, 0)))
    y = proj(mix, g, w_out, residual=x, norm=False, layer=a)
    rows = jnp.stack([kv[:, 0], kv[:, 1], ks_new, kv[:, 3]], axis=1)[:, None]
    win_new = jnp.stack([kw_new, kv[:, 5]], axis=1)[:, None]
    buf = jnp.concatenate([win_state[a], win_new], axis=1)
    return y, rows, buf[:, -min(WINDOW, n_win + 1):]


def kernel(x_prompt, x_sample, cache_nsa_kv, state_nsa_win, state_hgrn, state_mlstm_c, state_mlstm_n, state_mlstm_m, state_ffn_conv, page_table, norm_mix, norm_ffn, norm_out, nsa_w_in, nsa_cmp_w1, nsa_cmp_pe, nsa_cmp_w2, nsa_w_out, hgrn_w_in, hgrn_lb, hgrn_norm, hgrn_w_out, ml_w_in, ml_b_gate, ml_w_out, ffn_w_up, ffn_conv_w, ffn_conv_b, ffn_w_down):
    P = dict(norm_mix=norm_mix, norm_ffn=norm_ffn, norm_out=norm_out, nsa_w_in=nsa_w_in, nsa_cmp_w1=nsa_cmp_w1,
             nsa_cmp_pe=nsa_cmp_pe, nsa_cmp_w2=nsa_cmp_w2, nsa_w_out=nsa_w_out, hgrn_w_in=hgrn_w_in,
             hgrn_lb=hgrn_lb, hgrn_norm=hgrn_norm, hgrn_w_out=hgrn_w_out, ml_w_in=ml_w_in, ml_b_gate=ml_b_gate,
             ml_w_out=ml_w_out, ffn_w_up=ffn_w_up, ffn_conv_w=ffn_conv_w, ffn_conv_b=ffn_conv_b,
             ffn_w_down=ffn_w_down)
    past = dict(cache=cache_nsa_kv, win=state_nsa_win, hgrn=state_hgrn, c=state_mlstm_c, n=state_mlstm_n,
                m=state_mlstm_m, ffn=state_ffn_conv, page_table=page_table)
    yp, sp = _trunk(x_prompt, P, None)
    ys, ss = _trunk(x_sample, P, past)
    return (yp, ys, sp['nsa_kv'], ss['nsa_kv'], sp['nsa_win'], ss['nsa_win'], sp['hgrn'], ss['hgrn'],
            sp['c'], ss['c'], sp['n'], ss['n'], sp['m'], ss['m'], sp['ffn'], ss['ffn'])


DEC_ROWS = 16


def _spread(z, B, T):
    return jnp.zeros((B, T, z.shape[1]), F32).at[:, 0].set(z[:B]).reshape(B * T, z.shape[1])


def _gather_first(o, B, T, R):
    return jnp.pad(o.reshape(B, T, -1)[:, 0], ((0, R - B), (0, 0)))


def _trunk(x_in, P, past):
    B, T, D = x_in.shape
    decode = past is not None
    if decode:
        R = DEC_ROWS
        x = jnp.pad(x_in.reshape(B, D), ((0, R - B), (0, 0)))
        Ts = DEC_ROWS
    else:
        x = x_in.reshape(B * T, D)
        tables = rope_tables(jnp.arange(T, dtype=jnp.int32))
    rows_l, win_l, hg_l, c_l, n_l, m_l, ffn_l = [], [], [], [], [], [], []
    for i in range(DEPTH):
        kind, a = i % N_MIXERS, i // N_MIXERS
        g = P['norm_mix'][i]
        if kind == 0:
            args = (x, g, P['nsa_w_in'], P['nsa_cmp_w1'][a], P['nsa_cmp_pe'][a], P['nsa_cmp_w2'][a], P['nsa_w_out'])
            if decode:
                x, rows, win = nsa_layer_decode(*args, past['cache'], past['win'], past['page_table'], a, B)
                rows_l.append(rows)
            else:
                x, rows, win = nsa_layer_prefill(*args, a, tables, B, T)
                rows_l.append(rows)
            win_l.append(win)
        elif kind == 1:
            z = proj(x, g, P['hgrn_w_in'], layer=a)
            if decode:
                o, S = hgrn_scan(_spread(z, B, Ts), P['hgrn_lb'], i, P['hgrn_norm'][a], past['hgrn'][a], B, Ts, 1)
                o = _gather_first(o, B, Ts, R)
            else:
                o, S = hgrn_scan(z, P['hgrn_lb'], i, P['hgrn_norm'][a], jnp.zeros((B, HG_HEADS, HD, HD), F32), B, T)
            x = proj(o, g, P['hgrn_w_out'], residual=x, norm=False, layer=a)
            hg_l.append(S)
        else:
            w = P['ml_w_in']
            z = proj(x, g, jnp.swapaxes(w, 1, 2), n_out=ML_MAIN, layer=a, w_rows_out=True)
            gt = proj(x, g, jnp.pad(w[a, :, ML_MAIN:], ((0, 0), (0, LANE - 2 * ML_HEADS))))
            if decode:
                o, C, nv, m = mlstm_scan(_spread(z, B, Ts), _spread(gt, B, Ts), P['ml_b_gate'][a],
                                         past['c'][a], past['n'][a], past['m'][a], B, Ts, 1)
                o = _gather_first(o, B, Ts, R)
            else:
                o, C, nv, m = mlstm_scan(z, gt, P['ml_b_gate'][a], jnp.zeros((B, ML_HEADS, ML_DK, ML_DV), F32),
                                         jnp.zeros((B, ML_HEADS, ML_DK), F32), jnp.zeros((B, ML_HEADS), F32), B, T)
            x = proj(o, g, P['ml_w_out'], residual=x, norm=False, layer=a)
            c_l.append(C)
            n_l.append(nv)
            m_l.append(m)
        fargs = (P['norm_ffn'][i], P['ffn_w_up'], P['ffn_conv_w'], P['ffn_conv_b'], P['ffn_w_down'], i,
                 P['norm_out'])
        last = i == DEPTH - 1
        if decode:
            buf = past['ffn'][i]
            bufp = jnp.pad(buf.transpose(1, 0, 2), ((0, 0), (0, R - B), (0, 0)))
            y, sa = conv_ffn(x[None], *fargs, buf=bufp, final_norm=last)
            x = y[0]
            ffn_l.append(jnp.stack([buf[:, 1], sa[0, :B]], axis=1))
        else:
            y, sa = conv_ffn(x.reshape(B, T, D), *fargs, final_norm=last)
            x = y.reshape(B * T, D)
            ffn_l.append(sa[:, 6:8])
    out = x[:B].reshape(B, 1, D) if decode else x.reshape(B, T, D)
    if decode:
        nsa_kv = jnp.stack(rows_l, axis=2)
    else:
        nsa_kv = nsa_rows([z for z, _ in rows_l], [rot for _, rot in rows_l], T)
        nsa_kv = nsa_kv.reshape(B, T, len(rows_l), 4, NSA_KVH, HD)
    states = dict(nsa_kv=nsa_kv, nsa_win=jnp.stack(win_l, axis=0), hgrn=jnp.stack(hg_l, axis=0),
                  c=jnp.stack(c_l, axis=0), n=jnp.stack(n_l, axis=0), m=jnp.stack(m_l, axis=0),
                  ffn=jnp.stack(ffn_l, axis=0))
    return out, states
```
